```python
import jax
import jax.numpy as jnp
from jax import lax
import numpy as np

D_MODEL = 2048
BATCH = 32
SEQ = 256
DEPTH = 2
DEC_BATCH = 8
DEC_SEQ = 1024
PAST_LEN = 512

GRID_W = 64
N_MIXERS = 2
N_ATT_LAYERS = (DEPTH + 1) // 2
N_REC_LAYERS = DEPTH // 2
N_MOD = 6
NORM_EPS = 1e-6
MLA_HEADS = 16
MLA_Q_RANK = 512
MLA_KV_RANK = 512
MLA_NOPE_DIM = 128
MLA_ROPE_DIM = 64
MLA_V_DIM = 128
MLA_QK_DIM = MLA_NOPE_DIM + MLA_ROPE_DIM
ROPE_BASE = 10000.0
Q_BLOCK = 128
ML_HEADS = 8
ML_DV = D_MODEL // ML_HEADS
ML_DK = ML_DV // 2
ML_CHUNK = 64
FORGET_BIAS = 3.0
PEER_HEADS = 8
PEER_NKEYS = 128
PEER_EXPERTS = PEER_NKEYS * PEER_NKEYS
PEER_QDIM = 128
PEER_TOPK = 16
PEER_BLOCK = 128

kernel_name = 'hybrid_mla_mlstm_peer_diffusion_step'


def rms_norm(x, g):
    xf = x.astype(jnp.float32)
    y = xf * lax.rsqrt(jnp.mean(xf * xf, axis=-1, keepdims=True) + NORM_EPS)
    return (y * g.astype(jnp.float32)).astype(x.dtype)


def modulate(x, g, shift, scale):
    return rms_norm(x, g) * (1 + scale) + shift


def _rotate(xf, pos):
    nf = xf.shape[-1] // 2
    inv_freq = jnp.power(ROPE_BASE, -jnp.arange(nf, dtype=jnp.float32) / nf)
    ang = pos.astype(jnp.float32)[:, None] * inv_freq[None, :]
    cos = jnp.cos(ang)[None, :, None, :]
    sin = jnp.sin(ang)[None, :, None, :]
    x1, x2 = xf[..., :nf], xf[..., nf:]
    return jnp.concatenate([x1 * cos - x2 * sin, x1 * sin + x2 * cos], axis=-1)


def axial_rope_2d(x):
    n_tok = x.shape[1]
    rows = n_tok // GRID_W
    row = jnp.repeat(jnp.arange(rows), GRID_W)
    col = jnp.tile(jnp.arange(GRID_W), rows)
    xf = x.astype(jnp.float32)
    half = x.shape[-1] // 2
    out = jnp.concatenate([_rotate(xf[..., :half], row), _rotate(xf[..., half:], col)], axis=-1)
    return out.astype(x.dtype)


def block_attention(q, k, v):
    b, sq, h, d = q.shape
    nb = sq // Q_BLOCK
    scale = d ** -0.5
    qb = q.reshape(b, nb, Q_BLOCK, h, d).transpose(1, 0, 2, 3, 4)

    def one_block(qblk):
        s = jnp.einsum('bqhd,bkhd->bhqk', qblk, k).astype(jnp.float32) * scale
        p = jax.nn.softmax(s, axis=-1).astype(v.dtype)
        return jnp.einsum('bhqk,bkhd->bqhd', p, v)

    o = lax.map(one_block, qb)
    return o.transpose(1, 0, 2, 3, 4).reshape(b, sq, h, v.shape[-1])


def mla_queries_and_latents(h, w_in, g_q, g_kv, w_uq, g_qn):
    b, s, _ = h.shape
    a = h @ w_in
    cq = rms_norm(a[..., :MLA_Q_RANK], g_q)
    ckv = rms_norm(a[..., MLA_Q_RANK:MLA_Q_RANK + MLA_KV_RANK], g_kv)
    k_rope = a[..., MLA_Q_RANK + MLA_KV_RANK:]
    q = rms_norm((cq @ w_uq).reshape(b, s, MLA_HEADS, MLA_QK_DIM), g_qn)
    return q, ckv, k_rope


def mla_keys_values(ckv, k_rope, w_uk, w_uv, g_kn):
    b, s, _ = ckv.shape
    k_nope = (ckv @ w_uk).reshape(b, s, MLA_HEADS, MLA_NOPE_DIM)
    v = (ckv @ w_uv).reshape(b, s, MLA_HEADS, MLA_V_DIM)
    k_r = jnp.broadcast_to(k_rope[:, :, None, :], (b, s, MLA_HEADS, MLA_ROPE_DIM))
    k = rms_norm(jnp.concatenate([k_nope, k_r], axis=-1), g_kn)
    return k, v


def rope_rot_part(x):
    return jnp.concatenate([x[..., :MLA_NOPE_DIM], axial_rope_2d(x[..., MLA_NOPE_DIM:])], axis=-1)


def mla_out(o, w_o):
    b, s = o.shape[:2]
    return o.reshape(b, s, MLA_HEADS * MLA_V_DIM) @ w_o


def mla_context(h, w_in, g_q, g_kv, w_uq, g_qn, w_uk, w_uv, g_kn, w_o):
    q, ckv, k_rope = mla_queries_and_latents(h, w_in, g_q, g_kv, w_uq, g_qn)
    k, v = mla_keys_values(ckv, k_rope, w_uk, w_uv, g_kn)
    return mla_out(block_attention(q, k, v), w_o), ckv, k_rope


def mla_latent(h, ctx_ckv, ctx_krope, w_in, g_q, g_kv, w_uq, g_qn, w_uk, w_uv, g_kn, w_o):
    q, ckv, k_rope = mla_queries_and_latents(h, w_in, g_q, g_kv, w_uq, g_qn)
    k_lat, v_lat = mla_keys_values(ckv, k_rope, w_uk, w_uv, g_kn)
    k_ctx, v_ctx = mla_keys_values(ctx_ckv, ctx_krope, w_uk, w_uv, g_kn)
    q = rope_rot_part(q)
    k_lat = rope_rot_part(k_lat)
    k = jnp.concatenate([k_ctx, k_lat], axis=1)
    v = jnp.concatenate([v_ctx, v_lat], axis=1)
    return mla_out(block_attention(q, k, v), w_o)


def mlstm_chunkwise(q, k, v, log_i, log_f, mem0, norm0, m0):
    b, s, h, dk = q.shape
    dv = v.shape[-1]
    nc = s // ML_CHUNK

    def chunks(a):
        a = a.reshape((b, nc, ML_CHUNK) + a.shape[2:])
        return jnp.moveaxis(jnp.moveaxis(a, 1, 0), 3, 2)

    tril = jnp.tril(jnp.ones((ML_CHUNK, ML_CHUNK), dtype=bool))

    def step(carry, inp):
        mem, nrm, m = carry
        qc, kc, vc, ic, fc = inp
        qf, kf, vf = qc.astype(jnp.float32), kc.astype(jnp.float32), vc.astype(jnp.float32)
        bcum = jnp.cumsum(fc, axis=-1)
        d = bcum[..., :, None] - bcum[..., None, :] + ic[..., None, :]
        d = jnp.where(tril, d, -jnp.inf)
        inter = bcum + m[..., None]
        m_t = jnp.maximum(inter, jnp.max(d, axis=-1))
        w_inter = jnp.exp(inter - m_t)
        a = jnp.exp(d - m_t[..., None]) * jnp.einsum('bhtd,bhsd->bhts', qf, kf)
        num = w_inter[..., None] * jnp.einsum('bhtd,bhde->bhte', qf, mem) + jnp.einsum('bhts,bhse->bhte', a, vf)
        den = w_inter * jnp.einsum('bhtd,bhd->bht', qf, nrm) + jnp.sum(a, axis=-1)
        hc = num / jnp.maximum(jnp.abs(den), jnp.exp(-m_t))[..., None]
        m_new = m_t[..., -1]
        decay = jnp.exp(bcum[..., -1] + m - m_new)
        w_s = jnp.exp(bcum[..., -1:] - bcum + ic - m_new[..., None])
        mem_new = decay[..., None, None] * mem + jnp.einsum('bhs,bhsd,bhse->bhde', w_s, kf, vf)
        nrm_new = decay[..., None] * nrm + jnp.einsum('bhs,bhsd->bhd', w_s, kf)
        return (mem_new, nrm_new, m_new), hc

    carry0 = (mem0.astype(jnp.float32), norm0.astype(jnp.float32), m0.astype(jnp.float32))
    final, hs = lax.scan(step, carry0, (chunks(q), chunks(k), chunks(v), chunks(log_i), chunks(log_f)))
    hs = hs.transpose(1, 0, 3, 2, 4).reshape(b, s, h, dv)
    return hs, final


def mlstm_project(h, w_in, b_gate):
    b, s, _ = h.shape
    hk = ML_HEADS * ML_DK
    hv = ML_HEADS * ML_DV
    p = h @ w_in
    q = p[..., :hk].reshape(b, s, ML_HEADS, ML_DK) * (ML_DK ** -0.5)
    k = p[..., hk:2 * hk].reshape(b, s, ML_HEADS, ML_DK)
    v = p[..., 2 * hk:2 * hk + hv].reshape(b, s, ML_HEADS, ML_DV)
    o = p[..., 2 * hk + hv:2 * hk + 2 * hv]
    g = (p[..., 2 * hk + 2 * hv:].astype(jnp.float32) + b_gate.astype(jnp.float32)).reshape(b, s, 4, ML_HEADS)
    return q, k, v, o, g


def mlstm_bidir(q, k, v, g, state_fwd, state_bwd):
    h_f, st_f = mlstm_chunkwise(q, k, v, g[:, :, 0], jax.nn.log_sigmoid(g[:, :, 1]), *state_fwd)
    rev = lambda a: jnp.flip(a, axis=1)
    h_b, st_b = mlstm_chunkwise(rev(q), rev(k), rev(v), rev(g[:, :, 2]),
                                rev(jax.nn.log_sigmoid(g[:, :, 3])), *state_bwd)
    return h_f + rev(h_b), st_f, st_b


def mlstm_out(hsum, o, g_h, w_o):
    b, s = hsum.shape[:2]
    hn = rms_norm(hsum, g_h.reshape(ML_HEADS, ML_DV)).astype(o.dtype)
    y = hn.reshape(b, s, ML_HEADS * ML_DV) * jax.nn.sigmoid(o)
    return y @ w_o


def mlstm_context(h, w_in, b_gate, g_h, w_o):
    q, k, v, o, g = mlstm_project(h, w_in, b_gate)
    b = h.shape[0]
    zero = (jnp.zeros((b, ML_HEADS, ML_DK, ML_DV), jnp.float32),
            jnp.zeros((b, ML_HEADS, ML_DK), jnp.float32),
            jnp.zeros((b, ML_HEADS), jnp.float32))
    hsum, st_f, st_b = mlstm_bidir(q, k, v, g, zero, zero)
    return mlstm_out(hsum, o, g_h, w_o), st_f, st_b


def mlstm_latent(h, st_f, st_b, w_in, b_gate, g_h, w_o):
    q, k, v, o, g = mlstm_project(h, w_in, b_gate)
    hsum, _, _ = mlstm_bidir(q, k, v, g, st_f, st_b)
    return mlstm_out(hsum, o, g_h, w_o)


def peer_ffn(h, w_q, sub_keys, u_tab, v_tab):
    b, s, d = h.shape
    nb = (b * s) // PEER_BLOCK
    xb = h.reshape(nb, PEER_BLOCK, d)

    def block(xt):
        t = xt.shape[0]
        q = (xt @ w_q).reshape(t, PEER_HEADS, 2, PEER_QDIM // 2)
        sc = jnp.einsum('thpd,pkd->thpk', q, sub_keys).astype(jnp.float32)
        sv, si = lax.top_k(sc, PEER_TOPK)
        cand = (sv[:, :, 0, :, None] + sv[:, :, 1, None, :]).reshape(t, PEER_HEADS, PEER_TOPK * PEER_TOPK)
        cidx = (si[:, :, 0, :, None] * PEER_NKEYS + si[:, :, 1, None, :]).reshape(t, PEER_HEADS, PEER_TOPK * PEER_TOPK)
        best, pos = lax.top_k(cand, PEER_TOPK)
        e = jnp.take_along_axis(cidx, pos, axis=-1).reshape(t, PEER_HEADS * PEER_TOPK)
        gw = jax.nn.softmax(best, axis=-1).reshape(t, PEER_HEADS * PEER_TOPK)
        u = jnp.take(u_tab, e, axis=0)
        act = jax.nn.gelu(jnp.einsum('td,ted->te', xt, u).astype(jnp.float32), approximate=False)
        vv = jnp.take(v_tab, e, axis=0)
        return jnp.einsum('te,ted->td', (gw * act).astype(xt.dtype), vv)

    return lax.map(block, xb).reshape(b, s, d)


def setup_inputs(seed: int = 0) -> dict:
    key = jax.random.key(seed)
    k = jax.random.split(key, 30)
    f32 = jnp.float32
    D = D_MODEL
    nrm = lambda kk, shape, scale: jax.random.normal(kk, shape, f32) * scale
    gain = lambda kk, shape: 1.0 + 0.1 * jax.random.normal(kk, shape, f32)
    na, nr = N_ATT_LAYERS, N_REC_LAYERS
    ml_cols = 2 * ML_HEADS * ML_DK + 2 * ML_HEADS * ML_DV + 4 * ML_HEADS
    gate_offset = jnp.repeat(jnp.array([0.0, FORGET_BIAS, 0.0, FORGET_BIAS], f32), ML_HEADS)
    return {
        'x_prompt': nrm(k[0], (BATCH, SEQ, D), 1.0),
        'x_sample': nrm(k[1], (DEC_BATCH, DEC_SEQ, D), 1.0),
        'cache_ckv': nrm(k[2], (DEC_BATCH, na, PAST_LEN, MLA_KV_RANK), 1.0),
        'cache_krope': nrm(k[3], (DEC_BATCH, na, PAST_LEN, MLA_ROPE_DIM), 1.0),
        'state_C': nrm(k[4], (DEC_BATCH, nr, 2, ML_HEADS, ML_DK, ML_DV), ML_DK ** -0.5),
        'state_n': nrm(k[5], (DEC_BATCH, nr, 2, ML_HEADS, ML_DK), 1.0),
        'state_m': nrm(k[6], (DEC_BATCH, nr, 2, ML_HEADS), 1.0),
        'c': nrm(k[7], (DEC_BATCH, D), 1.0),
        'c_ctx': nrm(k[8], (D,), 1.0),
        'mod_w': nrm(k[9], (DEPTH, D, N_MOD * D), D ** -0.5),
        'mod_b': nrm(k[10], (DEPTH, N_MOD * D), 0.02),
        'norm_mix': gain(k[11], (DEPTH, D)),
        'norm_ffn': gain(k[12], (DEPTH, D)),
        'mla_w_in': nrm(k[13], (na, D, MLA_Q_RANK + MLA_KV_RANK + MLA_ROPE_DIM), D ** -0.5),
        'mla_g_q': gain(k[14], (na, MLA_Q_RANK)),
        'mla_g_kv': gain(k[15], (na, MLA_KV_RANK)),
        'mla_w_uq': nrm(k[16], (na, MLA_Q_RANK, MLA_HEADS * MLA_QK_DIM), MLA_Q_RANK ** -0.5),
        'mla_g_qn': gain(k[17], (na, MLA_QK_DIM)),
        'mla_w_uk': nrm(k[18], (na, MLA_KV_RANK, MLA_HEADS * MLA_NOPE_DIM), MLA_KV_RANK ** -0.5),
        'mla_w_uv': nrm(k[19], (na, MLA_KV_RANK, MLA_HEADS * MLA_V_DIM), MLA_KV_RANK ** -0.5),
        'mla_g_kn': gain(k[20], (na, MLA_QK_DIM)),
        'mla_w_o': nrm(k[21], (na, MLA_HEADS * MLA_V_DIM, D), (MLA_HEADS * MLA_V_DIM) ** -0.5),
        'ml_w_in': nrm(k[22], (nr, D, ml_cols), D ** -0.5),
        'ml_b_gate': nrm(k[23], (nr, 4 * ML_HEADS), 0.1) + gate_offset,
        'ml_g_h': gain(k[24], (nr, ML_HEADS * ML_DV)),
        'ml_w_o': nrm(k[25], (nr, ML_HEADS * ML_DV, D), (ML_HEADS * ML_DV) ** -0.5),
        'peer_w_q': nrm(k[26], (DEPTH, D, PEER_HEADS * PEER_QDIM), D ** -0.5),
        'peer_keys': nrm(k[27], (DEPTH, 2, PEER_NKEYS, PEER_QDIM // 2), (PEER_QDIM // 2) ** -0.5),
        'peer_u': nrm(k[28], (DEPTH, PEER_EXPERTS, D), D ** -0.5),
        'peer_v': nrm(k[29], (DEPTH, PEER_EXPERTS, D), PEER_HEADS ** -0.5),
    }


def reference(x_prompt, x_sample, cache_ckv, cache_krope, state_C, state_n, state_m, c, c_ctx,
              mod_w, mod_b, norm_mix, norm_ffn,
              mla_w_in, mla_g_q, mla_g_kv, mla_w_uq, mla_g_qn, mla_w_uk, mla_w_uv, mla_g_kn, mla_w_o,
              ml_w_in, ml_b_gate, ml_g_h, ml_w_o,
              peer_w_q, peer_keys, peer_u, peer_v):
    x = x_prompt
    ckv_list, kr_list, mem_list, nrm_list, m_list = [], [], [], [], []
    for layer in range(DEPTH):
        j = layer // N_MIXERS
        mod = (jax.nn.silu(c_ctx) @ mod_w[layer] + mod_b[layer]).reshape(N_MOD, D_MODEL)
        sh1, sc1, gt1, sh2, sc2, gt2 = (mod[i] for i in range(N_MOD))
        h = modulate(x, norm_mix[layer], sh1, sc1)
        if layer % N_MIXERS == 0:
            y, ckv, kr = mla_context(h, mla_w_in[j], mla_g_q[j], mla_g_kv[j], mla_w_uq[j], mla_g_qn[j],
                                     mla_w_uk[j], mla_w_uv[j], mla_g_kn[j], mla_w_o[j])
            ckv_list.append(ckv)
            kr_list.append(kr)
        else:
            y, st_f, st_b = mlstm_context(h, ml_w_in[j], ml_b_gate[j], ml_g_h[j], ml_w_o[j])
            mem_list.append(jnp.stack([st_f[0], st_b[0]], axis=1))
            nrm_list.append(jnp.stack([st_f[1], st_b[1]], axis=1))
            m_list.append(jnp.stack([st_f[2], st_b[2]], axis=1))
        x = x + gt1 * y
        h = modulate(x, norm_ffn[layer], sh2, sc2)
        x = x + gt2 * peer_ffn(h, peer_w_q[layer], peer_keys[layer], peer_u[layer], peer_v[layer])
    y_prompt = x
    new_cache_ckv = jnp.stack(ckv_list, axis=1)
    new_cache_krope = jnp.stack(kr_list, axis=1)
    new_state_C = jnp.stack(mem_list, axis=1).astype(x_prompt.dtype)
    new_state_n = jnp.stack(nrm_list, axis=1).astype(x_prompt.dtype)
    new_state_m = jnp.stack(m_list, axis=1).astype(x_prompt.dtype)

    x = x_sample
    for layer in range(DEPTH):
        j = layer // N_MIXERS
        mod = (jax.nn.silu(c) @ mod_w[layer] + mod_b[layer]).reshape(-1, 1, N_MOD, D_MODEL)
        sh1, sc1, gt1, sh2, sc2, gt2 = (mod[:, :, i] for i in range(N_MOD))
        h = modulate(x, norm_mix[layer], sh1, sc1)
        if layer % N_MIXERS == 0:
            y = mla_latent(h, cache_ckv[:, j], cache_krope[:, j], mla_w_in[j], mla_g_q[j], mla_g_kv[j],
                           mla_w_uq[j], mla_g_qn[j], mla_w_uk[j], mla_w_uv[j], mla_g_kn[j], mla_w_o[j])
        else:
            st_f = (state_C[:, j, 0], state_n[:, j, 0], state_m[:, j, 0])
            st_b = (state_C[:, j, 1], state_n[:, j, 1], state_m[:, j, 1])
            y = mlstm_latent(h, st_f, st_b, ml_w_in[j], ml_b_gate[j], ml_g_h[j], ml_w_o[j])
        x = x + gt1 * y
        h = modulate(x, norm_ffn[layer], sh2, sc2)
        x = x + gt2 * peer_ffn(h, peer_w_q[layer], peer_keys[layer], peer_u[layer], peer_v[layer])
    y_sample = x
    return (y_prompt, y_sample, new_cache_ckv, new_cache_krope, new_state_C, new_state_n, new_state_m)
```

```python
import functools

import numpy as np
import jax
import jax.numpy as jnp
from jax import lax
from jax.experimental import pallas as pl
from jax.experimental.pallas import tpu as pltpu

F32 = jnp.float32
BF16 = jnp.bfloat16
I32 = jnp.int32

D_MODEL = 2048
BATCH, SEQ = 32, 256
DEC_BATCH, DEC_SEQ = 8, 1024
PAST_LEN = 512
GRID_W = 64
N_MOD = 6
NORM_EPS = 1e-6
MLA_HEADS = 16
MLA_Q_RANK = 512
MLA_KV_RANK = 512
MLA_NOPE = 128
MLA_ROPE = 64
MLA_QK_DIM = MLA_NOPE + MLA_ROPE
MLA_HEAD_PAD = 256
ROPE_BASE = 10000.0
ML_HEADS = 8
ML_DV = D_MODEL // ML_HEADS
ML_DK = ML_DV // 2
ML_CHUNK = 64
PEER_HEADS = 8
PEER_NKEYS = 128
PEER_QDIM = 128
PEER_TOPK = 16
PEER_SEL = PEER_HEADS * PEER_TOPK

T_CTX = BATCH * SEQ
T_LAT = DEC_BATCH * DEC_SEQ
T_ALL = T_CTX + T_LAT
N_GROUPS = 1 + DEC_BATCH
GROUP_PAD = 16
ROW_TILE = 256

VMEM_LIMIT_BYTES = 56 * 1024 * 1024
LANES = 128
SUBLANES = 8

NEG_INF = float("-inf")


def _params(*sem):
    return pltpu.CompilerParams(dimension_semantics=sem, vmem_limit_bytes=VMEM_LIMIT_BYTES)


def _mod_kernel(c_ref, w_ref, b_ref, o_ref):
    c = c_ref[...]
    a = (c * jax.nn.sigmoid(c)).astype(BF16)
    o_ref[0] = jnp.dot(a, w_ref[0].astype(BF16), preferred_element_type=F32) + b_ref[0]


def mod_vectors(cvec, mod_w, mod_b):
    depth, d, n = mod_w.shape
    tn = 1024
    return pl.pallas_call(
        _mod_kernel,
        grid=(depth, n // tn),
        in_specs=[
            pl.BlockSpec((GROUP_PAD, d), lambda l, j: (0, 0)),
            pl.BlockSpec((1, d, tn), lambda l, j: (l, 0, j)),
            pl.BlockSpec((1, 1, tn), lambda l, j: (l, 0, j)),
        ],
        out_specs=pl.BlockSpec((1, GROUP_PAD, tn), lambda l, j: (l, 0, j)),
        out_shape=jax.ShapeDtypeStruct((depth, GROUP_PAD, n), F32),
        compiler_params=_params("arbitrary", "arbitrary"),
        name="mod_vectors",
    )(cvec, mod_w, mod_b.reshape(depth, 1, n))


def _group_of_tile(i):
    ctx_tiles = T_CTX // ROW_TILE
    tiles_per_lat = DEC_SEQ // ROW_TILE
    return jnp.where(i < ctx_tiles, 0, 1 + (i - ctx_tiles) // tiles_per_lat)


def _resmod_kernel(*refs, has_res, has_mod):
    refs = list(refs)
    x_ref = refs.pop(0)
    x = x_ref[...]
    if has_res:
        y_ref = refs.pop(0)
        gate_ref = refs.pop(0)
        x = x + gate_ref[0] * y_ref[...]
    if has_mod:
        g_ref, sh_ref, sc_ref = refs.pop(0), refs.pop(0), refs.pop(0)
    if has_res:
        xo_ref = refs.pop(0)
        xo_ref[...] = x
    if has_mod:
        h_ref = refs.pop(0)
        ms = jnp.mean(x * x, axis=-1, keepdims=True)
        yn = x * lax.rsqrt(ms + NORM_EPS) * g_ref[...]
        h_ref[...] = (yn * (1.0 + sc_ref[0]) + sh_ref[0]).astype(h_ref.dtype)


def resid_modulate(x, modrows, *, y=None, gate=None, norm_g=None, shift=None, scale=None):
    t, d = x.shape
    has_res = y is not None
    has_mod = norm_g is not None
    row_spec = pl.BlockSpec((ROW_TILE, d), lambda i: (i, 0))

    def mod_spec(layer_k):
        layer, k = layer_k
        return pl.BlockSpec((1, 1, d), lambda i: ((layer * GROUP_PAD + _group_of_tile(i)) * N_MOD + k, 0, 0))

    args, in_specs = [x], [row_spec]
    if has_res:
        args += [y, modrows]
        in_specs += [row_spec, mod_spec(gate)]
    if has_mod:
        args += [norm_g.reshape(1, d), modrows, modrows]
        in_specs += [pl.BlockSpec((1, d), lambda i: (0, 0)), mod_spec(shift), mod_spec(scale)]
    out_shape, out_specs = [], []
    if has_res:
        out_shape.append(jax.ShapeDtypeStruct((t, d), F32))
        out_specs.append(row_spec)
    if has_mod:
        out_shape.append(jax.ShapeDtypeStruct((t, d), BF16))
        out_specs.append(row_spec)
    outs = pl.pallas_call(
        functools.partial(_resmod_kernel, has_res=has_res, has_mod=has_mod),
        grid=(t // ROW_TILE,),
        in_specs=in_specs,
        out_specs=out_specs,
        out_shape=out_shape,
        compiler_params=_params("arbitrary"),
        name="resid_modulate",
    )(*args)
    return outs


def _mm_kernel(x_ref, w_ref, o_ref):
    o_ref[...] = jnp.dot(x_ref[...].astype(BF16), w_ref[...], preferred_element_type=F32).astype(o_ref.dtype)


def matmul(x, w, *, tm, tn, out_dtype):
    m, k = x.shape
    n = w.shape[1]
    return pl.pallas_call(
        _mm_kernel,
        grid=(m // tm, n // tn),
        in_specs=[pl.BlockSpec((tm, k), lambda i, j: (i, 0)), pl.BlockSpec((k, tn), lambda i, j: (0, j))],
        out_specs=pl.BlockSpec((tm, tn), lambda i, j: (i, j)),
        out_shape=jax.ShapeDtypeStruct((m, n), out_dtype),
        compiler_params=_params("arbitrary", "arbitrary"),
        name="matmul",
    )(x, w)


def _rms(x, n):
    return lax.rsqrt(jnp.sum(x * x, axis=-1, keepdims=True) / n + NORM_EPS)


def _mla_in_kernel(h_ref, w_ref, gq_ref, gkv_ref, cq_ref, ckv_ref, kr_ref):
    a = jnp.dot(h_ref[...], w_ref[...], preferred_element_type=F32)
    cq = a[:, :MLA_Q_RANK]
    ckv = a[:, MLA_Q_RANK:MLA_Q_RANK + MLA_KV_RANK]
    cq_ref[...] = (cq * _rms(cq, MLA_Q_RANK) * gq_ref[...]).astype(cq_ref.dtype)
    ckv_ref[...] = ckv * _rms(ckv, MLA_KV_RANK) * gkv_ref[...]
    kr_ref[...] = a[:, MLA_Q_RANK + MLA_KV_RANK:]


def mla_in_proj(h, w_ext, g_q, g_kv):
    t, d = h.shape
    n = w_ext.shape[1]
    tm = 512
    return pl.pallas_call(
        _mla_in_kernel,
        grid=(t // tm,),
        in_specs=[
            pl.BlockSpec((tm, d), lambda i: (i, 0)),
            pl.BlockSpec((d, n), lambda i: (0, 0)),
            pl.BlockSpec((1, MLA_Q_RANK), lambda i: (0, 0)),
            pl.BlockSpec((1, MLA_KV_RANK), lambda i: (0, 0)),
        ],
        out_specs=[
            pl.BlockSpec((tm, MLA_Q_RANK), lambda i: (i, 0)),
            pl.BlockSpec((tm, MLA_KV_RANK), lambda i: (i, 0)),
            pl.BlockSpec((tm, LANES), lambda i: (i, 0)),
        ],
        out_shape=[
            jax.ShapeDtypeStruct((t, MLA_Q_RANK), BF16),
            jax.ShapeDtypeStruct((t, MLA_KV_RANK), F32),
            jax.ShapeDtypeStruct((t, LANES), F32),
        ],
        compiler_params=_params("arbitrary"),
        name="mla_in_proj",
    )(h, w_ext, g_q.reshape(1, -1), g_kv.reshape(1, -1))


def _rope_block(x, c, s1, s2):
    return x * c + pltpu.roll(x, 96, 1) * s1 + pltpu.roll(x, 32, 1) * s2


def _q_up_kernel(cq_ref, w_ref, g_ref, c_ref, s1_ref, s2_ref, q_ref, *, heads):
    a = jnp.dot(cq_ref[...], w_ref[...], preferred_element_type=F32)
    g = g_ref[...]
    for hh in range(heads):
        base = hh * MLA_HEAD_PAD
        nope = a[:, base:base + MLA_NOPE]
        rp = a[:, base + MLA_NOPE:base + MLA_HEAD_PAD]
        ss = jnp.sum(nope * nope, axis=-1, keepdims=True) + jnp.sum(rp * rp, axis=-1, keepdims=True)
        r = lax.rsqrt(ss / MLA_QK_DIM + NORM_EPS)
        xr = _rope_block(rp * r * g[:, MLA_NOPE:], c_ref[...], s1_ref[...], s2_ref[...])
        q_ref[:, base:base + MLA_NOPE] = (nope * r * g[:, :MLA_NOPE]).astype(q_ref.dtype)
        q_ref[:, base + MLA_NOPE:base + MLA_HEAD_PAD] = xr.astype(q_ref.dtype)


def mla_q_up(cq, w_uq_ext, g_qn_ext, rope_c, rope_s1, rope_s2):
    t, r = cq.shape
    n = w_uq_ext.shape[1]
    tm, heads = 512, 2
    tn = heads * MLA_HEAD_PAD
    tab = pl.BlockSpec((tm, LANES), lambda i, j: (i, 0))
    return pl.pallas_call(
        functools.partial(_q_up_kernel, heads=heads),
        grid=(t // tm, n // tn),
        in_specs=[
            pl.BlockSpec((tm, r), lambda i, j: (i, 0)),
            pl.BlockSpec((r, tn), lambda i, j: (0, j)),
            pl.BlockSpec((1, MLA_HEAD_PAD), lambda i, j: (0, 0)),
            tab, tab, tab,
        ],
        out_specs=pl.BlockSpec((tm, tn), lambda i, j: (i, j)),
        out_shape=jax.ShapeDtypeStruct((t, n), BF16),
        compiler_params=_params("arbitrary", "arbitrary"),
        name="mla_q_up",
    )(cq, w_uq_ext, g_qn_ext, rope_c, rope_s1, rope_s2)


def _kv_up_kernel(ckv_ref, kr_ref, wk_ref, wv_ref, g_ref, c_ref, s1_ref, s2_ref, k_ref, v_ref, *, heads):
    ckv = ckv_ref[...].astype(BF16)
    kn = jnp.dot(ckv, wk_ref[...], preferred_element_type=F32)
    v_ref[...] = jnp.dot(ckv, wv_ref[...], preferred_element_type=F32).astype(v_ref.dtype)
    g = g_ref[...]
    lane = lax.broadcasted_iota(I32, (1, LANES), 1)
    kr = jnp.where(lane < MLA_ROPE, kr_ref[...], 0.0)
    ss_r = jnp.sum(kr * kr, axis=-1, keepdims=True)
    krot = _rope_block(kr * g[:, MLA_NOPE:], c_ref[...], s1_ref[...], s2_ref[...])
    for hh in range(heads):
        nope = kn[:, hh * MLA_NOPE:(hh + 1) * MLA_NOPE]
        r = lax.rsqrt((jnp.sum(nope * nope, axis=-1, keepdims=True) + ss_r) / MLA_QK_DIM + NORM_EPS)
        base = hh * MLA_HEAD_PAD
        k_ref[:, base:base + MLA_NOPE] = (nope * r * g[:, :MLA_NOPE]).astype(k_ref.dtype)
        k_ref[:, base + MLA_NOPE:base + MLA_HEAD_PAD] = (krot * r).astype(k_ref.dtype)


def mla_kv_up(ckv, kr, w_uk, w_uv, g_kn_ext, rope_c, rope_s1, rope_s2):
    t, r = ckv.shape
    tm, heads = 512, 2
    tab = pl.BlockSpec((tm, LANES), lambda i, j: (i, 0))
    return pl.pallas_call(
        functools.partial(_kv_up_kernel, heads=heads),
        grid=(t // tm, MLA_HEADS // heads),
        in_specs=[
            pl.BlockSpec((tm, r), lambda i, j: (i, 0)),
            tab,
            pl.BlockSpec((r, heads * MLA_NOPE), lambda i, j: (0, j)),
            pl.BlockSpec((r, heads * MLA_NOPE), lambda i, j: (0, j)),
            pl.BlockSpec((1, MLA_HEAD_PAD), lambda i, j: (0, 0)),
            tab, tab, tab,
        ],
        out_specs=[
            pl.BlockSpec((tm, heads * MLA_HEAD_PAD), lambda i, j: (i, j)),
            pl.BlockSpec((tm, heads * MLA_NOPE), lambda i, j: (i, j)),
        ],
        out_shape=[
            jax.ShapeDtypeStruct((t, MLA_HEADS * MLA_HEAD_PAD), BF16),
            jax.ShapeDtypeStruct((t, MLA_HEADS * MLA_NOPE), BF16),
        ],
        compiler_params=_params("arbitrary", "arbitrary"),
        name="mla_kv_up",
    )(ckv, kr, w_uk, w_uv, g_kn_ext, rope_c, rope_s1, rope_s2)


def _attn_kernel(*refs, nseg):
    q_ref = refs[0]
    k_refs = refs[1:1 + nseg]
    v_refs = refs[1 + nseg:1 + 2 * nseg]
    o_ref = refs[1 + 2 * nseg]
    scale = MLA_QK_DIM ** -0.5
    q = q_ref[...]
    nt = (((1,), (1,)), ((), ()))
    s = [lax.dot_general(q, k[...], nt, preferred_element_type=F32) * scale for k in k_refs]
    m = functools.reduce(jnp.maximum, [jnp.max(x, axis=-1, keepdims=True) for x in s])
    e = [jnp.exp(x - m) for x in s]
    inv = 1.0 / functools.reduce(lambda a, b: a + b, [jnp.sum(x, axis=-1, keepdims=True) for x in e])
    o = functools.reduce(
        lambda a, b: a + b,
        [jnp.dot((x * inv).astype(BF16), v[...], preferred_element_type=F32) for x, v in zip(e, v_refs)])
    o_ref[...] = o.astype(o_ref.dtype)


def mla_attention(q, k, v, *, q_row0, n_batch, s_q, segs):
    tq = 256
    nq = s_q // tq
    nseg = len(segs)
    q_blk0 = q_row0 // tq
    in_specs = [pl.BlockSpec((tq, MLA_HEAD_PAD), lambda b, h, i: (q_blk0 + b * nq + i, h))]
    for row0, length in segs:
        in_specs.append(pl.BlockSpec((length, MLA_HEAD_PAD), lambda b, h, i, o=row0 // length: (o + b, h)))
    for row0, length in segs:
        in_specs.append(pl.BlockSpec((length, MLA_NOPE), lambda b, h, i, o=row0 // length: (o + b, h)))
    return pl.pallas_call(
        functools.partial(_attn_kernel, nseg=nseg),
        grid=(n_batch, MLA_HEADS, nq),
        in_specs=in_specs,
        out_specs=pl.BlockSpec((tq, MLA_NOPE), lambda b, h, i: (b * nq + i, h)),
        out_shape=jax.ShapeDtypeStruct((n_batch * s_q, MLA_HEADS * MLA_NOPE), BF16),
        compiler_params=_params("arbitrary", "arbitrary", "arbitrary"),
        name="mla_attention",
    )(q, *([k] * nseg), *([v] * nseg))


def _rope_tables():
    nf = MLA_ROPE // 4
    inv_freq = jnp.power(ROPE_BASE, -jnp.arange(nf, dtype=F32) / nf)
    tok = jnp.arange(DEC_SEQ)
    row = (tok // GRID_W).astype(F32)[:, None] * inv_freq[None, :]
    col = (tok % GRID_W).astype(F32)[:, None] * inv_freq[None, :]
    ang = jnp.concatenate([row, col], axis=-1)
    cos, sin = jnp.cos(ang), jnp.sin(ang)
    z32 = jnp.zeros_like(cos)
    z64 = jnp.zeros((DEC_SEQ, 64), F32)
    c = jnp.concatenate([cos, cos, z64], axis=-1)
    s1 = jnp.concatenate([-sin, z32, z64], axis=-1)
    s2 = jnp.concatenate([z32, sin, z64], axis=-1)
    ident_c = jnp.concatenate([jnp.ones((1, 64), F32), jnp.zeros((1, 64), F32)], axis=-1)

    def full(lat, ident):
        n_cache = DEC_BATCH * PAST_LEN
        return jnp.concatenate([
            jnp.broadcast_to(ident, (T_CTX, LANES)),
            jnp.tile(lat, (DEC_BATCH, 1)),
            jnp.broadcast_to(ident, (n_cache, LANES)),
        ], axis=0)

    zero = jnp.zeros((1, LANES), F32)
    return full(c, ident_c), full(s1, zero), full(s2, zero)


_ROPE_PERM = np.concatenate([np.arange(0, 16), np.arange(32, 48), np.arange(16, 32), np.arange(48, 64)])


def mla_layer(h, cache_ckv, cache_krope, w_in, g_q, g_kv, w_uq, g_qn, w_uk, w_uv, g_kn, w_o):
    d = h.shape[1]
    perm = _ROPE_PERM
    n_lat = MLA_Q_RANK + MLA_KV_RANK
    w_in_ext = jnp.concatenate([w_in[:, :n_lat], w_in[:, n_lat:][:, perm], w_in[:, n_lat:]], axis=1).astype(BF16)
    cq, ckv, kr = mla_in_proj(h, w_in_ext, g_q, g_kv)

    w3 = w_uq.reshape(MLA_Q_RANK, MLA_HEADS, MLA_QK_DIM)
    w_uq_ext = jnp.concatenate([
        w3[:, :, :MLA_NOPE], w3[:, :, MLA_NOPE:][:, :, perm],
        jnp.zeros((MLA_Q_RANK, MLA_HEADS, MLA_HEAD_PAD - MLA_QK_DIM), w_uq.dtype)], axis=-1)
    w_uq_ext = w_uq_ext.reshape(MLA_Q_RANK, MLA_HEADS * MLA_HEAD_PAD).astype(BF16)

    def gain_ext(g):
        return jnp.concatenate([g[:MLA_NOPE], g[MLA_NOPE:][perm],
                                jnp.zeros((MLA_HEAD_PAD - MLA_QK_DIM,), g.dtype)]).reshape(1, MLA_HEAD_PAD)

    rope_c, rope_s1, rope_s2 = _rope_tables()
    q = mla_q_up(cq, w_uq_ext, gain_ext(g_qn), rope_c[:T_ALL], rope_s1[:T_ALL], rope_s2[:T_ALL])

    n_cache = DEC_BATCH * PAST_LEN
    ckv_all = jnp.concatenate([ckv, cache_ckv.reshape(n_cache, MLA_KV_RANK)], axis=0)
    kr_cache = cache_krope.reshape(n_cache, MLA_ROPE)
    kr_cache = jnp.concatenate([kr_cache[:, perm], kr_cache], axis=1)
    kr_all = jnp.concatenate([kr, kr_cache], axis=0)
    k, v = mla_kv_up(ckv_all, kr_all, w_uk.astype(BF16), w_uv.astype(BF16), gain_ext(g_kn),
                     rope_c, rope_s1, rope_s2)

    o_ctx = mla_attention(q, k, v, q_row0=0, n_batch=BATCH, s_q=SEQ, segs=[(0, SEQ)])
    o_lat = mla_attention(q, k, v, q_row0=T_CTX, n_batch=DEC_BATCH, s_q=DEC_SEQ,
                          segs=[(T_ALL, PAST_LEN), (T_CTX, DEC_SEQ)])
    o = jnp.concatenate([o_ctx, o_lat], axis=0)
    y = matmul(o, w_o.astype(BF16), tm=1024, tn=1024, out_dtype=F32)
    new_ckv = ckv[:T_CTX].reshape(BATCH, 1, SEQ, MLA_KV_RANK)
    new_krope = kr[:T_CTX, MLA_ROPE:].reshape(BATCH, 1, SEQ, MLA_ROPE)
    return y, new_ckv, new_krope


def _log_sigmoid(x):
    return jnp.minimum(x, 0.0) - jnp.log(1.0 + jnp.exp(-jnp.abs(x)))


def _gates_kernel(h_ref, wg_ref, wgt_ref, b_ref, bt_ref, gc_ref, gr_ref):
    h = h_ref[...]
    gc = jnp.dot(h, wg_ref[...], preferred_element_type=F32) + b_ref[...]
    gr = lax.dot_general(wgt_ref[...], h, (((1,), (1,)), ((), ())), preferred_element_type=F32) + bt_ref[...]
    lane = lax.broadcasted_iota(I32, (1, LANES), 1)
    is_f = ((lane >= ML_HEADS) & (lane < 2 * ML_HEADS)) | ((lane >= 3 * ML_HEADS) & (lane < 4 * ML_HEADS))
    gc_ref[...] = jnp.where(is_f, _log_sigmoid(gc), gc)
    row = lax.broadcasted_iota(I32, (4 * ML_HEADS, 1), 0)
    is_fr = ((row >= ML_HEADS) & (row < 2 * ML_HEADS)) | ((row >= 3 * ML_HEADS) & (row < 4 * ML_HEADS))
    gr_ref[...] = jnp.where(is_fr, _log_sigmoid(gr), gr)


def mlstm_gates(h, w_g, b_gate):
    t, d = h.shape
    ng = 4 * ML_HEADS
    tm = 512
    wg = jnp.concatenate([w_g, jnp.zeros((d, LANES - ng), w_g.dtype)], axis=1).astype(BF16)
    wgt = w_g.T.astype(BF16)
    b = jnp.concatenate([b_gate, jnp.zeros((LANES - ng,), b_gate.dtype)]).reshape(1, LANES)
    bt = b_gate.reshape(ng, 1)
    return pl.pallas_call(
        _gates_kernel,
        grid=(t // tm,),
        in_specs=[
            pl.BlockSpec((tm, d), lambda i: (i, 0)),
            pl.BlockSpec((d, LANES), lambda i: (0, 0)),
            pl.BlockSpec((ng, d), lambda i: (0, 0)),
            pl.BlockSpec((1, LANES), lambda i: (0, 0)),
            pl.BlockSpec((ng, 1), lambda i: (0, 0)),
        ],
        out_specs=[pl.BlockSpec((tm, LANES), lambda i: (i, 0)), pl.BlockSpec((ng, tm), lambda i: (0, i))],
        out_shape=[jax.ShapeDtypeStruct((t, LANES), F32), jax.ShapeDtypeStruct((ng, t), F32)],
        compiler_params=_params("arbitrary"),
        name="mlstm_gates",
    )(h, wg, wgt, b, bt)


def _mlstm_kernel(*refs, nc, has_init, emit_state):
    refs = list(refs)
    q_ref, k_ref, v_ref, o_ref, gc_ref, gr_ref, gh_ref = [refs.pop(0) for _ in range(7)]
    if has_init:
        c0_ref, n0_ref, m0_ref = [refs.pop(0) for _ in range(3)]
    y_ref = refs.pop(0)
    if emit_state:
        cf_ref, nf_ref, mf_ref = [refs.pop(0) for _ in range(3)]
    mem_s, nrm_s, m_s, hs_s = refs

    L = ML_CHUNK
    tt = lax.broadcasted_iota(I32, (L, L), 0)
    ss = lax.broadcasted_iota(I32, (L, L), 1)
    q_scale = ML_DK ** -0.5
    nt = (((1,), (1,)), ((), ()))
    tn = (((0,), (0,)), ((), ()))

    for d in range(2):
        causal = (ss <= tt) if d == 0 else (ss >= tt)
        causal_t = (tt <= ss) if d == 0 else (tt >= ss)
        last = L - 1 if d == 0 else 0
        if has_init:
            mem_s[...] = c0_ref[0, d, 0]
            nrm_s[...] = n0_ref[0, d, 0]
            m_s[...] = m0_ref[0, d, 0]
        else:
            mem_s[...] = jnp.zeros_like(mem_s)
            nrm_s[...] = jnp.zeros_like(nrm_s)
            m_s[...] = jnp.zeros_like(m_s)

        def chunk(ci, carry, d=d, causal=causal, causal_t=causal_t, last=last):
            c = ci if d == 0 else nc - 1 - ci
            gcol = gc_ref[0, 0, c]
            grow = gr_ref[0, 0, c]
            i_col = gcol[:, 2 * d:2 * d + 1]
            f_col = gcol[:, 2 * d + 1:2 * d + 2]
            i_row = grow[2 * d:2 * d + 1, :]
            f_row = grow[2 * d + 1:2 * d + 2, :]
            cum_col = jnp.sum(jnp.where(causal, jnp.broadcast_to(f_row, (L, L)), 0.0), axis=1, keepdims=True)
            cum_row = jnp.sum(jnp.where(causal_t, jnp.broadcast_to(f_col, (L, L)), 0.0), axis=0, keepdims=True)
            total = cum_col[last:last + 1, :]
            m_prev = m_s[:, 0:1]
            dmat = jnp.where(causal, cum_col - cum_row + i_row, NEG_INF)
            inter = cum_col + m_prev
            m_t = jnp.maximum(inter, jnp.max(dmat, axis=1, keepdims=True))
            w_inter = jnp.exp(inter - m_t)
            rows = pl.ds(pl.multiple_of(c * L, L), L)
            qf = q_ref[rows, :] * q_scale
            kf = k_ref[rows, :]
            vb = v_ref[rows, :].astype(BF16)
            qb = qf.astype(BF16)
            qk = lax.dot_general(qb, kf.astype(BF16), nt, preferred_element_type=F32)
            a = jnp.exp(dmat - m_t) * qk
            mem = mem_s[...]
            nrm = nrm_s[...]
            num = (w_inter * jnp.dot(qb, mem.astype(BF16), preferred_element_type=F32)
                   + jnp.dot(a.astype(BF16), vb, preferred_element_type=F32))
            den = w_inter * jnp.sum(qf * nrm, axis=1, keepdims=True) + jnp.sum(a, axis=1, keepdims=True)
            hc = num / jnp.maximum(jnp.abs(den), jnp.exp(-m_t))
            m_new = m_t[last:last + 1, :]
            decay = jnp.exp(total + m_prev - m_new)
            w_s = jnp.exp(total - cum_col + i_col - m_new)
            wk = w_s * kf
            mem_s[...] = decay * mem + lax.dot_general(wk.astype(BF16), vb, tn, preferred_element_type=F32)
            nrm_s[...] = decay * nrm + jnp.sum(wk, axis=0, keepdims=True)
            m_s[...] = jnp.broadcast_to(m_new, m_s.shape)
            if d == 0:
                hs_s[rows, :] = hc
            else:
                hs_s[rows, :] = hs_s[rows, :] + hc
            return carry

        lax.fori_loop(0, nc, chunk, 0)
        if emit_state:
            cf_ref[0, d, 0] = mem_s[...]
            nf_ref[0, d, 0] = nrm_s[...]
            mf_ref[0, d, 0] = m_s[...]

    hs = hs_s[...]
    hn = hs * lax.rsqrt(jnp.mean(hs * hs, axis=-1, keepdims=True) + NORM_EPS) * gh_ref[...]
    y_ref[...] = (hn * jax.nn.sigmoid(o_ref[...])).astype(y_ref.dtype)


def mlstm_scan(p, gcol, grow, g_h, *, row0, n_batch, seq, state=None, emit_state=False):
    nc = seq // ML_CHUNK
    rb0 = row0 // seq
    hk = ML_HEADS * ML_DK
    has_init = state is not None
    in_specs = [
        pl.BlockSpec((seq, ML_DK), lambda b, h: (rb0 + b, h)),
        pl.BlockSpec((seq, ML_DK), lambda b, h: (rb0 + b, ML_HEADS + h)),
        pl.BlockSpec((seq, ML_DV), lambda b, h: (rb0 + b, 2 * hk // ML_DV + h)),
        pl.BlockSpec((seq, ML_DV), lambda b, h: (rb0 + b, 2 * hk // ML_DV + ML_HEADS + h)),
        pl.BlockSpec((1, 1, nc, ML_CHUNK, 4), lambda b, h: (b, h, 0, 0, 0)),
        pl.BlockSpec((1, 1, nc, 4, ML_CHUNK), lambda b, h: (b, h, 0, 0, 0)),
        pl.BlockSpec((1, ML_DV), lambda b, h: (0, h)),
    ]
    args = [p, p, p, p, gcol, grow, g_h.reshape(1, -1)]
    c_spec = pl.BlockSpec((1, 2, 1, ML_DK, ML_DV), lambda b, h: (b, 0, h, 0, 0))
    n_spec = pl.BlockSpec((1, 2, 1, 1, ML_DK), lambda b, h: (b, 0, h, 0, 0))
    m_spec = pl.BlockSpec((1, 2, 1, 1, LANES), lambda b, h: (b, 0, h, 0, 0))
    if has_init:
        in_specs += [c_spec, n_spec, m_spec]
        args += list(state)
    out_specs = [pl.BlockSpec((seq, ML_DV), lambda b, h: (b, h))]
    out_shape = [jax.ShapeDtypeStruct((n_batch * seq, ML_HEADS * ML_DV), BF16)]
    if emit_state:
        out_specs += [c_spec, n_spec, m_spec]
        out_shape += [
            jax.ShapeDtypeStruct((n_batch, 2, ML_HEADS, ML_DK, ML_DV), F32),
            jax.ShapeDtypeStruct((n_batch, 2, ML_HEADS, 1, ML_DK), F32),
            jax.ShapeDtypeStruct((n_batch, 2, ML_HEADS, 1, LANES), F32),
        ]
    return pl.pallas_call(
        functools.partial(_mlstm_kernel, nc=nc, has_init=has_init, emit_state=emit_state),
        grid=(n_batch, ML_HEADS),
        in_specs=in_specs,
        out_specs=out_specs,
        out_shape=out_shape,
        scratch_shapes=[
            pltpu.VMEM((ML_DK, ML_DV), F32),
            pltpu.VMEM((1, ML_DK), F32),
            pltpu.VMEM((1, LANES), F32),
            pltpu.VMEM((seq, ML_DV), F32),
        ],
        compiler_params=_params("arbitrary", "arbitrary"),
        name="mlstm_scan",
    )(*args)


def _gate_layouts(gc, gr, row0, n_batch, seq):
    nc = seq // ML_CHUNK
    n = n_batch * seq
    gcol = gc[row0:row0 + n, :4 * ML_HEADS].reshape(n_batch, nc, ML_CHUNK, 4, ML_HEADS).transpose(0, 4, 1, 2, 3)
    grow = gr[:, row0:row0 + n].reshape(4, ML_HEADS, n_batch, nc, ML_CHUNK).transpose(2, 1, 3, 0, 4)
    return gcol, grow


def mlstm_layer(h, state_c, state_n, state_m, w_in, b_gate, g_h, w_o):
    hk = ML_HEADS * ML_DK
    hv = ML_HEADS * ML_DV
    n_main = 2 * hk + 2 * hv
    p = matmul(h, w_in[:, :n_main].astype(BF16), tm=1024, tn=1024 + 512, out_dtype=F32)
    gc, gr = mlstm_gates(h, w_in[:, n_main:], b_gate)

    gcol, grow = _gate_layouts(gc, gr, 0, BATCH, SEQ)
    y_ctx, cf, nf, mf = mlstm_scan(p, gcol, grow, g_h, row0=0, n_batch=BATCH, seq=SEQ, emit_state=True)

    gcol, grow = _gate_layouts(gc, gr, T_CTX, DEC_BATCH, DEC_SEQ)
    c0 = state_c[:, 0]
    n0 = state_n[:, 0].reshape(DEC_BATCH, 2, ML_HEADS, 1, ML_DK)
    m0 = jnp.broadcast_to(state_m[:, 0].reshape(DEC_BATCH, 2, ML_HEADS, 1, 1), (DEC_BATCH, 2, ML_HEADS, 1, LANES))
    (y_lat,) = mlstm_scan(p, gcol, grow, g_h, row0=T_CTX, n_batch=DEC_BATCH, seq=DEC_SEQ, state=(c0, n0, m0))

    y = matmul(jnp.concatenate([y_ctx, y_lat], axis=0), w_o.astype(BF16), tm=1024, tn=1024, out_dtype=F32)
    new_c = cf.reshape(BATCH, 1, 2, ML_HEADS, ML_DK, ML_DV)
    new_n = nf.reshape(BATCH, 1, 2, ML_HEADS, ML_DK)
    new_m = mf[..., 0, 0].reshape(BATCH, 1, 2, ML_HEADS)
    return y, new_c, new_n, new_m


def _topk_rows(x, payload, n_out):
    rows = x.shape[0]
    iota = lax.broadcasted_iota(I32, x.shape, 0)
    vals, outs = [], []
    for _ in range(n_out):
        m = jnp.max(x, axis=0, keepdims=True)
        pos = jnp.min(jnp.where(x == m, iota, rows), axis=0, keepdims=True)
        sel = iota == pos
        vals.append(m)
        outs.append(jnp.max(jnp.where(sel, payload, -1), axis=0, keepdims=True))
        x = jnp.where(sel, NEG_INF, x)
    return jnp.concatenate(vals, axis=0), jnp.concatenate(outs, axis=0)


def _peer_topk_kernel(q_ref, keys_ref, e_ref, gw_ref):
    tt = q_ref.shape[0]
    half = PEER_QDIM // 2
    nt = (((1,), (1,)), ((), ()))
    key_iota = lax.broadcasted_iota(I32, (PEER_NKEYS, tt), 0)
    e_parts, g_parts = [], []
    for hd in range(PEER_HEADS):
        sv, si = [], []
        for p in range(2):
            lo = (hd * 2 + p) * half
            qs = q_ref[:, lo:lo + half].astype(BF16)
            sc = lax.dot_general(keys_ref[p], qs, nt, preferred_element_type=F32)
            v, i = _topk_rows(sc, key_iota, PEER_TOPK)
            sv.append(v)
            si.append(i)
        cand = jnp.concatenate([sv[0][i:i + 1, :] + sv[1] for i in range(PEER_TOPK)], axis=0)
        cidx = jnp.concatenate([si[0][i:i + 1, :] * PEER_NKEYS + si[1] for i in range(PEER_TOPK)], axis=0)
        best, eidx = _topk_rows(cand, cidx, PEER_TOPK)
        ex = jnp.exp(best - best[0:1, :])
        g_parts.append(ex / jnp.sum(ex, axis=0, keepdims=True))
        e_parts.append(eidx)
    e_ref[...] = jnp.concatenate(e_parts, axis=0).T
    gw_ref[...] = jnp.concatenate(g_parts, axis=0).T


def peer_topk(q, keys):
    t, n = q.shape
    tt = 128
    return pl.pallas_call(
        _peer_topk_kernel,
        grid=(t // tt,),
        in_specs=[pl.BlockSpec((tt, n), lambda i: (i, 0)),
                  pl.BlockSpec(keys.shape, lambda i: (0, 0, 0))],
        out_specs=[pl.BlockSpec((tt, PEER_SEL), lambda i: (i, 0)), pl.BlockSpec((tt, PEER_SEL), lambda i: (i, 0))],
        out_shape=[jax.ShapeDtypeStruct((t, PEER_SEL), I32), jax.ShapeDtypeStruct((t, PEER_SEL), F32)],
        compiler_params=_params("arbitrary"),
        name="peer_topk",
    )(q, keys.astype(BF16))


PEER_TOK = 8
PEER_STEP_TOK = 16
PEER_SLABS = D_MODEL // LANES
_ERF_GELU_C = 0.7071067811865476


def _peer_apply_kernel(e_ref, h_ref, gw_ref, uv_ref, o_ref, buf_a, buf_b, sem):
    ns = PEER_SLABS
    nt = (((1,), (1,)), ((), ()))

    def copies(half, buf):
        def issue(n, carry):
            tok = n // PEER_SEL
            j = n % PEER_SEL
            idx = e_ref[half * PEER_TOK + tok, j]
            pltpu.make_async_copy(uv_ref.at[idx], buf.at[n], sem.at[half]).start()
            return carry
        lax.fori_loop(0, PEER_TOK * PEER_SEL, issue, 0)

    def wait_all(half, buf):
        pltpu.make_async_copy(uv_ref.at[pl.ds(0, PEER_TOK * PEER_SEL)], buf, sem.at[half]).wait()

    def compute(half, buf, act):
        r0 = half * PEER_TOK
        x = h_ref[...]
        rid = lax.broadcasted_iota(I32, (PEER_STEP_TOK, PEER_SEL), 0)

        def u_side(t, act):
            base = pl.multiple_of(t * PEER_SEL, PEER_SEL)
            a = jnp.zeros((PEER_STEP_TOK, PEER_SEL), F32)
            for s in range(ns):
                u_s = buf[pl.ds(base, PEER_SEL), s, :].astype(BF16)
                a = a + lax.dot_general(x[:, s * LANES:(s + 1) * LANES], u_s, nt, preferred_element_type=F32)
            return jnp.where(rid == r0 + t, a, act)

        act = lax.fori_loop(0, PEER_TOK, u_side, act)
        gelu = 0.5 * act * (1.0 + lax.erf(act * _ERF_GELU_C))
        w = (gw_ref[...] * gelu).astype(BF16)

        def v_side(t, carry):
            base = pl.multiple_of(t * PEER_SEL, PEER_SEL)
            for s in range(ns):
                v_s = buf[pl.ds(base, PEER_SEL), ns + s, :].astype(BF16)
                o = jnp.dot(w, v_s, preferred_element_type=F32)
                row = jnp.sum(jnp.where(rid == r0 + t, o, 0.0), axis=0, keepdims=True)
                o_ref[pl.ds(r0 + t, 1), s:s + 1, :] = row.reshape(1, 1, LANES)
            return carry

        lax.fori_loop(0, PEER_TOK, v_side, 0)
        return act

    copies(0, buf_a)
    copies(1, buf_b)
    wait_all(0, buf_a)
    act = compute(0, buf_a, jnp.zeros((PEER_STEP_TOK, PEER_SEL), F32))
    wait_all(1, buf_b)
    compute(1, buf_b, act)


def peer_apply(e, h, gw, uv):
    t, d = h.shape
    ns = PEER_SLABS
    tt = PEER_STEP_TOK
    out = pl.pallas_call(
        _peer_apply_kernel,
        grid=(t // tt,),
        in_specs=[
            pl.BlockSpec((tt, PEER_SEL), lambda i: (i, 0), memory_space=pltpu.SMEM),
            pl.BlockSpec((tt, d), lambda i: (i, 0)),
            pl.BlockSpec((tt, PEER_SEL), lambda i: (i, 0)),
            pl.BlockSpec(memory_space=pl.ANY),
        ],
        out_specs=pl.BlockSpec((tt, ns, LANES), lambda i: (i, 0, 0)),
        out_shape=jax.ShapeDtypeStruct((t, ns, LANES), F32),
        scratch_shapes=[
            pltpu.VMEM((PEER_TOK * PEER_SEL, 2 * ns, LANES), F32),
            pltpu.VMEM((PEER_TOK * PEER_SEL, 2 * ns, LANES), F32),
            pltpu.SemaphoreType.DMA((2,)),
        ],
        compiler_params=_params("arbitrary"),
        name="peer_apply",
    )(e, h, gw, uv)
    return out.reshape(t, d)


def peer_layer(h, w_q, keys, u_tab, v_tab):
    n_exp, d = u_tab.shape
    q = matmul(h, w_q.astype(BF16), tm=1024, tn=1024, out_dtype=F32)
    e, gw = peer_topk(q, keys)
    uv = jnp.concatenate([u_tab.reshape(n_exp, PEER_SLABS, LANES), v_tab.reshape(n_exp, PEER_SLABS, LANES)], axis=1)
    return peer_apply(e, h, gw, uv)


def kernel(x_prompt, x_sample, cache_ckv, cache_krope, state_C, state_n, state_m, c, c_ctx, mod_w, mod_b, norm_mix, norm_ffn, mla_w_in, mla_g_q, mla_g_kv, mla_w_uq, mla_g_qn, mla_w_uk, mla_w_uv, mla_g_kn, mla_w_o, ml_w_in, ml_b_gate, ml_g_h, ml_w_o, peer_w_q, peer_keys, peer_u, peer_v):
    d = D_MODEL
    x = jnp.concatenate([x_prompt.reshape(T_CTX, d), x_sample.reshape(T_LAT, d)], axis=0)
    cvec = jnp.concatenate([c_ctx.reshape(1, d), c, jnp.zeros((GROUP_PAD - N_GROUPS, d), c.dtype)], axis=0)
    mod = mod_vectors(cvec, mod_w, mod_b)
    modrows = mod.reshape(mod.shape[0] * GROUP_PAD * N_MOD, 1, d)

    (h,) = resid_modulate(x, modrows, norm_g=norm_mix[0], shift=(0, 0), scale=(0, 1))
    y, new_ckv, new_krope = mla_layer(h, cache_ckv[:, 0], cache_krope[:, 0], mla_w_in[0], mla_g_q[0], mla_g_kv[0],
                                      mla_w_uq[0], mla_g_qn[0], mla_w_uk[0], mla_w_uv[0], mla_g_kn[0], mla_w_o[0])
    x, h = resid_modulate(x, modrows, y=y, gate=(0, 2), norm_g=norm_ffn[0], shift=(0, 3), scale=(0, 4))
    y = peer_layer(h, peer_w_q[0], peer_keys[0], peer_u[0], peer_v[0])

    x, h = resid_modulate(x, modrows, y=y, gate=(0, 5), norm_g=norm_mix[1], shift=(1, 0), scale=(1, 1))
    y, new_c, new_n, new_m = mlstm_layer(h, state_C, state_n, state_m, ml_w_in[0], ml_b_gate[0], ml_g_h[0], ml_w_o[0])
    x, h = resid_modulate(x, modrows, y=y, gate=(1, 2), norm_g=norm_ffn[1], shift=(1, 3), scale=(1, 4))
    y = peer_layer(h, peer_w_q[1], peer_keys[1], peer_u[1], peer_v[1])
    (x,) = resid_modulate(x, modrows, y=y, gate=(1, 5))

    y_prompt = x[:T_CTX].reshape(BATCH, SEQ, d)
    y_sample = x[T_CTX:].reshape(DEC_BATCH, DEC_SEQ, d)
    return (y_prompt, y_sample, new_ckv, new_krope, new_c, new_n, new_m)
```

```python
import functools

import numpy as np
import jax
import jax.numpy as jnp
from jax import lax
from jax.experimental import pallas as pl
from jax.experimental.pallas import tpu as pltpu

F32 = jnp.float32
BF16 = jnp.bfloat16
I32 = jnp.int32

D_MODEL = 2048
BATCH, SEQ = 32, 256
DEC_BATCH, DEC_SEQ = 8, 1024
PAST_LEN = 512
GRID_W = 64
N_MOD = 6
NORM_EPS = 1e-6
MLA_HEADS = 16
MLA_Q_RANK = 512
MLA_KV_RANK = 512
MLA_NOPE = 128
MLA_ROPE = 64
MLA_QK_DIM = MLA_NOPE + MLA_ROPE
MLA_HEAD_PAD = 256
ROPE_BASE = 10000.0
ML_HEADS = 8
ML_DV = D_MODEL // ML_HEADS
ML_DK = ML_DV // 2
ML_CHUNK = 64
PEER_HEADS = 8
PEER_NKEYS = 128
PEER_QDIM = 128
PEER_TOPK = 16
PEER_SEL = PEER_HEADS * PEER_TOPK

T_CTX = BATCH * SEQ
T_LAT = DEC_BATCH * DEC_SEQ
T_ALL = T_CTX + T_LAT
N_GROUPS = 1 + DEC_BATCH
GROUP_PAD = 16
ROW_TILE = 256

VMEM_LIMIT_BYTES = 56 * 1024 * 1024
LANES = 128
SUBLANES = 8

NEG_INF = float("-inf")


def _params(*sem):
    return pltpu.CompilerParams(dimension_semantics=sem, vmem_limit_bytes=VMEM_LIMIT_BYTES)


def _mod_kernel(c_ref, w_ref, b_ref, o_ref):
    c = c_ref[...]
    a = (c * jax.nn.sigmoid(c)).astype(BF16)
    o_ref[0] = jnp.dot(a, w_ref[0].astype(BF16), preferred_element_type=F32) + b_ref[0]


def mod_vectors(cvec, mod_w, mod_b):
    depth, d, n = mod_w.shape
    tn = 1024
    return pl.pallas_call(
        _mod_kernel,
        grid=(depth, n // tn),
        in_specs=[
            pl.BlockSpec((GROUP_PAD, d), lambda l, j: (0, 0)),
            pl.BlockSpec((1, d, tn), lambda l, j: (l, 0, j)),
            pl.BlockSpec((1, 1, tn), lambda l, j: (l, 0, j)),
        ],
        out_specs=pl.BlockSpec((1, GROUP_PAD, tn), lambda l, j: (l, 0, j)),
        out_shape=jax.ShapeDtypeStruct((depth, GROUP_PAD, n), F32),
        compiler_params=_params("arbitrary", "arbitrary"),
        name="mod_vectors",
    )(cvec, mod_w, mod_b.reshape(depth, 1, n))


def _group_of_tile(i):
    ctx_tiles = T_CTX // ROW_TILE
    tiles_per_lat = DEC_SEQ // ROW_TILE
    return jnp.where(i < ctx_tiles, 0, 1 + (i - ctx_tiles) // tiles_per_lat)


def _resmod_kernel(*refs, has_res, has_mod):
    refs = list(refs)
    x_ref = refs.pop(0)
    x = x_ref[...]
    if has_res:
        y_ref = refs.pop(0)
        gate_ref = refs.pop(0)
        x = x + gate_ref[0] * y_ref[...]
    if has_mod:
        g_ref, sh_ref, sc_ref = refs.pop(0), refs.pop(0), refs.pop(0)
    if has_res:
        xo_ref = refs.pop(0)
        xo_ref[...] = x
    if has_mod:
        h_ref = refs.pop(0)
        ms = jnp.mean(x * x, axis=-1, keepdims=True)
        yn = x * lax.rsqrt(ms + NORM_EPS) * g_ref[...]
        h_ref[...] = (yn * (1.0 + sc_ref[0]) + sh_ref[0]).astype(h_ref.dtype)


def resid_modulate(x, modrows, *, y=None, gate=None, norm_g=None, shift=None, scale=None):
    t, d = x.shape
    has_res = y is not None
    has_mod = norm_g is not None
    row_spec = pl.BlockSpec((ROW_TILE, d), lambda i: (i, 0))

    def mod_spec(layer_k):
        layer, k = layer_k
        return pl.BlockSpec((1, 1, d), lambda i: ((layer * GROUP_PAD + _group_of_tile(i)) * N_MOD + k, 0, 0))

    args, in_specs = [x], [row_spec]
    if has_res:
        args += [y, modrows]
        in_specs += [row_spec, mod_spec(gate)]
    if has_mod:
        args += [norm_g.reshape(1, d), modrows, modrows]
        in_specs += [pl.BlockSpec((1, d), lambda i: (0, 0)), mod_spec(shift), mod_spec(scale)]
    out_shape, out_specs = [], []
    if has_res:
        out_shape.append(jax.ShapeDtypeStruct((t, d), F32))
        out_specs.append(row_spec)
    if has_mod:
        out_shape.append(jax.ShapeDtypeStruct((t, d), BF16))
        out_specs.append(row_spec)
    outs = pl.pallas_call(
        functools.partial(_resmod_kernel, has_res=has_res, has_mod=has_mod),
        grid=(t // ROW_TILE,),
        in_specs=in_specs,
        out_specs=out_specs,
        out_shape=out_shape,
        compiler_params=_params("arbitrary"),
        name="resid_modulate",
    )(*args)
    return outs


def _mm_kernel(x_ref, w_ref, o_ref):
    o_ref[...] = jnp.dot(x_ref[...].astype(BF16), w_ref[...], preferred_element_type=F32).astype(o_ref.dtype)


def matmul(x, w, *, tm, tn, out_dtype):
    m, k = x.shape
    n = w.shape[1]
    return pl.pallas_call(
        _mm_kernel,
        grid=(m // tm, n // tn),
        in_specs=[pl.BlockSpec((tm, k), lambda i, j: (i, 0)), pl.BlockSpec((k, tn), lambda i, j: (0, j))],
        out_specs=pl.BlockSpec((tm, tn), lambda i, j: (i, j)),
        out_shape=jax.ShapeDtypeStruct((m, n), out_dtype),
        compiler_params=_params("arbitrary", "arbitrary"),
        name="matmul",
    )(x, w)


def _rms(x, n):
    return lax.rsqrt(jnp.sum(x * x, axis=-1, keepdims=True) / n + NORM_EPS)


def _mla_in_kernel(h_ref, w_ref, gq_ref, gkv_ref, cq_ref, ckv_ref, kr_ref):
    a = jnp.dot(h_ref[...], w_ref[...], preferred_element_type=F32)
    cq = a[:, :MLA_Q_RANK]
    ckv = a[:, MLA_Q_RANK:MLA_Q_RANK + MLA_KV_RANK]
    cq_ref[...] = (cq * _rms(cq, MLA_Q_RANK) * gq_ref[...]).astype(cq_ref.dtype)
    ckv_ref[...] = ckv * _rms(ckv, MLA_KV_RANK) * gkv_ref[...]
    kr_ref[...] = a[:, MLA_Q_RANK + MLA_KV_RANK:]


def mla_in_proj(h, w_ext, g_q, g_kv):
    t, d = h.shape
    n = w_ext.shape[1]
    tm = 512
    return pl.pallas_call(
        _mla_in_kernel,
        grid=(t // tm,),
        in_specs=[
            pl.BlockSpec((tm, d), lambda i: (i, 0)),
            pl.BlockSpec((d, n), lambda i: (0, 0)),
            pl.BlockSpec((1, MLA_Q_RANK), lambda i: (0, 0)),
            pl.BlockSpec((1, MLA_KV_RANK), lambda i: (0, 0)),
        ],
        out_specs=[
            pl.BlockSpec((tm, MLA_Q_RANK), lambda i: (i, 0)),
            pl.BlockSpec((tm, MLA_KV_RANK), lambda i: (i, 0)),
            pl.BlockSpec((tm, LANES), lambda i: (i, 0)),
        ],
        out_shape=[
            jax.ShapeDtypeStruct((t, MLA_Q_RANK), BF16),
            jax.ShapeDtypeStruct((t, MLA_KV_RANK), F32),
            jax.ShapeDtypeStruct((t, LANES), F32),
        ],
        compiler_params=_params("arbitrary"),
        name="mla_in_proj",
    )(h, w_ext, g_q.reshape(1, -1), g_kv.reshape(1, -1))


def _rope_block(x, c, s1, s2):
    return x * c + pltpu.roll(x, 96, 1) * s1 + pltpu.roll(x, 32, 1) * s2


def _q_up_kernel(cq_ref, w_ref, g_ref, c_ref, s1_ref, s2_ref, q_ref, *, heads):
    a = jnp.dot(cq_ref[...], w_ref[...], preferred_element_type=F32)
    g = g_ref[...]
    for hh in range(heads):
        base = hh * MLA_HEAD_PAD
        nope = a[:, base:base + MLA_NOPE]
        rp = a[:, base + MLA_NOPE:base + MLA_HEAD_PAD]
        ss = jnp.sum(nope * nope, axis=-1, keepdims=True) + jnp.sum(rp * rp, axis=-1, keepdims=True)
        r = lax.rsqrt(ss / MLA_QK_DIM + NORM_EPS)
        xr = _rope_block(rp * r * g[:, MLA_NOPE:], c_ref[...], s1_ref[...], s2_ref[...])
        q_ref[:, base:base + MLA_NOPE] = (nope * r * g[:, :MLA_NOPE]).astype(q_ref.dtype)
        q_ref[:, base + MLA_NOPE:base + MLA_HEAD_PAD] = xr.astype(q_ref.dtype)


def mla_q_up(cq, w_uq_ext, g_qn_ext, rope_c, rope_s1, rope_s2):
    t, r = cq.shape
    n = w_uq_ext.shape[1]
    tm, heads = 512, 2
    tn = heads * MLA_HEAD_PAD
    tab = pl.BlockSpec((tm, LANES), lambda i, j: (i, 0))
    return pl.pallas_call(
        functools.partial(_q_up_kernel, heads=heads),
        grid=(t // tm, n // tn),
        in_specs=[
            pl.BlockSpec((tm, r), lambda i, j: (i, 0)),
            pl.BlockSpec((r, tn), lambda i, j: (0, j)),
            pl.BlockSpec((1, MLA_HEAD_PAD), lambda i, j: (0, 0)),
            tab, tab, tab,
        ],
        out_specs=pl.BlockSpec((tm, tn), lambda i, j: (i, j)),
        out_shape=jax.ShapeDtypeStruct((t, n), BF16),
        compiler_params=_params("arbitrary", "arbitrary"),
        name="mla_q_up",
    )(cq, w_uq_ext, g_qn_ext, rope_c, rope_s1, rope_s2)


def _kv_up_kernel(ckv_ref, kr_ref, wk_ref, wv_ref, g_ref, c_ref, s1_ref, s2_ref, k_ref, v_ref, *, heads):
    ckv = ckv_ref[...].astype(BF16)
    kn = jnp.dot(ckv, wk_ref[...], preferred_element_type=F32)
    v_ref[...] = jnp.dot(ckv, wv_ref[...], preferred_element_type=F32).astype(v_ref.dtype)
    g = g_ref[...]
    lane = lax.broadcasted_iota(I32, (1, LANES), 1)
    kr = jnp.where(lane < MLA_ROPE, kr_ref[...], 0.0)
    ss_r = jnp.sum(kr * kr, axis=-1, keepdims=True)
    krot = _rope_block(kr * g[:, MLA_NOPE:], c_ref[...], s1_ref[...], s2_ref[...])
    for hh in range(heads):
        nope = kn[:, hh * MLA_NOPE:(hh + 1) * MLA_NOPE]
        r = lax.rsqrt((jnp.sum(nope * nope, axis=-1, keepdims=True) + ss_r) / MLA_QK_DIM + NORM_EPS)
        base = hh * MLA_HEAD_PAD
        k_ref[:, base:base + MLA_NOPE] = (nope * r * g[:, :MLA_NOPE]).astype(k_ref.dtype)
        k_ref[:, base + MLA_NOPE:base + MLA_HEAD_PAD] = (krot * r).astype(k_ref.dtype)


def mla_kv_up(ckv, kr, w_uk, w_uv, g_kn_ext, rope_c, rope_s1, rope_s2):
    t, r = ckv.shape
    tm, heads = 512, 2
    tab = pl.BlockSpec((tm, LANES), lambda i, j: (i, 0))
    return pl.pallas_call(
        functools.partial(_kv_up_kernel, heads=heads),
        grid=(t // tm, MLA_HEADS // heads),
        in_specs=[
            pl.BlockSpec((tm, r), lambda i, j: (i, 0)),
            tab,
            pl.BlockSpec((r, heads * MLA_NOPE), lambda i, j: (0, j)),
            pl.BlockSpec((r, heads * MLA_NOPE), lambda i, j: (0, j)),
            pl.BlockSpec((1, MLA_HEAD_PAD), lambda i, j: (0, 0)),
            tab, tab, tab,
        ],
        out_specs=[
            pl.BlockSpec((tm, heads * MLA_HEAD_PAD), lambda i, j: (i, j)),
            pl.BlockSpec((tm, heads * MLA_NOPE), lambda i, j: (i, j)),
        ],
        out_shape=[
            jax.ShapeDtypeStruct((t, MLA_HEADS * MLA_HEAD_PAD), BF16),
            jax.ShapeDtypeStruct((t, MLA_HEADS * MLA_NOPE), BF16),
        ],
        compiler_params=_params("arbitrary", "arbitrary"),
        name="mla_kv_up",
    )(ckv, kr, w_uk, w_uv, g_kn_ext, rope_c, rope_s1, rope_s2)


def _attn_kernel(*refs, nseg):
    q_ref = refs[0]
    k_refs = refs[1:1 + nseg]
    v_refs = refs[1 + nseg:1 + 2 * nseg]
    o_ref = refs[1 + 2 * nseg]
    scale = MLA_QK_DIM ** -0.5
    q = q_ref[...]
    nt = (((1,), (1,)), ((), ()))
    s = [lax.dot_general(q, k[...], nt, preferred_element_type=F32) * scale for k in k_refs]
    m = functools.reduce(jnp.maximum, [jnp.max(x, axis=-1, keepdims=True) for x in s])
    e = [jnp.exp(x - m) for x in s]
    inv = 1.0 / functools.reduce(lambda a, b: a + b, [jnp.sum(x, axis=-1, keepdims=True) for x in e])
    o = functools.reduce(
        lambda a, b: a + b,
        [jnp.dot((x * inv).astype(BF16), v[...], preferred_element_type=F32) for x, v in zip(e, v_refs)])
    o_ref[...] = o.astype(o_ref.dtype)


def mla_attention(q, k, v, *, q_row0, n_batch, s_q, segs):
    tq = 256
    nq = s_q // tq
    nseg = len(segs)
    q_blk0 = q_row0 // tq
    in_specs = [pl.BlockSpec((tq, MLA_HEAD_PAD), lambda b, h, i: (q_blk0 + b * nq + i, h))]
    for row0, length in segs:
        in_specs.append(pl.BlockSpec((length, MLA_HEAD_PAD), lambda b, h, i, o=row0 // length: (o + b, h)))
    for row0, length in segs:
        in_specs.append(pl.BlockSpec((length, MLA_NOPE), lambda b, h, i, o=row0 // length: (o + b, h)))
    return pl.pallas_call(
        functools.partial(_attn_kernel, nseg=nseg),
        grid=(n_batch, MLA_HEADS, nq),
        in_specs=in_specs,
        out_specs=pl.BlockSpec((tq, MLA_NOPE), lambda b, h, i: (b * nq + i, h)),
        out_shape=jax.ShapeDtypeStruct((n_batch * s_q, MLA_HEADS * MLA_NOPE), BF16),
        compiler_params=_params("arbitrary", "arbitrary", "arbitrary"),
        name="mla_attention",
    )(q, *([k] * nseg), *([v] * nseg))


def _rope_tables():
    nf = MLA_ROPE // 4
    inv_freq = jnp.power(ROPE_BASE, -jnp.arange(nf, dtype=F32) / nf)
    tok = jnp.arange(DEC_SEQ)
    row = (tok // GRID_W).astype(F32)[:, None] * inv_freq[None, :]
    col = (tok % GRID_W).astype(F32)[:, None] * inv_freq[None, :]
    ang = jnp.concatenate([row, col], axis=-1)
    cos, sin = jnp.cos(ang), jnp.sin(ang)
    z32 = jnp.zeros_like(cos)
    z64 = jnp.zeros((DEC_SEQ, 64), F32)
    c = jnp.concatenate([cos, cos, z64], axis=-1)
    s1 = jnp.concatenate([-sin, z32, z64], axis=-1)
    s2 = jnp.concatenate([z32, sin, z64], axis=-1)
    ident_c = jnp.concatenate([jnp.ones((1, 64), F32), jnp.zeros((1, 64), F32)], axis=-1)

    def full(lat, ident):
        n_cache = DEC_BATCH * PAST_LEN
        return jnp.concatenate([
            jnp.broadcast_to(ident, (T_CTX, LANES)),
            jnp.tile(lat, (DEC_BATCH, 1)),
            jnp.broadcast_to(ident, (n_cache, LANES)),
        ], axis=0)

    zero = jnp.zeros((1, LANES), F32)
    return full(c, ident_c), full(s1, zero), full(s2, zero)


_ROPE_PERM = np.concatenate([np.arange(0, 16), np.arange(32, 48), np.arange(16, 32), np.arange(48, 64)])


def mla_layer(h, cache_ckv, cache_krope, w_in, g_q, g_kv, w_uq, g_qn, w_uk, w_uv, g_kn, w_o):
    d = h.shape[1]
    perm = _ROPE_PERM
    n_lat = MLA_Q_RANK + MLA_KV_RANK
    w_in_ext = jnp.concatenate([w_in[:, :n_lat], w_in[:, n_lat:][:, perm], w_in[:, n_lat:]], axis=1).astype(BF16)
    cq, ckv, kr = mla_in_proj(h, w_in_ext, g_q, g_kv)

    w3 = w_uq.reshape(MLA_Q_RANK, MLA_HEADS, MLA_QK_DIM)
    w_uq_ext = jnp.concatenate([
        w3[:, :, :MLA_NOPE], w3[:, :, MLA_NOPE:][:, :, perm],
        jnp.zeros((MLA_Q_RANK, MLA_HEADS, MLA_HEAD_PAD - MLA_QK_DIM), w_uq.dtype)], axis=-1)
    w_uq_ext = w_uq_ext.reshape(MLA_Q_RANK, MLA_HEADS * MLA_HEAD_PAD).astype(BF16)

    def gain_ext(g):
        return jnp.concatenate([g[:MLA_NOPE], g[MLA_NOPE:][perm],
                                jnp.zeros((MLA_HEAD_PAD - MLA_QK_DIM,), g.dtype)]).reshape(1, MLA_HEAD_PAD)

    rope_c, rope_s1, rope_s2 = _rope_tables()
    q = mla_q_up(cq, w_uq_ext, gain_ext(g_qn), rope_c[:T_ALL], rope_s1[:T_ALL], rope_s2[:T_ALL])

    n_cache = DEC_BATCH * PAST_LEN
    ckv_all = jnp.concatenate([ckv, cache_ckv.reshape(n_cache, MLA_KV_RANK)], axis=0)
    kr_cache = cache_krope.reshape(n_cache, MLA_ROPE)
    kr_cache = jnp.concatenate([kr_cache[:, perm], kr_cache], axis=1)
    kr_all = jnp.concatenate([kr, kr_cache], axis=0)
    k, v = mla_kv_up(ckv_all, kr_all, w_uk.astype(BF16), w_uv.astype(BF16), gain_ext(g_kn),
                     rope_c, rope_s1, rope_s2)

    o_ctx = mla_attention(q, k, v, q_row0=0, n_batch=BATCH, s_q=SEQ, segs=[(0, SEQ)])
    o_lat = mla_attention(q, k, v, q_row0=T_CTX, n_batch=DEC_BATCH, s_q=DEC_SEQ,
                          segs=[(T_ALL, PAST_LEN), (T_CTX, DEC_SEQ)])
    o = jnp.concatenate([o_ctx, o_lat], axis=0)
    y = matmul(o, w_o.astype(BF16), tm=1024, tn=1024, out_dtype=F32)
    new_ckv = ckv[:T_CTX].reshape(BATCH, 1, SEQ, MLA_KV_RANK)
    new_krope = kr[:T_CTX, MLA_ROPE:].reshape(BATCH, 1, SEQ, MLA_ROPE)
    return y, new_ckv, new_krope


def _log_sigmoid(x):
    return jnp.minimum(x, 0.0) - jnp.log(1.0 + jnp.exp(-jnp.abs(x)))


def _gates_kernel(h_ref, wg_ref, wgt_ref, b_ref, bt_ref, gc_ref, gr_ref):
    h = h_ref[...]
    gc = jnp.dot(h, wg_ref[...], preferred_element_type=F32) + b_ref[...]
    gr = lax.dot_general(wgt_ref[...], h, (((1,), (1,)), ((), ())), preferred_element_type=F32) + bt_ref[...]
    lane = lax.broadcasted_iota(I32, (1, LANES), 1)
    is_f = ((lane >= ML_HEADS) & (lane < 2 * ML_HEADS)) | ((lane >= 3 * ML_HEADS) & (lane < 4 * ML_HEADS))
    gc_ref[...] = jnp.where(is_f, _log_sigmoid(gc), gc)
    row = lax.broadcasted_iota(I32, (4 * ML_HEADS, 1), 0)
    is_fr = ((row >= ML_HEADS) & (row < 2 * ML_HEADS)) | ((row >= 3 * ML_HEADS) & (row < 4 * ML_HEADS))
    gr_ref[...] = jnp.where(is_fr, _log_sigmoid(gr), gr)


def mlstm_gates(h, w_g, b_gate):
    t, d = h.shape
    ng = 4 * ML_HEADS
    tm = 512
    wg = jnp.concatenate([w_g, jnp.zeros((d, LANES - ng), w_g.dtype)], axis=1).astype(BF16)
    wgt = w_g.T.astype(BF16)
    b = jnp.concatenate([b_gate, jnp.zeros((LANES - ng,), b_gate.dtype)]).reshape(1, LANES)
    bt = b_gate.reshape(ng, 1)
    return pl.pallas_call(
        _gates_kernel,
        grid=(t // tm,),
        in_specs=[
            pl.BlockSpec((tm, d), lambda i: (i, 0)),
            pl.BlockSpec((d, LANES), lambda i: (0, 0)),
            pl.BlockSpec((ng, d), lambda i: (0, 0)),
            pl.BlockSpec((1, LANES), lambda i: (0, 0)),
            pl.BlockSpec((ng, 1), lambda i: (0, 0)),
        ],
        out_specs=[pl.BlockSpec((tm, LANES), lambda i: (i, 0)), pl.BlockSpec((ng, tm), lambda i: (0, i))],
        out_shape=[jax.ShapeDtypeStruct((t, LANES), F32), jax.ShapeDtypeStruct((ng, t), F32)],
        compiler_params=_params("arbitrary"),
        name="mlstm_gates",
    )(h, wg, wgt, b, bt)


def _mlstm_kernel(*refs, nc, has_init, emit_state):
    refs = list(refs)
    q_ref, k_ref, v_ref, o_ref, gc_ref, gr_ref, gh_ref = [refs.pop(0) for _ in range(7)]
    if has_init:
        c0_ref, n0_ref, m0_ref = [refs.pop(0) for _ in range(3)]
    y_ref = refs.pop(0)
    if emit_state:
        cf_ref, nf_ref, mf_ref = [refs.pop(0) for _ in range(3)]
    mem_s, nrm_s, m_s, hs_s = refs

    L = ML_CHUNK
    tt = lax.broadcasted_iota(I32, (L, L), 0)
    ss = lax.broadcasted_iota(I32, (L, L), 1)
    q_scale = ML_DK ** -0.5
    nt = (((1,), (1,)), ((), ()))
    tn = (((0,), (0,)), ((), ()))

    for d in range(2):
        causal = (ss <= tt) if d == 0 else (ss >= tt)
        causal_t = (tt <= ss) if d == 0 else (tt >= ss)
        last = L - 1 if d == 0 else 0
        if has_init:
            mem_s[...] = c0_ref[0, d, 0]
            nrm_s[...] = n0_ref[0, d, 0]
            m_s[...] = m0_ref[0, d, 0]
        else:
            mem_s[...] = jnp.zeros_like(mem_s)
            nrm_s[...] = jnp.zeros_like(nrm_s)
            m_s[...] = jnp.zeros_like(m_s)

        def chunk(ci, carry, d=d, causal=causal, causal_t=causal_t, last=last):
            c = ci if d == 0 else nc - 1 - ci
            gcol = gc_ref[0, 0, c]
            grow = gr_ref[0, 0, c]
            i_col = gcol[:, 2 * d:2 * d + 1]
            f_col = gcol[:, 2 * d + 1:2 * d + 2]
            i_row = grow[2 * d:2 * d + 1, :]
            f_row = grow[2 * d + 1:2 * d + 2, :]
            cum_col = jnp.sum(jnp.where(causal, jnp.broadcast_to(f_row, (L, L)), 0.0), axis=1, keepdims=True)
            cum_row = jnp.sum(jnp.where(causal_t, jnp.broadcast_to(f_col, (L, L)), 0.0), axis=0, keepdims=True)
            total = cum_col[last:last + 1, :]
            m_prev = m_s[:, 0:1]
            dmat = jnp.where(causal, cum_col - cum_row + i_row, NEG_INF)
            inter = cum_col + m_prev
            m_t = jnp.maximum(inter, jnp.max(dmat, axis=1, keepdims=True))
            w_inter = jnp.exp(inter - m_t)
            rows = pl.ds(pl.multiple_of(c * L, L), L)
            qf = q_ref[rows, :] * q_scale
            kf = k_ref[rows, :]
            vb = v_ref[rows, :].astype(BF16)
            qb = qf.astype(BF16)
            qk = lax.dot_general(qb, kf.astype(BF16), nt, preferred_element_type=F32)
            a = jnp.exp(dmat - m_t) * qk
            mem = mem_s[...]
            nrm = nrm_s[...]
            num = (w_inter * jnp.dot(qb, mem.astype(BF16), preferred_element_type=F32)
                   + jnp.dot(a.astype(BF16), vb, preferred_element_type=F32))
            den = w_inter * jnp.sum(qf * nrm, axis=1, keepdims=True) + jnp.sum(a, axis=1, keepdims=True)
            hc = num / jnp.maximum(jnp.abs(den), jnp.exp(-m_t))
            m_new = m_t[last:last + 1, :]
            decay = jnp.exp(total + m_prev - m_new)
            w_s = jnp.exp(total - cum_col + i_col - m_new)
            wk = w_s * kf
            mem_s[...] = decay * mem + lax.dot_general(wk.astype(BF16), vb, tn, preferred_element_type=F32)
            nrm_s[...] = decay * nrm + jnp.sum(wk, axis=0, keepdims=True)
            m_s[...] = jnp.broadcast_to(m_new, m_s.shape)
            if d == 0:
                hs_s[rows, :] = hc
            else:
                hs_s[rows, :] = hs_s[rows, :] + hc
            return carry

        lax.fori_loop(0, nc, chunk, 0)
        if emit_state:
            cf_ref[0, d, 0] = mem_s[...]
            nf_ref[0, d, 0] = nrm_s[...]
            mf_ref[0, d, 0] = m_s[...]

    hs = hs_s[...]
    hn = hs * lax.rsqrt(jnp.mean(hs * hs, axis=-1, keepdims=True) + NORM_EPS) * gh_ref[...]
    y_ref[...] = (hn * jax.nn.sigmoid(o_ref[...])).astype(y_ref.dtype)


def mlstm_scan(p, gcol, grow, g_h, *, row0, n_batch, seq, state=None, emit_state=False):
    nc = seq // ML_CHUNK
    rb0 = row0 // seq
    hk = ML_HEADS * ML_DK
    has_init = state is not None
    in_specs = [
        pl.BlockSpec((seq, ML_DK), lambda b, h: (rb0 + b, h)),
        pl.BlockSpec((seq, ML_DK), lambda b, h: (rb0 + b, ML_HEADS + h)),
        pl.BlockSpec((seq, ML_DV), lambda b, h: (rb0 + b, 2 * hk // ML_DV + h)),
        pl.BlockSpec((seq, ML_DV), lambda b, h: (rb0 + b, 2 * hk // ML_DV + ML_HEADS + h)),
        pl.BlockSpec((1, 1, nc, ML_CHUNK, 4), lambda b, h: (b, h, 0, 0, 0)),
        pl.BlockSpec((1, 1, nc, 4, ML_CHUNK), lambda b, h: (b, h, 0, 0, 0)),
        pl.BlockSpec((1, ML_DV), lambda b, h: (0, h)),
    ]
    args = [p, p, p, p, gcol, grow, g_h.reshape(1, -1)]
    c_spec = pl.BlockSpec((1, 2, 1, ML_DK, ML_DV), lambda b, h: (b, 0, h, 0, 0))
    n_spec = pl.BlockSpec((1, 2, 1, 1, ML_DK), lambda b, h: (b, 0, h, 0, 0))
    m_spec = pl.BlockSpec((1, 2, 1, 1, LANES), lambda b, h: (b, 0, h, 0, 0))
    if has_init:
        in_specs += [c_spec, n_spec, m_spec]
        args += list(state)
    out_specs = [pl.BlockSpec((seq, ML_DV), lambda b, h: (b, h))]
    out_shape = [jax.ShapeDtypeStruct((n_batch * seq, ML_HEADS * ML_DV), BF16)]
    if emit_state:
        out_specs += [c_spec, n_spec, m_spec]
        out_shape += [
            jax.ShapeDtypeStruct((n_batch, 2, ML_HEADS, ML_DK, ML_DV), F32),
            jax.ShapeDtypeStruct((n_batch, 2, ML_HEADS, 1, ML_DK), F32),
            jax.ShapeDtypeStruct((n_batch, 2, ML_HEADS, 1, LANES), F32),
        ]
    return pl.pallas_call(
        functools.partial(_mlstm_kernel, nc=nc, has_init=has_init, emit_state=emit_state),
        grid=(n_batch, ML_HEADS),
        in_specs=in_specs,
        out_specs=out_specs,
        out_shape=out_shape,
        scratch_shapes=[
            pltpu.VMEM((ML_DK, ML_DV), F32),
            pltpu.VMEM((1, ML_DK), F32),
            pltpu.VMEM((1, LANES), F32),
            pltpu.VMEM((seq, ML_DV), F32),
        ],
        compiler_params=_params("arbitrary", "arbitrary"),
        name="mlstm_scan",
    )(*args)


def _gate_layouts(gc, gr, row0, n_batch, seq):
    nc = seq // ML_CHUNK
    n = n_batch * seq
    gcol = gc[row0:row0 + n, :4 * ML_HEADS].reshape(n_batch, nc, ML_CHUNK, 4, ML_HEADS).transpose(0, 4, 1, 2, 3)
    grow = gr[:, row0:row0 + n].reshape(4, ML_HEADS, n_batch, nc, ML_CHUNK).transpose(2, 1, 3, 0, 4)
    return gcol, grow


def mlstm_layer(h, state_c, state_n, state_m, w_in, b_gate, g_h, w_o):
    hk = ML_HEADS * ML_DK
    hv = ML_HEADS * ML_DV
    n_main = 2 * hk + 2 * hv
    p = matmul(h, w_in[:, :n_main].astype(BF16), tm=1024, tn=1024 + 512, out_dtype=F32)
    gc, gr = mlstm_gates(h, w_in[:, n_main:], b_gate)

    gcol, grow = _gate_layouts(gc, gr, 0, BATCH, SEQ)
    y_ctx, cf, nf, mf = mlstm_scan(p, gcol, grow, g_h, row0=0, n_batch=BATCH, seq=SEQ, emit_state=True)

    gcol, grow = _gate_layouts(gc, gr, T_CTX, DEC_BATCH, DEC_SEQ)
    c0 = state_c[:, 0]
    n0 = state_n[:, 0].reshape(DEC_BATCH, 2, ML_HEADS, 1, ML_DK)
    m0 = jnp.broadcast_to(state_m[:, 0].reshape(DEC_BATCH, 2, ML_HEADS, 1, 1), (DEC_BATCH, 2, ML_HEADS, 1, LANES))
    (y_lat,) = mlstm_scan(p, gcol, grow, g_h, row0=T_CTX, n_batch=DEC_BATCH, seq=DEC_SEQ, state=(c0, n0, m0))

    y = matmul(jnp.concatenate([y_ctx, y_lat], axis=0), w_o.astype(BF16), tm=1024, tn=1024, out_dtype=F32)
    new_c = cf.reshape(BATCH, 1, 2, ML_HEADS, ML_DK, ML_DV)
    new_n = nf.reshape(BATCH, 1, 2, ML_HEADS, ML_DK)
    new_m = mf[..., 0, 0].reshape(BATCH, 1, 2, ML_HEADS)
    return y, new_c, new_n, new_m


def _topk_rows(x, payload, n_out):
    rows = x.shape[0]
    iota = lax.broadcasted_iota(I32, x.shape, 0)
    vals, outs = [], []
    for _ in range(n_out):
        m = jnp.max(x, axis=0, keepdims=True)
        pos = jnp.min(jnp.where(x == m, iota, rows), axis=0, keepdims=True)
        sel = iota == pos
        vals.append(m)
        outs.append(pos if payload is None else jnp.max(jnp.where(sel, payload, -1), axis=0, keepdims=True))
        x = jnp.where(sel, NEG_INF, x)
    return jnp.concatenate(vals, axis=0), jnp.concatenate(outs, axis=0)


def _pair_candidates(a, b, combine, fill):
    k = PEER_TOPK
    row = lax.broadcasted_iota(I32, (SUBLANES, a.shape[1]), 0)
    blocks = [combine(a[0:1, :], b), combine(a[1:2, :], b[0:SUBLANES, :])]
    for i in range(2, SUBLANES):
        blocks.append(jnp.where(row < k // (i + 1), combine(a[i:i + 1, :], b[0:SUBLANES, :]), fill))
    blocks.append(combine(a[SUBLANES:k, :], b[0:1, :]))
    return jnp.concatenate(blocks, axis=0)


def _peer_topk_kernel(q_ref, keys_ref, e_ref, gw_ref):
    tt = q_ref.shape[0]
    half = PEER_QDIM // 2
    nt = (((1,), (1,)), ((), ()))
    e_parts, g_parts = [], []
    for hd in range(PEER_HEADS):
        sv, si = [], []
        for p in range(2):
            lo = (hd * 2 + p) * half
            qs = q_ref[:, lo:lo + half].astype(BF16)
            sc = lax.dot_general(keys_ref[p], qs, nt, preferred_element_type=F32)
            v, i = _topk_rows(sc, None, PEER_TOPK)
            sv.append(v)
            si.append(i)
        cand = _pair_candidates(sv[0], sv[1], lambda x, y: x + y, NEG_INF)
        cidx = _pair_candidates(si[0], si[1], lambda x, y: x * PEER_NKEYS + y, -1)
        best, eidx = _topk_rows(cand, cidx, PEER_TOPK)
        ex = jnp.exp(best - best[0:1, :])
        g_parts.append(ex / jnp.sum(ex, axis=0, keepdims=True))
        e_parts.append(eidx)
    e_ref[...] = jnp.concatenate(e_parts, axis=0).T
    gw_ref[...] = jnp.concatenate(g_parts, axis=0).T


def peer_topk(q, keys):
    t, n = q.shape
    tt = 128
    return pl.pallas_call(
        _peer_topk_kernel,
        grid=(t // tt,),
        in_specs=[pl.BlockSpec((tt, n), lambda i: (i, 0)),
                  pl.BlockSpec(keys.shape, lambda i: (0, 0, 0))],
        out_specs=[pl.BlockSpec((tt, PEER_SEL), lambda i: (i, 0)), pl.BlockSpec((tt, PEER_SEL), lambda i: (i, 0))],
        out_shape=[jax.ShapeDtypeStruct((t, PEER_SEL), I32), jax.ShapeDtypeStruct((t, PEER_SEL), F32)],
        compiler_params=_params("arbitrary"),
        name="peer_topk",
    )(q, keys.astype(BF16))


PEER_STEP_TOK = 16
PEER_SLABS = D_MODEL // LANES
PEER_ROWS = PEER_SEL * PEER_SLABS
_ERF_GELU_C = 0.7071067811865476


def _peer_apply_kernel(e_ref, en_ref, h_ref, gw_ref, expand_ref, expand_t_ref, uv_ref, o_ref,
                       buf_a, buf_b, zs, wexp, sem):
    i = pl.program_id(0)
    n_steps = pl.num_programs(0)
    ns = PEER_SLABS
    tt = PEER_STEP_TOK
    nt = (((1,), (1,)), ((), ()))

    def gather_copy(idx, buf, n, s):
        return pltpu.make_async_copy(uv_ref.at[idx], buf.at[:, pl.ds(n * ns, ns), :], sem.at[s])

    def issue(idx_ref, buf, s):
        for tok in range(tt):
            for k in range(PEER_SEL):
                gather_copy(idx_ref[tok, k], buf, tok * PEER_SEL + k, s).start()

    def wait(buf, s):
        pltpu.make_async_copy(buf, buf, sem.at[s]).wait()

    def compute(buf):
        sub = lax.broadcasted_iota(I32, (ns, PEER_ROWS), 0)
        col = lax.broadcasted_iota(I32, (ns, PEER_ROWS), 1)
        diag = (col % ns) == sub
        for t in range(tt):
            u_t = buf[0, pl.ds(t * PEER_ROWS, PEER_ROWS), :]
            y = lax.dot_general(h_ref[t], u_t, nt, preferred_element_type=F32)
            zs[pl.ds(t, 1), :] = jnp.sum(jnp.where(diag, y, 0.0), axis=0, keepdims=True)
        z = zs[...]
        z_hi = z.astype(BF16)
        z_lo = (z - z_hi.astype(F32)).astype(BF16)
        act = (jnp.dot(z_hi, expand_t_ref[...], preferred_element_type=F32)
               + jnp.dot(z_lo, expand_t_ref[...], preferred_element_type=F32))
        gelu = 0.5 * act * (1.0 + lax.erf(act * _ERF_GELU_C))
        w = (gw_ref[...] * gelu).astype(BF16)
        wexp[...] = jnp.dot(w, expand_ref[...], preferred_element_type=F32)
        for t in range(tt):
            wrow = wexp[pl.ds(t, 1), :]
            wbig = jnp.where(diag, jnp.broadcast_to(wrow, (ns, PEER_ROWS)), 0.0).astype(BF16)
            v_t = buf[1, pl.ds(t * PEER_ROWS, PEER_ROWS), :]
            o_ref[t] = jnp.dot(wbig, v_t, preferred_element_type=F32)

    def step(cur, s_cur, nxt, s_nxt):
        wait(cur, s_cur)
        issue(en_ref, nxt, s_nxt)
        compute(cur)

        @pl.when(i == n_steps - 1)
        def _():
            wait(nxt, s_nxt)

    @pl.when(i == 0)
    def _():
        issue(e_ref, buf_a, 0)

    @pl.when(i % 2 == 0)
    def _():
        step(buf_a, 0, buf_b, 1)

    @pl.when(i % 2 == 1)
    def _():
        step(buf_b, 1, buf_a, 0)


def peer_apply(e, h, gw, uv):
    t, d = h.shape
    ns = PEER_SLABS
    tt = PEER_STEP_TOK
    n_steps = t // tt
    h3 = h.reshape(t, ns, LANES)
    group = np.arange(PEER_ROWS) // ns
    expand = jnp.asarray(group[None, :] == np.arange(PEER_SEL)[:, None], BF16)
    out = pl.pallas_call(
        _peer_apply_kernel,
        grid=(n_steps,),
        in_specs=[
            pl.BlockSpec((tt, PEER_SEL), lambda i: (i, 0), memory_space=pltpu.SMEM),
            pl.BlockSpec((tt, PEER_SEL), lambda i: (jnp.minimum(i + 1, n_steps - 1), 0), memory_space=pltpu.SMEM),
            pl.BlockSpec((tt, ns, LANES), lambda i: (i, 0, 0)),
            pl.BlockSpec((tt, PEER_SEL), lambda i: (i, 0)),
            pl.BlockSpec((PEER_SEL, PEER_ROWS), lambda i: (0, 0)),
            pl.BlockSpec((PEER_ROWS, PEER_SEL), lambda i: (0, 0)),
            pl.BlockSpec(memory_space=pl.ANY),
        ],
        out_specs=pl.BlockSpec((tt, ns, LANES), lambda i: (i, 0, 0)),
        out_shape=jax.ShapeDtypeStruct((t, ns, LANES), F32),
        scratch_shapes=[
            pltpu.VMEM((2, tt * PEER_ROWS, LANES), BF16),
            pltpu.VMEM((2, tt * PEER_ROWS, LANES), BF16),
            pltpu.VMEM((tt, PEER_ROWS), F32),
            pltpu.VMEM((tt, PEER_ROWS), F32),
            pltpu.SemaphoreType.DMA((2,)),
        ],
        compiler_params=pltpu.CompilerParams(dimension_semantics=("arbitrary",), vmem_limit_bytes=VMEM_LIMIT_BYTES,
                                             disable_bounds_checks=True),
        name="peer_apply",
    )(e, e, h3, gw, expand, expand.T, uv)
    return out.reshape(t, d)


def peer_layer(h, w_q, keys, u_tab, v_tab):
    n_exp, d = u_tab.shape
    q = matmul(h, w_q.astype(BF16), tm=1024, tn=1024, out_dtype=F32)
    e, gw = peer_topk(q, keys)
    uv = jnp.stack([u_tab.astype(BF16).reshape(n_exp, PEER_SLABS, LANES),
                    v_tab.astype(BF16).reshape(n_exp, PEER_SLABS, LANES)], axis=1)
    return peer_apply(e, h, gw, uv)


def kernel(x_prompt, x_sample, cache_ckv, cache_krope, state_C, state_n, state_m, c, c_ctx, mod_w, mod_b, norm_mix, norm_ffn, mla_w_in, mla_g_q, mla_g_kv, mla_w_uq, mla_g_qn, mla_w_uk, mla_w_uv, mla_g_kn, mla_w_o, ml_w_in, ml_b_gate, ml_g_h, ml_w_o, peer_w_q, peer_keys, peer_u, peer_v):
    d = D_MODEL
    x = jnp.concatenate([x_prompt.reshape(T_CTX, d), x_sample.reshape(T_LAT, d)], axis=0)
    cvec = jnp.concatenate([c_ctx.reshape(1, d), c, jnp.zeros((GROUP_PAD - N_GROUPS, d), c.dtype)], axis=0)
    mod = mod_vectors(cvec, mod_w, mod_b)
    modrows = mod.reshape(mod.shape[0] * GROUP_PAD * N_MOD, 1, d)

    (h,) = resid_modulate(x, modrows, norm_g=norm_mix[0], shift=(0, 0), scale=(0, 1))
    y, new_ckv, new_krope = mla_layer(h, cache_ckv[:, 0], cache_krope[:, 0], mla_w_in[0], mla_g_q[0], mla_g_kv[0],
                                      mla_w_uq[0], mla_g_qn[0], mla_w_uk[0], mla_w_uv[0], mla_g_kn[0], mla_w_o[0])
    x, h = resid_modulate(x, modrows, y=y, gate=(0, 2), norm_g=norm_ffn[0], shift=(0, 3), scale=(0, 4))
    y = peer_layer(h, peer_w_q[0], peer_keys[0], peer_u[0], peer_v[0])

    x, h = resid_modulate(x, modrows, y=y, gate=(0, 5), norm_g=norm_mix[1], shift=(1, 0), scale=(1, 1))
    y, new_c, new_n, new_m = mlstm_layer(h, state_C, state_n, state_m, ml_w_in[0], ml_b_gate[0], ml_g_h[0], ml_w_o[0])
    x, h = resid_modulate(x, modrows, y=y, gate=(1, 2), norm_g=norm_ffn[1], shift=(1, 3), scale=(1, 4))
    y = peer_layer(h, peer_w_q[1], peer_keys[1], peer_u[1], peer_v[1])
    (x,) = resid_modulate(x, modrows, y=y, gate=(1, 5))

    y_prompt = x[:T_CTX].reshape(BATCH, SEQ, d)
    y_sample = x[T_CTX:].reshape(DEC_BATCH, DEC_SEQ, d)
    return (y_prompt, y_sample, new_ckv, new_krope, new_c, new_n, new_m)
```

```python
import functools

import numpy as np
import jax
import jax.numpy as jnp
from jax import lax
from jax.experimental import pallas as pl
from jax.experimental.pallas import tpu as pltpu

F32 = jnp.float32
BF16 = jnp.bfloat16
I32 = jnp.int32

D_MODEL = 2048
BATCH, SEQ = 32, 256
DEC_BATCH, DEC_SEQ = 8, 1024
PAST_LEN = 512
GRID_W = 64
N_MOD = 6
NORM_EPS = 1e-6
MLA_HEADS = 16
MLA_Q_RANK = 512
MLA_KV_RANK = 512
MLA_NOPE = 128
MLA_ROPE = 64
MLA_QK_DIM = MLA_NOPE + MLA_ROPE
MLA_HEAD_PAD = 256
ROPE_BASE = 10000.0
ML_HEADS = 8
ML_DV = D_MODEL // ML_HEADS
ML_DK = ML_DV // 2
ML_CHUNK = 64
PEER_HEADS = 8
PEER_NKEYS = 128
PEER_QDIM = 128
PEER_TOPK = 16
PEER_SEL = PEER_HEADS * PEER_TOPK

T_CTX = BATCH * SEQ
T_LAT = DEC_BATCH * DEC_SEQ
T_ALL = T_CTX + T_LAT
N_GROUPS = 1 + DEC_BATCH
GROUP_PAD = 16
ROW_TILE = 256

VMEM_LIMIT_BYTES = 56 * 1024 * 1024
LANES = 128
SUBLANES = 8

NEG_INF = float("-inf")


def _params(*sem):
    return pltpu.CompilerParams(dimension_semantics=sem, vmem_limit_bytes=VMEM_LIMIT_BYTES)


def _mod_kernel(c_ref, w_ref, b_ref, o_ref):
    c = c_ref[...]
    a = (c * jax.nn.sigmoid(c)).astype(BF16)
    o_ref[0] = jnp.dot(a, w_ref[0].astype(BF16), preferred_element_type=F32) + b_ref[0]


def mod_vectors(cvec, mod_w, mod_b):
    depth, d, n = mod_w.shape
    tn = 1024
    return pl.pallas_call(
        _mod_kernel,
        grid=(depth, n // tn),
        in_specs=[
            pl.BlockSpec((GROUP_PAD, d), lambda l, j: (0, 0)),
            pl.BlockSpec((1, d, tn), lambda l, j: (l, 0, j)),
            pl.BlockSpec((1, 1, tn), lambda l, j: (l, 0, j)),
        ],
        out_specs=pl.BlockSpec((1, GROUP_PAD, tn), lambda l, j: (l, 0, j)),
        out_shape=jax.ShapeDtypeStruct((depth, GROUP_PAD, n), F32),
        compiler_params=_params("arbitrary", "arbitrary"),
        name="mod_vectors",
    )(cvec, mod_w, mod_b.reshape(depth, 1, n))


def _group_of_tile(i):
    ctx_tiles = T_CTX // ROW_TILE
    tiles_per_lat = DEC_SEQ // ROW_TILE
    return jnp.where(i < ctx_tiles, 0, 1 + (i - ctx_tiles) // tiles_per_lat)


def _resmod_kernel(*refs, has_res, has_mod):
    refs = list(refs)
    x_ref = refs.pop(0)
    x = x_ref[...]
    if has_res:
        y_ref = refs.pop(0)
        gate_ref = refs.pop(0)
        x = x + gate_ref[0] * y_ref[...]
    if has_mod:
        g_ref, sh_ref, sc_ref = refs.pop(0), refs.pop(0), refs.pop(0)
    if has_res:
        xo_ref = refs.pop(0)
        xo_ref[...] = x
    if has_mod:
        h_ref = refs.pop(0)
        ms = jnp.mean(x * x, axis=-1, keepdims=True)
        yn = x * lax.rsqrt(ms + NORM_EPS) * g_ref[...]
        h_ref[...] = (yn * (1.0 + sc_ref[0]) + sh_ref[0]).astype(h_ref.dtype)


def resid_modulate(x, modrows, *, y=None, gate=None, norm_g=None, shift=None, scale=None):
    t, d = x.shape
    has_res = y is not None
    has_mod = norm_g is not None
    row_spec = pl.BlockSpec((ROW_TILE, d), lambda i: (i, 0))

    def mod_spec(layer_k):
        layer, k = layer_k
        return pl.BlockSpec((1, 1, d), lambda i: ((layer * GROUP_PAD + _group_of_tile(i)) * N_MOD + k, 0, 0))

    args, in_specs = [x], [row_spec]
    if has_res:
        args += [y, modrows]
        in_specs += [row_spec, mod_spec(gate)]
    if has_mod:
        args += [norm_g.reshape(1, d), modrows, modrows]
        in_specs += [pl.BlockSpec((1, d), lambda i: (0, 0)), mod_spec(shift), mod_spec(scale)]
    out_shape, out_specs = [], []
    if has_res:
        out_shape.append(jax.ShapeDtypeStruct((t, d), F32))
        out_specs.append(row_spec)
    if has_mod:
        out_shape.append(jax.ShapeDtypeStruct((t, d), BF16))
        out_specs.append(row_spec)
    outs = pl.pallas_call(
        functools.partial(_resmod_kernel, has_res=has_res, has_mod=has_mod),
        grid=(t // ROW_TILE,),
        in_specs=in_specs,
        out_specs=out_specs,
        out_shape=out_shape,
        compiler_params=_params("arbitrary"),
        name="resid_modulate",
    )(*args)
    return outs


def _mm_kernel(x_ref, w_ref, o_ref):
    o_ref[...] = jnp.dot(x_ref[...].astype(BF16), w_ref[...], preferred_element_type=F32).astype(o_ref.dtype)


def matmul(x, w, *, tm, tn, out_dtype):
    m, k = x.shape
    n = w.shape[1]
    return pl.pallas_call(
        _mm_kernel,
        grid=(m // tm, n // tn),
        in_specs=[pl.BlockSpec((tm, k), lambda i, j: (i, 0)), pl.BlockSpec((k, tn), lambda i, j: (0, j))],
        out_specs=pl.BlockSpec((tm, tn), lambda i, j: (i, j)),
        out_shape=jax.ShapeDtypeStruct((m, n), out_dtype),
        compiler_params=_params("arbitrary", "arbitrary"),
        name="matmul",
    )(x, w)


def _rms(x, n):
    return lax.rsqrt(jnp.sum(x * x, axis=-1, keepdims=True) / n + NORM_EPS)


def _mla_in_kernel(h_ref, w_ref, gq_ref, gkv_ref, cq_ref, ckv_ref, kr_ref):
    a = jnp.dot(h_ref[...], w_ref[...], preferred_element_type=F32)
    cq = a[:, :MLA_Q_RANK]
    ckv = a[:, MLA_Q_RANK:MLA_Q_RANK + MLA_KV_RANK]
    cq_ref[...] = (cq * _rms(cq, MLA_Q_RANK) * gq_ref[...]).astype(cq_ref.dtype)
    ckv_ref[...] = ckv * _rms(ckv, MLA_KV_RANK) * gkv_ref[...]
    kr_ref[...] = a[:, MLA_Q_RANK + MLA_KV_RANK:]


def mla_in_proj(h, w_ext, g_q, g_kv):
    t, d = h.shape
    n = w_ext.shape[1]
    tm = 512
    return pl.pallas_call(
        _mla_in_kernel,
        grid=(t // tm,),
        in_specs=[
            pl.BlockSpec((tm, d), lambda i: (i, 0)),
            pl.BlockSpec((d, n), lambda i: (0, 0)),
            pl.BlockSpec((1, MLA_Q_RANK), lambda i: (0, 0)),
            pl.BlockSpec((1, MLA_KV_RANK), lambda i: (0, 0)),
        ],
        out_specs=[
            pl.BlockSpec((tm, MLA_Q_RANK), lambda i: (i, 0)),
            pl.BlockSpec((tm, MLA_KV_RANK), lambda i: (i, 0)),
            pl.BlockSpec((tm, LANES), lambda i: (i, 0)),
        ],
        out_shape=[
            jax.ShapeDtypeStruct((t, MLA_Q_RANK), BF16),
            jax.ShapeDtypeStruct((t, MLA_KV_RANK), F32),
            jax.ShapeDtypeStruct((t, LANES), F32),
        ],
        compiler_params=_params("arbitrary"),
        name="mla_in_proj",
    )(h, w_ext, g_q.reshape(1, -1), g_kv.reshape(1, -1))


def _rope_block(x, c, s1, s2):
    return x * c + pltpu.roll(x, 96, 1) * s1 + pltpu.roll(x, 32, 1) * s2


def _q_up_kernel(cq_ref, w_ref, g_ref, c_ref, s1_ref, s2_ref, q_ref, *, heads):
    a = jnp.dot(cq_ref[...], w_ref[...], preferred_element_type=F32)
    g = g_ref[...]
    for hh in range(heads):
        base = hh * MLA_HEAD_PAD
        nope = a[:, base:base + MLA_NOPE]
        rp = a[:, base + MLA_NOPE:base + MLA_HEAD_PAD]
        ss = jnp.sum(nope * nope, axis=-1, keepdims=True) + jnp.sum(rp * rp, axis=-1, keepdims=True)
        r = lax.rsqrt(ss / MLA_QK_DIM + NORM_EPS)
        xr = _rope_block(rp * r * g[:, MLA_NOPE:], c_ref[...], s1_ref[...], s2_ref[...])
        q_ref[:, base:base + MLA_NOPE] = (nope * r * g[:, :MLA_NOPE]).astype(q_ref.dtype)
        q_ref[:, base + MLA_NOPE:base + MLA_HEAD_PAD] = xr.astype(q_ref.dtype)


def mla_q_up(cq, w_uq_ext, g_qn_ext, rope_c, rope_s1, rope_s2):
    t, r = cq.shape
    n = w_uq_ext.shape[1]
    tm, heads = 512, 2
    tn = heads * MLA_HEAD_PAD
    tab = pl.BlockSpec((tm, LANES), lambda i, j: (i, 0))
    return pl.pallas_call(
        functools.partial(_q_up_kernel, heads=heads),
        grid=(t // tm, n // tn),
        in_specs=[
            pl.BlockSpec((tm, r), lambda i, j: (i, 0)),
            pl.BlockSpec((r, tn), lambda i, j: (0, j)),
            pl.BlockSpec((1, MLA_HEAD_PAD), lambda i, j: (0, 0)),
            tab, tab, tab,
        ],
        out_specs=pl.BlockSpec((tm, tn), lambda i, j: (i, j)),
        out_shape=jax.ShapeDtypeStruct((t, n), BF16),
        compiler_params=_params("arbitrary", "arbitrary"),
        name="mla_q_up",
    )(cq, w_uq_ext, g_qn_ext, rope_c, rope_s1, rope_s2)


def _kv_up_kernel(ckv_ref, kr_ref, wk_ref, wv_ref, g_ref, c_ref, s1_ref, s2_ref, k_ref, v_ref, *, heads):
    ckv = ckv_ref[...].astype(BF16)
    kn = jnp.dot(ckv, wk_ref[...], preferred_element_type=F32)
    v_ref[...] = jnp.dot(ckv, wv_ref[...], preferred_element_type=F32).astype(v_ref.dtype)
    g = g_ref[...]
    lane = lax.broadcasted_iota(I32, (1, LANES), 1)
    kr = jnp.where(lane < MLA_ROPE, kr_ref[...], 0.0)
    ss_r = jnp.sum(kr * kr, axis=-1, keepdims=True)
    krot = _rope_block(kr * g[:, MLA_NOPE:], c_ref[...], s1_ref[...], s2_ref[...])
    for hh in range(heads):
        nope = kn[:, hh * MLA_NOPE:(hh + 1) * MLA_NOPE]
        r = lax.rsqrt((jnp.sum(nope * nope, axis=-1, keepdims=True) + ss_r) / MLA_QK_DIM + NORM_EPS)
        base = hh * MLA_HEAD_PAD
        k_ref[:, base:base + MLA_NOPE] = (nope * r * g[:, :MLA_NOPE]).astype(k_ref.dtype)
        k_ref[:, base + MLA_NOPE:base + MLA_HEAD_PAD] = (krot * r).astype(k_ref.dtype)


def mla_kv_up(ckv, kr, w_uk, w_uv, g_kn_ext, rope_c, rope_s1, rope_s2):
    t, r = ckv.shape
    tm, heads = 512, 2
    tab = pl.BlockSpec((tm, LANES), lambda i, j: (i, 0))
    return pl.pallas_call(
        functools.partial(_kv_up_kernel, heads=heads),
        grid=(t // tm, MLA_HEADS // heads),
        in_specs=[
            pl.BlockSpec((tm, r), lambda i, j: (i, 0)),
            tab,
            pl.BlockSpec((r, heads * MLA_NOPE), lambda i, j: (0, j)),
            pl.BlockSpec((r, heads * MLA_NOPE), lambda i, j: (0, j)),
            pl.BlockSpec((1, MLA_HEAD_PAD), lambda i, j: (0, 0)),
            tab, tab, tab,
        ],
        out_specs=[
            pl.BlockSpec((tm, heads * MLA_HEAD_PAD), lambda i, j: (i, j)),
            pl.BlockSpec((tm, heads * MLA_NOPE), lambda i, j: (i, j)),
        ],
        out_shape=[
            jax.ShapeDtypeStruct((t, MLA_HEADS * MLA_HEAD_PAD), BF16),
            jax.ShapeDtypeStruct((t, MLA_HEADS * MLA_NOPE), BF16),
        ],
        compiler_params=_params("arbitrary", "arbitrary"),
        name="mla_kv_up",
    )(ckv, kr, w_uk, w_uv, g_kn_ext, rope_c, rope_s1, rope_s2)


ATTN_HEADS_PER_STEP = 4


def _attn_kernel(*refs, nseg):
    q_ref = refs[0]
    k_refs = refs[1:1 + nseg]
    v_refs = refs[1 + nseg:1 + 2 * nseg]
    o_ref = refs[1 + 2 * nseg]
    scale = MLA_QK_DIM ** -0.5
    nt = (((1,), (1,)), ((), ()))
    for hh in range(ATTN_HEADS_PER_STEP):
        qk_cols = slice(hh * MLA_HEAD_PAD, (hh + 1) * MLA_HEAD_PAD)
        v_cols = slice(hh * MLA_NOPE, (hh + 1) * MLA_NOPE)
        q = q_ref[:, qk_cols]
        s = [lax.dot_general(q, k[:, qk_cols], nt, preferred_element_type=F32) * scale for k in k_refs]
        m = functools.reduce(jnp.maximum, [jnp.max(x, axis=-1, keepdims=True) for x in s])
        e = [jnp.exp(x - m) for x in s]
        inv = 1.0 / functools.reduce(lambda a, b: a + b, [jnp.sum(x, axis=-1, keepdims=True) for x in e])
        o = functools.reduce(
            lambda a, b: a + b,
            [jnp.dot((x * inv).astype(BF16), v[:, v_cols], preferred_element_type=F32) for x, v in zip(e, v_refs)])
        o_ref[:, v_cols] = o.astype(o_ref.dtype)


def mla_attention(q, k, v, *, q_row0, n_batch, s_q, segs):
    tq = 256
    nq = s_q // tq
    nseg = len(segs)
    q_blk0 = q_row0 // tq
    hp = ATTN_HEADS_PER_STEP
    in_specs = [pl.BlockSpec((tq, hp * MLA_HEAD_PAD), lambda b, h, i: (q_blk0 + b * nq + i, h))]
    for row0, length in segs:
        in_specs.append(pl.BlockSpec((length, hp * MLA_HEAD_PAD), lambda b, h, i, o=row0 // length: (o + b, h)))
    for row0, length in segs:
        in_specs.append(pl.BlockSpec((length, hp * MLA_NOPE), lambda b, h, i, o=row0 // length: (o + b, h)))
    return pl.pallas_call(
        functools.partial(_attn_kernel, nseg=nseg),
        grid=(n_batch, MLA_HEADS // hp, nq),
        in_specs=in_specs,
        out_specs=pl.BlockSpec((tq, hp * MLA_NOPE), lambda b, h, i: (b * nq + i, h)),
        out_shape=jax.ShapeDtypeStruct((n_batch * s_q, MLA_HEADS * MLA_NOPE), BF16),
        compiler_params=_params("arbitrary", "arbitrary", "arbitrary"),
        name="mla_attention",
    )(q, *([k] * nseg), *([v] * nseg))


def _rope_tables():
    nf = MLA_ROPE // 4
    inv_freq = jnp.power(ROPE_BASE, -jnp.arange(nf, dtype=F32) / nf)
    tok = jnp.arange(DEC_SEQ)
    row = (tok // GRID_W).astype(F32)[:, None] * inv_freq[None, :]
    col = (tok % GRID_W).astype(F32)[:, None] * inv_freq[None, :]
    ang = jnp.concatenate([row, col], axis=-1)
    cos, sin = jnp.cos(ang), jnp.sin(ang)
    z32 = jnp.zeros_like(cos)
    z64 = jnp.zeros((DEC_SEQ, 64), F32)
    c = jnp.concatenate([cos, cos, z64], axis=-1)
    s1 = jnp.concatenate([-sin, z32, z64], axis=-1)
    s2 = jnp.concatenate([z32, sin, z64], axis=-1)
    ident_c = jnp.concatenate([jnp.ones((1, 64), F32), jnp.zeros((1, 64), F32)], axis=-1)

    def full(lat, ident):
        n_cache = DEC_BATCH * PAST_LEN
        return jnp.concatenate([
            jnp.broadcast_to(ident, (T_CTX, LANES)),
            jnp.tile(lat, (DEC_BATCH, 1)),
            jnp.broadcast_to(ident, (n_cache, LANES)),
        ], axis=0)

    zero = jnp.zeros((1, LANES), F32)
    return full(c, ident_c), full(s1, zero), full(s2, zero)


_ROPE_PERM = np.concatenate([np.arange(0, 16), np.arange(32, 48), np.arange(16, 32), np.arange(48, 64)])


def mla_layer(h, cache_ckv, cache_krope, w_in, g_q, g_kv, w_uq, g_qn, w_uk, w_uv, g_kn, w_o):
    d = h.shape[1]
    perm = _ROPE_PERM
    n_lat = MLA_Q_RANK + MLA_KV_RANK
    w_in_ext = jnp.concatenate([w_in[:, :n_lat], w_in[:, n_lat:][:, perm], w_in[:, n_lat:]], axis=1).astype(BF16)
    cq, ckv, kr = mla_in_proj(h, w_in_ext, g_q, g_kv)

    w3 = w_uq.reshape(MLA_Q_RANK, MLA_HEADS, MLA_QK_DIM)
    w_uq_ext = jnp.concatenate([
        w3[:, :, :MLA_NOPE], w3[:, :, MLA_NOPE:][:, :, perm],
        jnp.zeros((MLA_Q_RANK, MLA_HEADS, MLA_HEAD_PAD - MLA_QK_DIM), w_uq.dtype)], axis=-1)
    w_uq_ext = w_uq_ext.reshape(MLA_Q_RANK, MLA_HEADS * MLA_HEAD_PAD).astype(BF16)

    def gain_ext(g):
        return jnp.concatenate([g[:MLA_NOPE], g[MLA_NOPE:][perm],
                                jnp.zeros((MLA_HEAD_PAD - MLA_QK_DIM,), g.dtype)]).reshape(1, MLA_HEAD_PAD)

    rope_c, rope_s1, rope_s2 = _rope_tables()
    q = mla_q_up(cq, w_uq_ext, gain_ext(g_qn), rope_c[:T_ALL], rope_s1[:T_ALL], rope_s2[:T_ALL])

    n_cache = DEC_BATCH * PAST_LEN
    ckv_all = jnp.concatenate([ckv, cache_ckv.reshape(n_cache, MLA_KV_RANK)], axis=0)
    kr_cache = cache_krope.reshape(n_cache, MLA_ROPE)
    kr_cache = jnp.concatenate([kr_cache[:, perm], kr_cache], axis=1)
    kr_all = jnp.concatenate([kr, kr_cache], axis=0)
    k, v = mla_kv_up(ckv_all, kr_all, w_uk.astype(BF16), w_uv.astype(BF16), gain_ext(g_kn),
                     rope_c, rope_s1, rope_s2)

    o_ctx = mla_attention(q, k, v, q_row0=0, n_batch=BATCH, s_q=SEQ, segs=[(0, SEQ)])
    o_lat = mla_attention(q, k, v, q_row0=T_CTX, n_batch=DEC_BATCH, s_q=DEC_SEQ,
                          segs=[(T_ALL, PAST_LEN), (T_CTX, DEC_SEQ)])
    o = jnp.concatenate([o_ctx, o_lat], axis=0)
    y = matmul(o, w_o.astype(BF16), tm=1024, tn=1024, out_dtype=F32)
    new_ckv = ckv[:T_CTX].reshape(BATCH, 1, SEQ, MLA_KV_RANK)
    new_krope = kr[:T_CTX, MLA_ROPE:].reshape(BATCH, 1, SEQ, MLA_ROPE)
    return y, new_ckv, new_krope


def _log_sigmoid(x):
    return jnp.minimum(x, 0.0) - jnp.log(1.0 + jnp.exp(-jnp.abs(x)))


def _gates_kernel(h_ref, wg_ref, wgt_ref, b_ref, bt_ref, gc_ref, gr_ref):
    h = h_ref[...]
    gc = jnp.dot(h, wg_ref[...], preferred_element_type=F32) + b_ref[...]
    gr = lax.dot_general(wgt_ref[...], h, (((1,), (1,)), ((), ())), preferred_element_type=F32) + bt_ref[...]
    lane = lax.broadcasted_iota(I32, (1, LANES), 1)
    is_f = ((lane >= ML_HEADS) & (lane < 2 * ML_HEADS)) | ((lane >= 3 * ML_HEADS) & (lane < 4 * ML_HEADS))
    gc_ref[...] = jnp.where(is_f, _log_sigmoid(gc), gc)
    row = lax.broadcasted_iota(I32, (4 * ML_HEADS, 1), 0)
    is_fr = ((row >= ML_HEADS) & (row < 2 * ML_HEADS)) | ((row >= 3 * ML_HEADS) & (row < 4 * ML_HEADS))
    gr_ref[...] = jnp.where(is_fr, _log_sigmoid(gr), gr)


def mlstm_gates(h, w_g, b_gate):
    t, d = h.shape
    ng = 4 * ML_HEADS
    tm = 512
    wg = jnp.concatenate([w_g, jnp.zeros((d, LANES - ng), w_g.dtype)], axis=1).astype(BF16)
    wgt = w_g.T.astype(BF16)
    b = jnp.concatenate([b_gate, jnp.zeros((LANES - ng,), b_gate.dtype)]).reshape(1, LANES)
    bt = b_gate.reshape(ng, 1)
    return pl.pallas_call(
        _gates_kernel,
        grid=(t // tm,),
        in_specs=[
            pl.BlockSpec((tm, d), lambda i: (i, 0)),
            pl.BlockSpec((d, LANES), lambda i: (0, 0)),
            pl.BlockSpec((ng, d), lambda i: (0, 0)),
            pl.BlockSpec((1, LANES), lambda i: (0, 0)),
            pl.BlockSpec((ng, 1), lambda i: (0, 0)),
        ],
        out_specs=[pl.BlockSpec((tm, LANES), lambda i: (i, 0)), pl.BlockSpec((ng, tm), lambda i: (0, i))],
        out_shape=[jax.ShapeDtypeStruct((t, LANES), F32), jax.ShapeDtypeStruct((ng, t), F32)],
        compiler_params=_params("arbitrary"),
        name="mlstm_gates",
    )(h, wg, wgt, b, bt)


def _mlstm_kernel(*refs, nc, has_init, emit_state):
    refs = list(refs)
    q_ref, k_ref, v_ref, o_ref, gc_ref, gr_ref, gh_ref = [refs.pop(0) for _ in range(7)]
    if has_init:
        c0_ref, n0_ref, m0_ref = [refs.pop(0) for _ in range(3)]
    y_ref = refs.pop(0)
    if emit_state:
        cf_ref, nf_ref, mf_ref = [refs.pop(0) for _ in range(3)]
    mem_s, nrm_s, m_s, hs_s = refs[0:2], refs[2:4], refs[4:6], refs[6:8]

    L = ML_CHUNK
    tt = lax.broadcasted_iota(I32, (L, L), 0)
    ss = lax.broadcasted_iota(I32, (L, L), 1)
    q_scale = ML_DK ** -0.5
    nt = (((1,), (1,)), ((), ()))
    tn = (((0,), (0,)), ((), ()))

    for d in range(2):
        if has_init:
            mem_s[d][...] = c0_ref[0, d, 0]
            nrm_s[d][...] = n0_ref[0, d, 0]
            m_s[d][...] = m0_ref[0, d, 0]
        else:
            mem_s[d][...] = jnp.zeros(mem_s[d].shape, F32)
            nrm_s[d][...] = jnp.zeros(nrm_s[d].shape, F32)
            m_s[d][...] = jnp.zeros(m_s[d].shape, F32)

    def chunk(d, c):
        causal = (ss <= tt) if d == 0 else (ss >= tt)
        causal_t = (tt <= ss) if d == 0 else (tt >= ss)
        last = L - 1 if d == 0 else 0
        gcol = gc_ref[0, 0, c]
        grow = gr_ref[0, 0, c]
        i_col = gcol[:, 2 * d:2 * d + 1]
        f_col = gcol[:, 2 * d + 1:2 * d + 2]
        i_row = grow[2 * d:2 * d + 1, :]
        f_row = grow[2 * d + 1:2 * d + 2, :]
        cum_col = jnp.sum(jnp.where(causal, jnp.broadcast_to(f_row, (L, L)), 0.0), axis=1, keepdims=True)
        cum_row = jnp.sum(jnp.where(causal_t, jnp.broadcast_to(f_col, (L, L)), 0.0), axis=0, keepdims=True)
        total = cum_col[last:last + 1, :]
        m_prev = m_s[d][:, 0:1]
        dmat = jnp.where(causal, cum_col - cum_row + i_row, NEG_INF)
        inter = cum_col + m_prev
        m_t = jnp.maximum(inter, jnp.max(dmat, axis=1, keepdims=True))
        w_inter = jnp.exp(inter - m_t)
        rows = pl.ds(pl.multiple_of(c * L, L), L)
        qf = q_ref[rows, :] * q_scale
        kf = k_ref[rows, :]
        vb = v_ref[rows, :].astype(BF16)
        qb = qf.astype(BF16)
        qk = lax.dot_general(qb, kf.astype(BF16), nt, preferred_element_type=F32)
        a = jnp.exp(dmat - m_t) * qk
        mem = mem_s[d][...]
        nrm = nrm_s[d][...]
        num = (w_inter * jnp.dot(qb, mem.astype(BF16), preferred_element_type=F32)
               + jnp.dot(a.astype(BF16), vb, preferred_element_type=F32))
        den = w_inter * jnp.sum(qf * nrm, axis=1, keepdims=True) + jnp.sum(a, axis=1, keepdims=True)
        hs_s[d][rows, :] = num / jnp.maximum(jnp.abs(den), jnp.exp(-m_t))
        m_new = m_t[last:last + 1, :]
        decay = jnp.exp(total + m_prev - m_new)
        w_s = jnp.exp(total - cum_col + i_col - m_new)
        wk = w_s * kf
        mem_s[d][...] = decay * mem + lax.dot_general(wk.astype(BF16), vb, tn, preferred_element_type=F32)
        nrm_s[d][...] = decay * nrm + jnp.sum(wk, axis=0, keepdims=True)
        m_s[d][...] = jnp.broadcast_to(m_new, m_s[d].shape)

    def both(ci, carry):
        chunk(0, ci)
        chunk(1, nc - 1 - ci)
        return carry

    lax.fori_loop(0, nc, both, 0)
    if emit_state:
        for d in range(2):
            cf_ref[0, d, 0] = mem_s[d][...]
            nf_ref[0, d, 0] = nrm_s[d][...]
            mf_ref[0, d, 0] = m_s[d][...]

    hs = hs_s[0][...] + hs_s[1][...]
    hn = hs * lax.rsqrt(jnp.mean(hs * hs, axis=-1, keepdims=True) + NORM_EPS) * gh_ref[...]
    y_ref[...] = (hn * jax.nn.sigmoid(o_ref[...])).astype(y_ref.dtype)


def mlstm_scan(p, gcol, grow, g_h, *, row0, n_batch, seq, state=None, emit_state=False):
    nc = seq // ML_CHUNK
    rb0 = row0 // seq
    hk = ML_HEADS * ML_DK
    has_init = state is not None
    in_specs = [
        pl.BlockSpec((seq, ML_DK), lambda b, h: (rb0 + b, h)),
        pl.BlockSpec((seq, ML_DK), lambda b, h: (rb0 + b, ML_HEADS + h)),
        pl.BlockSpec((seq, ML_DV), lambda b, h: (rb0 + b, 2 * hk // ML_DV + h)),
        pl.BlockSpec((seq, ML_DV), lambda b, h: (rb0 + b, 2 * hk // ML_DV + ML_HEADS + h)),
        pl.BlockSpec((1, 1, nc, ML_CHUNK, 4), lambda b, h: (b, h, 0, 0, 0)),
        pl.BlockSpec((1, 1, nc, 4, ML_CHUNK), lambda b, h: (b, h, 0, 0, 0)),
        pl.BlockSpec((1, ML_DV), lambda b, h: (0, h)),
    ]
    args = [p, p, p, p, gcol, grow, g_h.reshape(1, -1)]
    c_spec = pl.BlockSpec((1, 2, 1, ML_DK, ML_DV), lambda b, h: (b, 0, h, 0, 0))
    n_spec = pl.BlockSpec((1, 2, 1, 1, ML_DK), lambda b, h: (b, 0, h, 0, 0))
    m_spec = pl.BlockSpec((1, 2, 1, 1, LANES), lambda b, h: (b, 0, h, 0, 0))
    if has_init:
        in_specs += [c_spec, n_spec, m_spec]
        args += list(state)
    out_specs = [pl.BlockSpec((seq, ML_DV), lambda b, h: (b, h))]
    out_shape = [jax.ShapeDtypeStruct((n_batch * seq, ML_HEADS * ML_DV), BF16)]
    if emit_state:
        out_specs += [c_spec, n_spec, m_spec]
        out_shape += [
            jax.ShapeDtypeStruct((n_batch, 2, ML_HEADS, ML_DK, ML_DV), F32),
            jax.ShapeDtypeStruct((n_batch, 2, ML_HEADS, 1, ML_DK), F32),
            jax.ShapeDtypeStruct((n_batch, 2, ML_HEADS, 1, LANES), F32),
        ]
    return pl.pallas_call(
        functools.partial(_mlstm_kernel, nc=nc, has_init=has_init, emit_state=emit_state),
        grid=(n_batch, ML_HEADS),
        in_specs=in_specs,
        out_specs=out_specs,
        out_shape=out_shape,
        scratch_shapes=[
            pltpu.VMEM((ML_DK, ML_DV), F32), pltpu.VMEM((ML_DK, ML_DV), F32),
            pltpu.VMEM((1, ML_DK), F32), pltpu.VMEM((1, ML_DK), F32),
            pltpu.VMEM((1, LANES), F32), pltpu.VMEM((1, LANES), F32),
            pltpu.VMEM((seq, ML_DV), F32), pltpu.VMEM((seq, ML_DV), F32),
        ],
        compiler_params=_params("arbitrary", "arbitrary"),
        name="mlstm_scan",
    )(*args)


def _gate_layouts(gc, gr, row0, n_batch, seq):
    nc = seq // ML_CHUNK
    n = n_batch * seq
    gcol = gc[row0:row0 + n, :4 * ML_HEADS].reshape(n_batch, nc, ML_CHUNK, 4, ML_HEADS).transpose(0, 4, 1, 2, 3)
    grow = gr[:, row0:row0 + n].reshape(4, ML_HEADS, n_batch, nc, ML_CHUNK).transpose(2, 1, 3, 0, 4)
    return gcol, grow


def mlstm_layer(h, state_c, state_n, state_m, w_in, b_gate, g_h, w_o):
    hk = ML_HEADS * ML_DK
    hv = ML_HEADS * ML_DV
    n_main = 2 * hk + 2 * hv
    p = matmul(h, w_in[:, :n_main].astype(BF16), tm=1024, tn=1024 + 512, out_dtype=F32)
    gc, gr = mlstm_gates(h, w_in[:, n_main:], b_gate)

    gcol, grow = _gate_layouts(gc, gr, 0, BATCH, SEQ)
    y_ctx, cf, nf, mf = mlstm_scan(p, gcol, grow, g_h, row0=0, n_batch=BATCH, seq=SEQ, emit_state=True)

    gcol, grow = _gate_layouts(gc, gr, T_CTX, DEC_BATCH, DEC_SEQ)
    c0 = state_c[:, 0]
    n0 = state_n[:, 0].reshape(DEC_BATCH, 2, ML_HEADS, 1, ML_DK)
    m0 = jnp.broadcast_to(state_m[:, 0].reshape(DEC_BATCH, 2, ML_HEADS, 1, 1), (DEC_BATCH, 2, ML_HEADS, 1, LANES))
    (y_lat,) = mlstm_scan(p, gcol, grow, g_h, row0=T_CTX, n_batch=DEC_BATCH, seq=DEC_SEQ, state=(c0, n0, m0))

    y = matmul(jnp.concatenate([y_ctx, y_lat], axis=0), w_o.astype(BF16), tm=1024, tn=1024, out_dtype=F32)
    new_c = cf.reshape(BATCH, 1, 2, ML_HEADS, ML_DK, ML_DV)
    new_n = nf.reshape(BATCH, 1, 2, ML_HEADS, ML_DK)
    new_m = mf[..., 0, 0].reshape(BATCH, 1, 2, ML_HEADS)
    return y, new_c, new_n, new_m


def _topk_rows(x, payload, n_out):
    rows = x.shape[0]
    iota = lax.broadcasted_iota(I32, x.shape, 0)
    vals, outs = [], []
    for _ in range(n_out):
        m = jnp.max(x, axis=0, keepdims=True)
        pos = jnp.min(jnp.where(x == m, iota, rows), axis=0, keepdims=True)
        sel = iota == pos
        vals.append(m)
        outs.append(pos if payload is None else jnp.max(jnp.where(sel, payload, -1), axis=0, keepdims=True))
        x = jnp.where(sel, NEG_INF, x)
    return jnp.concatenate(vals, axis=0), jnp.concatenate(outs, axis=0)


def _pair_candidates(a, b, combine, fill):
    k = PEER_TOPK
    row = lax.broadcasted_iota(I32, (SUBLANES, a.shape[1]), 0)
    blocks = [combine(a[0:1, :], b), combine(a[1:2, :], b[0:SUBLANES, :])]
    for i in range(2, SUBLANES):
        blocks.append(jnp.where(row < k // (i + 1), combine(a[i:i + 1, :], b[0:SUBLANES, :]), fill))
    blocks.append(combine(a[SUBLANES:k, :], b[0:1, :]))
    return jnp.concatenate(blocks, axis=0)


def _peer_topk_kernel(q_ref, keys_ref, e_ref, gw_ref):
    tt = q_ref.shape[0]
    half = PEER_QDIM // 2
    nt = (((1,), (1,)), ((), ()))
    e_parts, g_parts = [], []
    for hd in range(PEER_HEADS):
        sv, si = [], []
        for p in range(2):
            lo = (hd * 2 + p) * half
            qs = q_ref[:, lo:lo + half].astype(BF16)
            sc = lax.dot_general(keys_ref[p], qs, nt, preferred_element_type=F32)
            v, i = _topk_rows(sc, None, PEER_TOPK)
            sv.append(v)
            si.append(i)
        cand = _pair_candidates(sv[0], sv[1], lambda x, y: x + y, NEG_INF)
        cidx = _pair_candidates(si[0], si[1], lambda x, y: x * PEER_NKEYS + y, -1)
        best, eidx = _topk_rows(cand, cidx, PEER_TOPK)
        ex = jnp.exp(best - best[0:1, :])
        g_parts.append(ex / jnp.sum(ex, axis=0, keepdims=True))
        e_parts.append(eidx)
    e_ref[...] = jnp.concatenate(e_parts, axis=0).T
    gw_ref[...] = jnp.concatenate(g_parts, axis=0).T


def peer_topk(q, keys):
    t, n = q.shape
    tt = 128
    return pl.pallas_call(
        _peer_topk_kernel,
        grid=(t // tt,),
        in_specs=[pl.BlockSpec((tt, n), lambda i: (i, 0)),
                  pl.BlockSpec(keys.shape, lambda i: (0, 0, 0))],
        out_specs=[pl.BlockSpec((tt, PEER_SEL), lambda i: (i, 0)), pl.BlockSpec((tt, PEER_SEL), lambda i: (i, 0))],
        out_shape=[jax.ShapeDtypeStruct((t, PEER_SEL), I32), jax.ShapeDtypeStruct((t, PEER_SEL), F32)],
        compiler_params=_params("arbitrary"),
        name="peer_topk",
    )(q, keys.astype(BF16))


PEER_STEP_TOK = 16
PEER_SLABS = D_MODEL // LANES
PEER_ROWS = PEER_SEL * PEER_SLABS
_ERF_GELU_C = 0.7071067811865476


def _peer_apply_kernel(e_ref, en_ref, h_ref, gw_ref, expand_ref, expand_t_ref, uv_ref, o_ref,
                       buf_a, buf_b, zs, wexp, sem):
    i = pl.program_id(0)
    n_steps = pl.num_programs(0)
    ns = PEER_SLABS
    tt = PEER_STEP_TOK
    nt = (((1,), (1,)), ((), ()))

    def gather_copy(idx, buf, n, s):
        return pltpu.make_async_copy(uv_ref.at[idx], buf.at[:, pl.ds(n * ns, ns), :], sem.at[s])

    def issue(idx_ref, buf, s):
        for tok in range(tt):
            for k in range(PEER_SEL):
                gather_copy(idx_ref[tok, k], buf, tok * PEER_SEL + k, s).start(priority=k % 2)

    def wait(buf, s):
        pltpu.make_async_copy(buf, buf, sem.at[s]).wait()

    def compute(buf):
        sub = lax.broadcasted_iota(I32, (ns, PEER_ROWS), 0)
        col = lax.broadcasted_iota(I32, (ns, PEER_ROWS), 1)
        diag = (col % ns) == sub
        for t in range(tt):
            u_t = buf[0, pl.ds(t * PEER_ROWS, PEER_ROWS), :]
            y = lax.dot_general(h_ref[t], u_t, nt, preferred_element_type=F32)
            zs[pl.ds(t, 1), :] = jnp.sum(jnp.where(diag, y, 0.0), axis=0, keepdims=True)
        z = zs[...]
        z_hi = z.astype(BF16)
        z_lo = (z - z_hi.astype(F32)).astype(BF16)
        act = (jnp.dot(z_hi, expand_t_ref[...], preferred_element_type=F32)
               + jnp.dot(z_lo, expand_t_ref[...], preferred_element_type=F32))
        gelu = 0.5 * act * (1.0 + lax.erf(act * _ERF_GELU_C))
        w = (gw_ref[...] * gelu).astype(BF16)
        wexp[...] = jnp.dot(w, expand_ref[...], preferred_element_type=F32)
        for t in range(tt):
            wrow = wexp[pl.ds(t, 1), :]
            wbig = jnp.where(diag, jnp.broadcast_to(wrow, (ns, PEER_ROWS)), 0.0).astype(BF16)
            v_t = buf[1, pl.ds(t * PEER_ROWS, PEER_ROWS), :]
            o_ref[t] = jnp.dot(wbig, v_t, preferred_element_type=F32)

    def step(cur, s_cur, nxt, s_nxt):
        wait(cur, s_cur)
        issue(en_ref, nxt, s_nxt)
        compute(cur)

        @pl.when(i == n_steps - 1)
        def _():
            wait(nxt, s_nxt)

    @pl.when(i == 0)
    def _():
        issue(e_ref, buf_a, 0)

    @pl.when(i % 2 == 0)
    def _():
        step(buf_a, 0, buf_b, 1)

    @pl.when(i % 2 == 1)
    def _():
        step(buf_b, 1, buf_a, 0)


def peer_apply(e, h, gw, uv):
    t, d = h.shape
    ns = PEER_SLABS
    tt = PEER_STEP_TOK
    n_steps = t // tt
    h3 = h.reshape(t, ns, LANES)
    group = np.arange(PEER_ROWS) // ns
    expand = jnp.asarray(group[None, :] == np.arange(PEER_SEL)[:, None], BF16)
    out = pl.pallas_call(
        _peer_apply_kernel,
        grid=(n_steps,),
        in_specs=[
            pl.BlockSpec((tt, PEER_SEL), lambda i: (i, 0), memory_space=pltpu.SMEM),
            pl.BlockSpec((tt, PEER_SEL), lambda i: (jnp.minimum(i + 1, n_steps - 1), 0), memory_space=pltpu.SMEM),
            pl.BlockSpec((tt, ns, LANES), lambda i: (i, 0, 0)),
            pl.BlockSpec((tt, PEER_SEL), lambda i: (i, 0)),
            pl.BlockSpec((PEER_SEL, PEER_ROWS), lambda i: (0, 0)),
            pl.BlockSpec((PEER_ROWS, PEER_SEL), lambda i: (0, 0)),
            pl.BlockSpec(memory_space=pl.ANY),
        ],
        out_specs=pl.BlockSpec((tt, ns, LANES), lambda i: (i, 0, 0)),
        out_shape=jax.ShapeDtypeStruct((t, ns, LANES), F32),
        scratch_shapes=[
            pltpu.VMEM((2, tt * PEER_ROWS, LANES), BF16),
            pltpu.VMEM((2, tt * PEER_ROWS, LANES), BF16),
            pltpu.VMEM((tt, PEER_ROWS), F32),
            pltpu.VMEM((tt, PEER_ROWS), F32),
            pltpu.SemaphoreType.DMA((2,)),
        ],
        compiler_params=pltpu.CompilerParams(dimension_semantics=("arbitrary",), vmem_limit_bytes=VMEM_LIMIT_BYTES,
                                             disable_bounds_checks=True),
        name="peer_apply",
    )(e, e, h3, gw, expand, expand.T, uv)
    return out.reshape(t, d)


def peer_layer(h, w_q, keys, u_tab, v_tab):
    n_exp, d = u_tab.shape
    q = matmul(h, w_q.astype(BF16), tm=1024, tn=1024, out_dtype=F32)
    e, gw = peer_topk(q, keys)
    uv = jnp.stack([u_tab.astype(BF16).reshape(n_exp, PEER_SLABS, LANES),
                    v_tab.astype(BF16).reshape(n_exp, PEER_SLABS, LANES)], axis=1)
    return peer_apply(e, h, gw, uv)


def kernel(x_prompt, x_sample, cache_ckv, cache_krope, state_C, state_n, state_m, c, c_ctx, mod_w, mod_b, norm_mix, norm_ffn, mla_w_in, mla_g_q, mla_g_kv, mla_w_uq, mla_g_qn, mla_w_uk, mla_w_uv, mla_g_kn, mla_w_o, ml_w_in, ml_b_gate, ml_g_h, ml_w_o, peer_w_q, peer_keys, peer_u, peer_v):
    d = D_MODEL
    x = jnp.concatenate([x_prompt.reshape(T_CTX, d), x_sample.reshape(T_LAT, d)], axis=0)
    cvec = jnp.concatenate([c_ctx.reshape(1, d), c, jnp.zeros((GROUP_PAD - N_GROUPS, d), c.dtype)], axis=0)
    mod = mod_vectors(cvec, mod_w, mod_b)
    modrows = mod.reshape(mod.shape[0] * GROUP_PAD * N_MOD, 1, d)

    (h,) = resid_modulate(x, modrows, norm_g=norm_mix[0], shift=(0, 0), scale=(0, 1))
    y, new_ckv, new_krope = mla_layer(h, cache_ckv[:, 0], cache_krope[:, 0], mla_w_in[0], mla_g_q[0], mla_g_kv[0],
                                      mla_w_uq[0], mla_g_qn[0], mla_w_uk[0], mla_w_uv[0], mla_g_kn[0], mla_w_o[0])
    x, h = resid_modulate(x, modrows, y=y, gate=(0, 2), norm_g=norm_ffn[0], shift=(0, 3), scale=(0, 4))
    y = peer_layer(h, peer_w_q[0], peer_keys[0], peer_u[0], peer_v[0])

    x, h = resid_modulate(x, modrows, y=y, gate=(0, 5), norm_g=norm_mix[1], shift=(1, 0), scale=(1, 1))
    y, new_c, new_n, new_m = mlstm_layer(h, state_C, state_n, state_m, ml_w_in[0], ml_b_gate[0], ml_g_h[0], ml_w_o[0])
    x, h = resid_modulate(x, modrows, y=y, gate=(1, 2), norm_g=norm_ffn[1], shift=(1, 3), scale=(1, 4))
    y = peer_layer(h, peer_w_q[1], peer_keys[1], peer_u[1], peer_v[1])
    (x,) = resid_modulate(x, modrows, y=y, gate=(1, 5))

    y_prompt = x[:T_CTX].reshape(BATCH, SEQ, d)
    y_sample = x[T_CTX:].reshape(DEC_BATCH, DEC_SEQ, d)
    return (y_prompt, y_sample, new_ckv, new_krope, new_c, new_n, new_m)
```

```python
import functools

import numpy as np
import jax
import jax.numpy as jnp
from jax import lax
from jax.experimental import pallas as pl
from jax.experimental.pallas import tpu as pltpu

F32 = jnp.float32
BF16 = jnp.bfloat16
I32 = jnp.int32

D_MODEL = 2048
BATCH, SEQ = 32, 256
DEC_BATCH, DEC_SEQ = 8, 1024
PAST_LEN = 512
GRID_W = 64
N_MOD = 6
NORM_EPS = 1e-6
MLA_HEADS = 16
MLA_Q_RANK = 512
MLA_KV_RANK = 512
MLA_NOPE = 128
MLA_ROPE = 64
MLA_QK_DIM = MLA_NOPE + MLA_ROPE
MLA_HEAD_PAD = 256
ROPE_BASE = 10000.0
ML_HEADS = 8
ML_DV = D_MODEL // ML_HEADS
ML_DK = ML_DV // 2
ML_CHUNK = 64
PEER_HEADS = 8
PEER_NKEYS = 128
PEER_QDIM = 128
PEER_TOPK = 16
PEER_SEL = PEER_HEADS * PEER_TOPK

T_CTX = BATCH * SEQ
T_LAT = DEC_BATCH * DEC_SEQ
T_ALL = T_CTX + T_LAT
N_GROUPS = 1 + DEC_BATCH
GROUP_PAD = 16
ROW_TILE = 256

VMEM_LIMIT_BYTES = 56 * 1024 * 1024
LANES = 128
SUBLANES = 8

NEG_INF = float("-inf")


def _params(*sem):
    return pltpu.CompilerParams(dimension_semantics=sem, vmem_limit_bytes=VMEM_LIMIT_BYTES)


def _mod_kernel(c_ref, w_ref, b_ref, o_ref):
    c = c_ref[...]
    a = (c * jax.nn.sigmoid(c)).astype(BF16)
    o_ref[0] = jnp.dot(a, w_ref[0].astype(BF16), preferred_element_type=F32) + b_ref[0]


def mod_vectors(cvec, mod_w, mod_b):
    depth, d, n = mod_w.shape
    tn = 1024
    return pl.pallas_call(
        _mod_kernel,
        grid=(depth, n // tn),
        in_specs=[
            pl.BlockSpec((GROUP_PAD, d), lambda l, j: (0, 0)),
            pl.BlockSpec((1, d, tn), lambda l, j: (l, 0, j)),
            pl.BlockSpec((1, 1, tn), lambda l, j: (l, 0, j)),
        ],
        out_specs=pl.BlockSpec((1, GROUP_PAD, tn), lambda l, j: (l, 0, j)),
        out_shape=jax.ShapeDtypeStruct((depth, GROUP_PAD, n), F32),
        compiler_params=_params("arbitrary", "arbitrary"),
        name="mod_vectors",
    )(cvec, mod_w, mod_b.reshape(depth, 1, n))


def _group_of_tile(i):
    ctx_tiles = T_CTX // ROW_TILE
    tiles_per_lat = DEC_SEQ // ROW_TILE
    return jnp.where(i < ctx_tiles, 0, 1 + (i - ctx_tiles) // tiles_per_lat)


def _resmod_kernel(*refs, has_res, has_mod):
    refs = list(refs)
    x_ref = refs.pop(0)
    x = x_ref[...]
    if has_res:
        y_ref = refs.pop(0)
        gate_ref = refs.pop(0)
        x = x + gate_ref[0] * y_ref[...]
    if has_mod:
        g_ref, sh_ref, sc_ref = refs.pop(0), refs.pop(0), refs.pop(0)
    if has_res:
        xo_ref = refs.pop(0)
        xo_ref[...] = x
    if has_mod:
        h_ref = refs.pop(0)
        ms = jnp.mean(x * x, axis=-1, keepdims=True)
        yn = x * lax.rsqrt(ms + NORM_EPS) * g_ref[...]
        h_ref[...] = (yn * (1.0 + sc_ref[0]) + sh_ref[0]).astype(h_ref.dtype)


def resid_modulate(x, modrows, *, y=None, gate=None, norm_g=None, shift=None, scale=None):
    t, d = x.shape
    has_res = y is not None
    has_mod = norm_g is not None
    row_spec = pl.BlockSpec((ROW_TILE, d), lambda i: (i, 0))

    def mod_spec(layer_k):
        layer, k = layer_k
        return pl.BlockSpec((1, 1, d), lambda i: ((layer * GROUP_PAD + _group_of_tile(i)) * N_MOD + k, 0, 0))

    args, in_specs = [x], [row_spec]
    if has_res:
        args += [y, modrows]
        in_specs += [row_spec, mod_spec(gate)]
    if has_mod:
        args += [norm_g.reshape(1, d), modrows, modrows]
        in_specs += [pl.BlockSpec((1, d), lambda i: (0, 0)), mod_spec(shift), mod_spec(scale)]
    out_shape, out_specs = [], []
    if has_res:
        out_shape.append(jax.ShapeDtypeStruct((t, d), F32))
        out_specs.append(row_spec)
    if has_mod:
        out_shape.append(jax.ShapeDtypeStruct((t, d), BF16))
        out_specs.append(row_spec)
    outs = pl.pallas_call(
        functools.partial(_resmod_kernel, has_res=has_res, has_mod=has_mod),
        grid=(t // ROW_TILE,),
        in_specs=in_specs,
        out_specs=out_specs,
        out_shape=out_shape,
        compiler_params=_params("arbitrary"),
        name="resid_modulate",
    )(*args)
    return outs


def _mm_kernel(x_ref, w_ref, o_ref):
    o_ref[...] = jnp.dot(x_ref[...].astype(BF16), w_ref[...], preferred_element_type=F32).astype(o_ref.dtype)


def matmul(x, w, *, tm, tn, out_dtype, n_cols=None):
    m, k = x.shape
    n = w.shape[1] if n_cols is None else n_cols
    return pl.pallas_call(
        _mm_kernel,
        grid=(m // tm, n // tn),
        in_specs=[pl.BlockSpec((tm, k), lambda i, j: (i, 0)), pl.BlockSpec((k, tn), lambda i, j: (0, j))],
        out_specs=pl.BlockSpec((tm, tn), lambda i, j: (i, j)),
        out_shape=jax.ShapeDtypeStruct((m, n), out_dtype),
        compiler_params=_params("arbitrary", "arbitrary"),
        name="matmul",
    )(x, w)


def _mm2_kernel(xa_ref, xb_ref, w_ref, o_ref, *, tiles_a):
    i = pl.program_id(0)

    @pl.when(i < tiles_a)
    def _():
        o_ref[...] = jnp.dot(xa_ref[...], w_ref[...], preferred_element_type=F32).astype(o_ref.dtype)

    @pl.when(i >= tiles_a)
    def _():
        o_ref[...] = jnp.dot(xb_ref[...], w_ref[...], preferred_element_type=F32).astype(o_ref.dtype)


def matmul_stacked(xa, xb, w, *, tm, tn, out_dtype):
    ma, k = xa.shape
    mb = xb.shape[0]
    n = w.shape[1]
    tiles_a = ma // tm
    last_a = tiles_a - 1
    return pl.pallas_call(
        functools.partial(_mm2_kernel, tiles_a=tiles_a),
        grid=((ma + mb) // tm, n // tn),
        in_specs=[
            pl.BlockSpec((tm, k), lambda i, j: (jnp.minimum(i, last_a), 0)),
            pl.BlockSpec((tm, k), lambda i, j: (jnp.maximum(i - tiles_a, 0), 0)),
            pl.BlockSpec((k, tn), lambda i, j: (0, j)),
        ],
        out_specs=pl.BlockSpec((tm, tn), lambda i, j: (i, j)),
        out_shape=jax.ShapeDtypeStruct((ma + mb, n), out_dtype),
        compiler_params=_params("arbitrary", "arbitrary"),
        name="matmul_stacked",
    )(xa, xb, w)


def _rms(x, n):
    return lax.rsqrt(jnp.sum(x * x, axis=-1, keepdims=True) / n + NORM_EPS)


def _mla_in_kernel(h_ref, w_ref, gq_ref, gkv_ref, cq_ref, ckv_ref, kr_ref):
    a = jnp.dot(h_ref[...], w_ref[...], preferred_element_type=F32)
    cq = a[:, :MLA_Q_RANK]
    ckv = a[:, MLA_Q_RANK:MLA_Q_RANK + MLA_KV_RANK]
    cq_ref[...] = (cq * _rms(cq, MLA_Q_RANK) * gq_ref[...]).astype(cq_ref.dtype)
    ckv_ref[...] = ckv * _rms(ckv, MLA_KV_RANK) * gkv_ref[...]
    kr_ref[...] = a[:, MLA_Q_RANK + MLA_KV_RANK:]


def mla_in_proj(h, w_ext, g_q, g_kv):
    t, d = h.shape
    n = w_ext.shape[1]
    tm = 512
    return pl.pallas_call(
        _mla_in_kernel,
        grid=(t // tm,),
        in_specs=[
            pl.BlockSpec((tm, d), lambda i: (i, 0)),
            pl.BlockSpec((d, n), lambda i: (0, 0)),
            pl.BlockSpec((1, MLA_Q_RANK), lambda i: (0, 0)),
            pl.BlockSpec((1, MLA_KV_RANK), lambda i: (0, 0)),
        ],
        out_specs=[
            pl.BlockSpec((tm, MLA_Q_RANK), lambda i: (i, 0)),
            pl.BlockSpec((tm, MLA_KV_RANK), lambda i: (i, 0)),
            pl.BlockSpec((tm, LANES), lambda i: (i, 0)),
        ],
        out_shape=[
            jax.ShapeDtypeStruct((t, MLA_Q_RANK), BF16),
            jax.ShapeDtypeStruct((t, MLA_KV_RANK), F32),
            jax.ShapeDtypeStruct((t, LANES), F32),
        ],
        compiler_params=_params("arbitrary"),
        name="mla_in_proj",
    )(h, w_ext, g_q.reshape(1, -1), g_kv.reshape(1, -1))


def _rope_block(x, c, s1, s2):
    return x * c + pltpu.roll(x, 96, 1) * s1 + pltpu.roll(x, 32, 1) * s2


def _q_up_kernel(cq_ref, w_ref, g_ref, c_ref, s1_ref, s2_ref, q_ref, *, heads):
    a = jnp.dot(cq_ref[...], w_ref[...], preferred_element_type=F32)
    g = g_ref[...]
    for hh in range(heads):
        base = hh * MLA_HEAD_PAD
        nope = a[:, base:base + MLA_NOPE]
        rp = a[:, base + MLA_NOPE:base + MLA_HEAD_PAD]
        ss = jnp.sum(nope * nope, axis=-1, keepdims=True) + jnp.sum(rp * rp, axis=-1, keepdims=True)
        r = lax.rsqrt(ss / MLA_QK_DIM + NORM_EPS)
        xr = _rope_block(rp * r * g[:, MLA_NOPE:], c_ref[...], s1_ref[...], s2_ref[...])
        q_ref[:, base:base + MLA_NOPE] = (nope * r * g[:, :MLA_NOPE]).astype(q_ref.dtype)
        q_ref[:, base + MLA_NOPE:base + MLA_HEAD_PAD] = xr.astype(q_ref.dtype)


def mla_q_up(cq, w_uq_ext, g_qn_ext, rope_c, rope_s1, rope_s2):
    t, r = cq.shape
    n = w_uq_ext.shape[1]
    tm, heads = 512, 2
    tn = heads * MLA_HEAD_PAD
    tab = pl.BlockSpec((tm, LANES), lambda i, j: (i, 0))
    return pl.pallas_call(
        functools.partial(_q_up_kernel, heads=heads),
        grid=(t // tm, n // tn),
        in_specs=[
            pl.BlockSpec((tm, r), lambda i, j: (i, 0)),
            pl.BlockSpec((r, tn), lambda i, j: (0, j)),
            pl.BlockSpec((1, MLA_HEAD_PAD), lambda i, j: (0, 0)),
            tab, tab, tab,
        ],
        out_specs=pl.BlockSpec((tm, tn), lambda i, j: (i, j)),
        out_shape=jax.ShapeDtypeStruct((t, n), BF16),
        compiler_params=_params("arbitrary", "arbitrary"),
        name="mla_q_up",
    )(cq, w_uq_ext, g_qn_ext, rope_c, rope_s1, rope_s2)


def _kv_up_kernel(ckv_ref, kr_ref, wk_ref, wv_ref, g_ref, c_ref, s1_ref, s2_ref, k_ref, v_ref, *, heads):
    ckv = ckv_ref[...].astype(BF16)
    kn = jnp.dot(ckv, wk_ref[...], preferred_element_type=F32)
    v_ref[...] = jnp.dot(ckv, wv_ref[...], preferred_element_type=F32).astype(v_ref.dtype)
    g = g_ref[...]
    lane = lax.broadcasted_iota(I32, (1, LANES), 1)
    kr = jnp.where(lane < MLA_ROPE, kr_ref[...], 0.0)
    ss_r = jnp.sum(kr * kr, axis=-1, keepdims=True)
    krot = _rope_block(kr * g[:, MLA_NOPE:], c_ref[...], s1_ref[...], s2_ref[...])
    for hh in range(heads):
        nope = kn[:, hh * MLA_NOPE:(hh + 1) * MLA_NOPE]
        r = lax.rsqrt((jnp.sum(nope * nope, axis=-1, keepdims=True) + ss_r) / MLA_QK_DIM + NORM_EPS)
        base = hh * MLA_HEAD_PAD
        k_ref[:, base:base + MLA_NOPE] = (nope * r * g[:, :MLA_NOPE]).astype(k_ref.dtype)
        k_ref[:, base + MLA_NOPE:base + MLA_HEAD_PAD] = (krot * r).astype(k_ref.dtype)


def mla_kv_up(ckv, kr, w_uk, w_uv, g_kn_ext, rope_c, rope_s1, rope_s2):
    t, r = ckv.shape
    tm, heads = 512, 2
    tab = pl.BlockSpec((tm, LANES), lambda i, j: (i, 0))
    return pl.pallas_call(
        functools.partial(_kv_up_kernel, heads=heads),
        grid=(t // tm, MLA_HEADS // heads),
        in_specs=[
            pl.BlockSpec((tm, r), lambda i, j: (i, 0)),
            tab,
            pl.BlockSpec((r, heads * MLA_NOPE), lambda i, j: (0, j)),
            pl.BlockSpec((r, heads * MLA_NOPE), lambda i, j: (0, j)),
            pl.BlockSpec((1, MLA_HEAD_PAD), lambda i, j: (0, 0)),
            tab, tab, tab,
        ],
        out_specs=[
            pl.BlockSpec((tm, heads * MLA_HEAD_PAD), lambda i, j: (i, j)),
            pl.BlockSpec((tm, heads * MLA_NOPE), lambda i, j: (i, j)),
        ],
        out_shape=[
            jax.ShapeDtypeStruct((t, MLA_HEADS * MLA_HEAD_PAD), BF16),
            jax.ShapeDtypeStruct((t, MLA_HEADS * MLA_NOPE), BF16),
        ],
        compiler_params=_params("arbitrary", "arbitrary"),
        name="mla_kv_up",
    )(ckv, kr, w_uk, w_uv, g_kn_ext, rope_c, rope_s1, rope_s2)


ATTN_HEADS_PER_STEP = 4


def _attn_kernel(*refs, nseg):
    q_ref = refs[0]
    k_refs = refs[1:1 + nseg]
    v_refs = refs[1 + nseg:1 + 2 * nseg]
    o_ref = refs[-1]
    scale = MLA_QK_DIM ** -0.5
    nt = (((1,), (1,)), ((), ()))
    for hh in range(ATTN_HEADS_PER_STEP):
        qk_cols = slice(hh * MLA_HEAD_PAD, (hh + 1) * MLA_HEAD_PAD)
        v_cols = slice(hh * MLA_NOPE, (hh + 1) * MLA_NOPE)
        q = q_ref[:, qk_cols]
        s = [lax.dot_general(q, k[:, qk_cols], nt, preferred_element_type=F32) * scale for k in k_refs]
        m = functools.reduce(jnp.maximum, [jnp.max(x, axis=-1, keepdims=True) for x in s])
        e = [jnp.exp(x - m) for x in s]
        inv = 1.0 / functools.reduce(lambda a, b: a + b, [jnp.sum(x, axis=-1, keepdims=True) for x in e])
        o = functools.reduce(
            lambda a, b: a + b,
            [jnp.dot((x * inv).astype(BF16), v[:, v_cols], preferred_element_type=F32) for x, v in zip(e, v_refs)])
        o_ref[:, v_cols] = o.astype(o_ref.dtype)


def mla_attention(q, k, v, *, q_row0, n_batch, s_q, segs):
    tq = 256
    nq = s_q // tq
    nseg = len(segs)
    q_blk0 = q_row0 // tq
    hp = ATTN_HEADS_PER_STEP
    in_specs = [pl.BlockSpec((tq, hp * MLA_HEAD_PAD), lambda b, h, i: (q_blk0 + b * nq + i, h))]
    for row0, length in segs:
        in_specs.append(pl.BlockSpec((length, hp * MLA_HEAD_PAD), lambda b, h, i, o=row0 // length: (o + b, h)))
    for row0, length in segs:
        in_specs.append(pl.BlockSpec((length, hp * MLA_NOPE), lambda b, h, i, o=row0 // length: (o + b, h)))
    return pl.pallas_call(
        functools.partial(_attn_kernel, nseg=nseg),
        grid=(n_batch, MLA_HEADS // hp, nq),
        in_specs=in_specs,
        out_specs=pl.BlockSpec((tq, hp * MLA_NOPE), lambda b, h, i: (b * nq + i, h)),
        out_shape=jax.ShapeDtypeStruct((n_batch * s_q, MLA_HEADS * MLA_NOPE), BF16),
        compiler_params=_params("arbitrary", "arbitrary", "arbitrary"),
        name="mla_attention",
    )(q, *([k] * nseg), *([v] * nseg))


def _rope_tables():
    nf = MLA_ROPE // 4
    inv_freq = jnp.power(ROPE_BASE, -jnp.arange(nf, dtype=F32) / nf)
    tok = jnp.arange(DEC_SEQ)
    row = (tok // GRID_W).astype(F32)[:, None] * inv_freq[None, :]
    col = (tok % GRID_W).astype(F32)[:, None] * inv_freq[None, :]
    ang = jnp.concatenate([row, col], axis=-1)
    cos, sin = jnp.cos(ang), jnp.sin(ang)
    z32 = jnp.zeros_like(cos)
    z64 = jnp.zeros((DEC_SEQ, 64), F32)
    c = jnp.concatenate([cos, cos, z64], axis=-1)
    s1 = jnp.concatenate([-sin, z32, z64], axis=-1)
    s2 = jnp.concatenate([z32, sin, z64], axis=-1)
    ident_c = jnp.concatenate([jnp.ones((1, 64), F32), jnp.zeros((1, 64), F32)], axis=-1)

    def full(lat, ident):
        n_cache = DEC_BATCH * PAST_LEN
        return jnp.concatenate([
            jnp.broadcast_to(ident, (T_CTX, LANES)),
            jnp.tile(lat, (DEC_BATCH, 1)),
            jnp.broadcast_to(ident, (n_cache, LANES)),
        ], axis=0)

    zero = jnp.zeros((1, LANES), F32)
    return full(c, ident_c), full(s1, zero), full(s2, zero)


_ROPE_PERM = np.concatenate([np.arange(0, 16), np.arange(32, 48), np.arange(16, 32), np.arange(48, 64)])


def mla_layer(h, cache_ckv, cache_krope, w_in, g_q, g_kv, w_uq, g_qn, w_uk, w_uv, g_kn, w_o):
    d = h.shape[1]
    perm = _ROPE_PERM
    n_lat = MLA_Q_RANK + MLA_KV_RANK
    w_in_ext = jnp.concatenate([w_in[:, :n_lat], w_in[:, n_lat:][:, perm], w_in[:, n_lat:]], axis=1).astype(BF16)
    cq, ckv, kr = mla_in_proj(h, w_in_ext, g_q, g_kv)

    w3 = w_uq.reshape(MLA_Q_RANK, MLA_HEADS, MLA_QK_DIM)
    w_uq_ext = jnp.concatenate([
        w3[:, :, :MLA_NOPE], w3[:, :, MLA_NOPE:][:, :, perm],
        jnp.zeros((MLA_Q_RANK, MLA_HEADS, MLA_HEAD_PAD - MLA_QK_DIM), w_uq.dtype)], axis=-1)
    w_uq_ext = w_uq_ext.reshape(MLA_Q_RANK, MLA_HEADS * MLA_HEAD_PAD).astype(BF16)

    def gain_ext(g):
        return jnp.concatenate([g[:MLA_NOPE], g[MLA_NOPE:][perm],
                                jnp.zeros((MLA_HEAD_PAD - MLA_QK_DIM,), g.dtype)]).reshape(1, MLA_HEAD_PAD)

    rope_c, rope_s1, rope_s2 = _rope_tables()
    q = mla_q_up(cq, w_uq_ext, gain_ext(g_qn), rope_c[:T_ALL], rope_s1[:T_ALL], rope_s2[:T_ALL])

    n_cache = DEC_BATCH * PAST_LEN
    ckv_all = jnp.concatenate([ckv, cache_ckv.reshape(n_cache, MLA_KV_RANK)], axis=0)
    kr_cache = cache_krope.reshape(n_cache, MLA_ROPE)
    kr_cache = jnp.concatenate([kr_cache[:, perm], kr_cache], axis=1)
    kr_all = jnp.concatenate([kr, kr_cache], axis=0)
    k, v = mla_kv_up(ckv_all, kr_all, w_uk.astype(BF16), w_uv.astype(BF16), gain_ext(g_kn),
                     rope_c, rope_s1, rope_s2)

    o_ctx = mla_attention(q, k, v, q_row0=0, n_batch=BATCH, s_q=SEQ, segs=[(0, SEQ)])
    o_lat = mla_attention(q, k, v, q_row0=T_CTX, n_batch=DEC_BATCH, s_q=DEC_SEQ,
                          segs=[(T_ALL, PAST_LEN), (T_CTX, DEC_SEQ)])
    y = matmul_stacked(o_ctx, o_lat, w_o.astype(BF16), tm=1024, tn=1024, out_dtype=F32)
    new_ckv = ckv[:T_CTX].reshape(BATCH, 1, SEQ, MLA_KV_RANK)
    new_krope = kr[:T_CTX, MLA_ROPE:].reshape(BATCH, 1, SEQ, MLA_ROPE)
    return y, new_ckv, new_krope


def _log_sigmoid(x):
    return jnp.minimum(x, 0.0) - jnp.log(1.0 + jnp.exp(-jnp.abs(x)))


def _gates_kernel(h_ref, wg_ref, wgt_ref, b_ref, bt_ref, gc_ref, gr_ref):
    h = h_ref[...]
    gc = jnp.dot(h, wg_ref[...], preferred_element_type=F32) + b_ref[...]
    gr = lax.dot_general(wgt_ref[...], h, (((1,), (1,)), ((), ())), preferred_element_type=F32) + bt_ref[...]
    lane = lax.broadcasted_iota(I32, (1, LANES), 1)
    is_f = ((lane >= ML_HEADS) & (lane < 2 * ML_HEADS)) | ((lane >= 3 * ML_HEADS) & (lane < 4 * ML_HEADS))
    gc_ref[...] = jnp.where(is_f, _log_sigmoid(gc), gc)
    row = lax.broadcasted_iota(I32, (4 * ML_HEADS, 1), 0)
    is_fr = ((row >= ML_HEADS) & (row < 2 * ML_HEADS)) | ((row >= 3 * ML_HEADS) & (row < 4 * ML_HEADS))
    gr_ref[...] = jnp.where(is_fr, _log_sigmoid(gr), gr)


def mlstm_gates(h, w_g, b_gate):
    t, d = h.shape
    ng = 4 * ML_HEADS
    tm = 512
    wg = jnp.concatenate([w_g, jnp.zeros((d, LANES - ng), w_g.dtype)], axis=1).astype(BF16)
    wgt = w_g.T.astype(BF16)
    b = jnp.concatenate([b_gate, jnp.zeros((LANES - ng,), b_gate.dtype)]).reshape(1, LANES)
    bt = b_gate.reshape(ng, 1)
    return pl.pallas_call(
        _gates_kernel,
        grid=(t // tm,),
        in_specs=[
            pl.BlockSpec((tm, d), lambda i: (i, 0)),
            pl.BlockSpec((d, LANES), lambda i: (0, 0)),
            pl.BlockSpec((ng, d), lambda i: (0, 0)),
            pl.BlockSpec((1, LANES), lambda i: (0, 0)),
            pl.BlockSpec((ng, 1), lambda i: (0, 0)),
        ],
        out_specs=[pl.BlockSpec((tm, LANES), lambda i: (i, 0)), pl.BlockSpec((ng, tm), lambda i: (0, i))],
        out_shape=[jax.ShapeDtypeStruct((t, LANES), F32), jax.ShapeDtypeStruct((ng, t), F32)],
        compiler_params=_params("arbitrary"),
        name="mlstm_gates",
    )(h, wg, wgt, b, bt)


def _mlstm_kernel(*refs, nc, has_init, emit_state):
    refs = list(refs)
    q_ref, k_ref, v_ref, o_ref, gc_ref, gr_ref, gh_ref = [refs.pop(0) for _ in range(7)]
    if has_init:
        c0_ref, n0_ref, m0_ref = [refs.pop(0) for _ in range(3)]
    y_ref = refs.pop(0)
    if emit_state:
        cf_ref, nf_ref, mf_ref = [refs.pop(0) for _ in range(3)]
    mem_s, nrm_s, m_s, hs_s = refs[0:2], refs[2:4], refs[4:6], refs[6:8]

    L = ML_CHUNK
    tt = lax.broadcasted_iota(I32, (L, L), 0)
    ss = lax.broadcasted_iota(I32, (L, L), 1)
    q_scale = ML_DK ** -0.5
    nt = (((1,), (1,)), ((), ()))
    tn = (((0,), (0,)), ((), ()))

    for d in range(2):
        if has_init:
            mem_s[d][...] = c0_ref[0, d, 0]
            nrm_s[d][...] = n0_ref[0, d, 0]
            m_s[d][...] = m0_ref[0, d, 0]
        else:
            mem_s[d][...] = jnp.zeros(mem_s[d].shape, F32)
            nrm_s[d][...] = jnp.zeros(nrm_s[d].shape, F32)
            m_s[d][...] = jnp.zeros(m_s[d].shape, F32)

    def chunk(d, c):
        causal = (ss <= tt) if d == 0 else (ss >= tt)
        causal_t = (tt <= ss) if d == 0 else (tt >= ss)
        last = L - 1 if d == 0 else 0
        gcol = gc_ref[0, 0, c]
        grow = gr_ref[0, 0, c]
        i_col = gcol[:, 2 * d:2 * d + 1]
        f_col = gcol[:, 2 * d + 1:2 * d + 2]
        i_row = grow[2 * d:2 * d + 1, :]
        f_row = grow[2 * d + 1:2 * d + 2, :]
        cum_col = jnp.sum(jnp.where(causal, jnp.broadcast_to(f_row, (L, L)), 0.0), axis=1, keepdims=True)
        cum_row = jnp.sum(jnp.where(causal_t, jnp.broadcast_to(f_col, (L, L)), 0.0), axis=0, keepdims=True)
        total = cum_col[last:last + 1, :]
        m_prev = m_s[d][:, 0:1]
        dmat = jnp.where(causal, cum_col - cum_row + i_row, NEG_INF)
        inter = cum_col + m_prev
        m_t = jnp.maximum(inter, jnp.max(dmat, axis=1, keepdims=True))
        w_inter = jnp.exp(inter - m_t)
        rows = pl.ds(pl.multiple_of(c * L, L), L)
        qf = q_ref[rows, :] * q_scale
        kf = k_ref[rows, :]
        vb = v_ref[rows, :].astype(BF16)
        qb = qf.astype(BF16)
        qk = lax.dot_general(qb, kf.astype(BF16), nt, preferred_element_type=F32)
        a = jnp.exp(dmat - m_t) * qk
        mem = mem_s[d][...]
        nrm = nrm_s[d][...]
        num = (w_inter * jnp.dot(qb, mem.astype(BF16), preferred_element_type=F32)
               + jnp.dot(a.astype(BF16), vb, preferred_element_type=F32))
        den = w_inter * jnp.sum(qf * nrm, axis=1, keepdims=True) + jnp.sum(a, axis=1, keepdims=True)
        hs_s[d][rows, :] = num / jnp.maximum(jnp.abs(den), jnp.exp(-m_t))
        m_new = m_t[last:last + 1, :]
        decay = jnp.exp(total + m_prev - m_new)
        w_s = jnp.exp(total - cum_col + i_col - m_new)
        wk = w_s * kf
        mem_s[d][...] = decay * mem + lax.dot_general(wk.astype(BF16), vb, tn, preferred_element_type=F32)
        nrm_s[d][...] = decay * nrm + jnp.sum(wk, axis=0, keepdims=True)
        m_s[d][...] = jnp.broadcast_to(m_new, m_s[d].shape)

    def both(ci, carry):
        chunk(0, ci)
        chunk(1, nc - 1 - ci)
        return carry

    lax.fori_loop(0, nc, both, 0)
    if emit_state:
        for d in range(2):
            cf_ref[0, d, 0] = mem_s[d][...]
            nf_ref[0, d, 0] = nrm_s[d][...]
            mf_ref[0, d, 0] = m_s[d][...]

    hs = hs_s[0][...] + hs_s[1][...]
    hn = hs * lax.rsqrt(jnp.mean(hs * hs, axis=-1, keepdims=True) + NORM_EPS) * gh_ref[...]
    y_ref[...] = (hn * jax.nn.sigmoid(o_ref[...])).astype(y_ref.dtype)


def mlstm_scan(p, gcol, grow, g_h, *, row0, n_batch, seq, state=None, emit_state=False):
    nc = seq // ML_CHUNK
    rb0 = row0 // seq
    hk = ML_HEADS * ML_DK
    has_init = state is not None
    in_specs = [
        pl.BlockSpec((seq, ML_DK), lambda b, h: (rb0 + b, h)),
        pl.BlockSpec((seq, ML_DK), lambda b, h: (rb0 + b, ML_HEADS + h)),
        pl.BlockSpec((seq, ML_DV), lambda b, h: (rb0 + b, 2 * hk // ML_DV + h)),
        pl.BlockSpec((seq, ML_DV), lambda b, h: (rb0 + b, 2 * hk // ML_DV + ML_HEADS + h)),
        pl.BlockSpec((1, 1, nc, ML_CHUNK, 4), lambda b, h: (b, h, 0, 0, 0)),
        pl.BlockSpec((1, 1, nc, 4, ML_CHUNK), lambda b, h: (b, h, 0, 0, 0)),
        pl.BlockSpec((1, ML_DV), lambda b, h: (0, h)),
    ]
    args = [p, p, p, p, gcol, grow, g_h.reshape(1, -1)]
    c_spec = pl.BlockSpec((1, 2, 1, ML_DK, ML_DV), lambda b, h: (b, 0, h, 0, 0))
    n_spec = pl.BlockSpec((1, 2, 1, 1, ML_DK), lambda b, h: (b, 0, h, 0, 0))
    m_spec = pl.BlockSpec((1, 2, 1, 1, LANES), lambda b, h: (b, 0, h, 0, 0))
    if has_init:
        in_specs += [c_spec, n_spec, m_spec]
        args += list(state)
    out_specs = [pl.BlockSpec((seq, ML_DV), lambda b, h: (b, h))]
    out_shape = [jax.ShapeDtypeStruct((n_batch * seq, ML_HEADS * ML_DV), BF16)]
    if emit_state:
        out_specs += [c_spec, n_spec, m_spec]
        out_shape += [
            jax.ShapeDtypeStruct((n_batch, 2, ML_HEADS, ML_DK, ML_DV), F32),
            jax.ShapeDtypeStruct((n_batch, 2, ML_HEADS, 1, ML_DK), F32),
            jax.ShapeDtypeStruct((n_batch, 2, ML_HEADS, 1, LANES), F32),
        ]
    return pl.pallas_call(
        functools.partial(_mlstm_kernel, nc=nc, has_init=has_init, emit_state=emit_state),
        grid=(n_batch, ML_HEADS),
        in_specs=in_specs,
        out_specs=out_specs,
        out_shape=out_shape,
        scratch_shapes=[
            pltpu.VMEM((ML_DK, ML_DV), F32), pltpu.VMEM((ML_DK, ML_DV), F32),
            pltpu.VMEM((1, ML_DK), F32), pltpu.VMEM((1, ML_DK), F32),
            pltpu.VMEM((1, LANES), F32), pltpu.VMEM((1, LANES), F32),
            pltpu.VMEM((seq, ML_DV), F32), pltpu.VMEM((seq, ML_DV), F32),
        ],
        compiler_params=_params("arbitrary", "arbitrary"),
        name="mlstm_scan",
    )(*args)


def _gate_layouts(gc, gr, row0, n_batch, seq):
    nc = seq // ML_CHUNK
    n = n_batch * seq
    gcol = gc[row0:row0 + n, :4 * ML_HEADS].reshape(n_batch, nc, ML_CHUNK, 4, ML_HEADS).transpose(0, 4, 1, 2, 3)
    grow = gr[:, row0:row0 + n].reshape(4, ML_HEADS, n_batch, nc, ML_CHUNK).transpose(2, 1, 3, 0, 4)
    return gcol, grow


def mlstm_layer(h, state_c, state_n, state_m, w_in, b_gate, g_h, w_o):
    hk = ML_HEADS * ML_DK
    hv = ML_HEADS * ML_DV
    n_main = 2 * hk + 2 * hv
    p = matmul(h, w_in.astype(BF16), tm=1024, tn=1024 + 512, out_dtype=F32, n_cols=n_main)
    gc, gr = mlstm_gates(h, w_in[:, n_main:], b_gate)

    gcol, grow = _gate_layouts(gc, gr, 0, BATCH, SEQ)
    y_ctx, cf, nf, mf = mlstm_scan(p, gcol, grow, g_h, row0=0, n_batch=BATCH, seq=SEQ, emit_state=True)

    gcol, grow = _gate_layouts(gc, gr, T_CTX, DEC_BATCH, DEC_SEQ)
    c0 = state_c[:, 0]
    n0 = state_n[:, 0].reshape(DEC_BATCH, 2, ML_HEADS, 1, ML_DK)
    m0 = jnp.broadcast_to(state_m[:, 0].reshape(DEC_BATCH, 2, ML_HEADS, 1, 1), (DEC_BATCH, 2, ML_HEADS, 1, LANES))
    (y_lat,) = mlstm_scan(p, gcol, grow, g_h, row0=T_CTX, n_batch=DEC_BATCH, seq=DEC_SEQ, state=(c0, n0, m0))

    y = matmul_stacked(y_ctx, y_lat, w_o.astype(BF16), tm=1024, tn=1024, out_dtype=F32)
    new_c = cf.reshape(BATCH, 1, 2, ML_HEADS, ML_DK, ML_DV)
    new_n = nf.reshape(BATCH, 1, 2, ML_HEADS, ML_DK)
    new_m = mf[..., 0, 0].reshape(BATCH, 1, 2, ML_HEADS)
    return y, new_c, new_n, new_m


def _topk_rows(x, payload, n_out):
    rows = x.shape[0]
    iota = lax.broadcasted_iota(I32, x.shape, 0)
    vals, outs = [], []
    for _ in range(n_out):
        m = jnp.max(x, axis=0, keepdims=True)
        pos = jnp.min(jnp.where(x == m, iota, rows), axis=0, keepdims=True)
        sel = iota == pos
        vals.append(m)
        outs.append(pos if payload is None else jnp.max(jnp.where(sel, payload, -1), axis=0, keepdims=True))
        x = jnp.where(sel, NEG_INF, x)
    return jnp.concatenate(vals, axis=0), jnp.concatenate(outs, axis=0)


def _pair_candidates(a, b, combine, fill):
    k = PEER_TOPK
    row = lax.broadcasted_iota(I32, (SUBLANES, a.shape[1]), 0)
    blocks = [combine(a[0:1, :], b), combine(a[1:2, :], b[0:SUBLANES, :])]
    for i in range(2, SUBLANES):
        blocks.append(jnp.where(row < k // (i + 1), combine(a[i:i + 1, :], b[0:SUBLANES, :]), fill))
    blocks.append(combine(a[SUBLANES:k, :], b[0:1, :]))
    return jnp.concatenate(blocks, axis=0)


def _peer_topk_kernel(q_ref, keys_ref, e_ref, gw_ref):
    tt = q_ref.shape[0]
    half = PEER_QDIM // 2
    nt = (((1,), (1,)), ((), ()))
    e_parts, g_parts = [], []
    for hd in range(PEER_HEADS):
        sv, si = [], []
        for p in range(2):
            lo = (hd * 2 + p) * half
            qs = q_ref[:, lo:lo + half].astype(BF16)
            sc = lax.dot_general(keys_ref[p], qs, nt, preferred_element_type=F32)
            v, i = _topk_rows(sc, None, PEER_TOPK)
            sv.append(v)
            si.append(i)
        cand = _pair_candidates(sv[0], sv[1], lambda x, y: x + y, NEG_INF)
        cidx = _pair_candidates(si[0], si[1], lambda x, y: x * PEER_NKEYS + y, -1)
        best, eidx = _topk_rows(cand, cidx, PEER_TOPK)
        ex = jnp.exp(best - best[0:1, :])
        g_parts.append(ex / jnp.sum(ex, axis=0, keepdims=True))
        e_parts.append(eidx)
    e_ref[...] = jnp.concatenate(e_parts, axis=0).T
    gw_ref[...] = jnp.concatenate(g_parts, axis=0).T


def peer_topk(q, keys):
    t, n = q.shape
    tt = 128
    return pl.pallas_call(
        _peer_topk_kernel,
        grid=(t // tt,),
        in_specs=[pl.BlockSpec((tt, n), lambda i: (i, 0)),
                  pl.BlockSpec(keys.shape, lambda i: (0, 0, 0))],
        out_specs=[pl.BlockSpec((tt, PEER_SEL), lambda i: (i, 0)), pl.BlockSpec((tt, PEER_SEL), lambda i: (i, 0))],
        out_shape=[jax.ShapeDtypeStruct((t, PEER_SEL), I32), jax.ShapeDtypeStruct((t, PEER_SEL), F32)],
        compiler_params=_params("arbitrary"),
        name="peer_topk",
    )(q, keys.astype(BF16))


PEER_STEP_TOK = 16
PEER_SLABS = D_MODEL // LANES
PEER_ROWS = PEER_SEL * PEER_SLABS
_ERF_GELU_C = 0.7071067811865476


def _peer_apply_kernel(e_ref, en_ref, h_ref, gw_ref, expand_ref, expand_t_ref, uv_ref, o_ref,
                       buf_a, buf_b, zs, wexp, sem, *, layer):
    i = pl.program_id(0)
    n_steps = pl.num_programs(0)
    ns = PEER_SLABS
    tt = PEER_STEP_TOK
    nt = (((1,), (1,)), ((), ()))

    def gather_copy(idx, buf, n, s):
        return pltpu.make_async_copy(uv_ref.at[layer, idx], buf.at[n], sem.at[s])

    def issue(idx_ref, buf, s):
        for tok in range(tt):
            for k in range(PEER_SEL):
                gather_copy(idx_ref[tok, k], buf, tok * PEER_SEL + k, s).start(priority=k % 2)

    def wait(buf, s):
        pltpu.make_async_copy(buf, buf, sem.at[s]).wait()

    def compute(buf):
        sub = lax.broadcasted_iota(I32, (ns, PEER_ROWS), 0)
        col = lax.broadcasted_iota(I32, (ns, PEER_ROWS), 1)
        diag = (col % ns) == sub
        for t in range(tt):
            u_t = buf[pl.ds(t * PEER_SEL, PEER_SEL), 0].reshape(PEER_ROWS, LANES)
            y = lax.dot_general(h_ref[t], u_t, nt, preferred_element_type=F32)
            zs[pl.ds(t, 1), :] = jnp.sum(jnp.where(diag, y, 0.0), axis=0, keepdims=True)
        z = zs[...]
        z_hi = z.astype(BF16)
        z_lo = (z - z_hi.astype(F32)).astype(BF16)
        act = (jnp.dot(z_hi, expand_t_ref[...], preferred_element_type=F32)
               + jnp.dot(z_lo, expand_t_ref[...], preferred_element_type=F32))
        gelu = 0.5 * act * (1.0 + lax.erf(act * _ERF_GELU_C))
        w = (gw_ref[...] * gelu).astype(BF16)
        wexp[...] = jnp.dot(w, expand_ref[...], preferred_element_type=F32)
        for t in range(tt):
            wrow = wexp[pl.ds(t, 1), :]
            wbig = jnp.where(diag, jnp.broadcast_to(wrow, (ns, PEER_ROWS)), 0.0).astype(BF16)
            v_t = buf[pl.ds(t * PEER_SEL, PEER_SEL), 1].reshape(PEER_ROWS, LANES)
            o_ref[t] = jnp.dot(wbig, v_t, preferred_element_type=F32)

    def step(cur, s_cur, nxt, s_nxt):
        wait(cur, s_cur)
        issue(en_ref, nxt, s_nxt)
        compute(cur)

        @pl.when(i == n_steps - 1)
        def _():
            wait(nxt, s_nxt)

    @pl.when(i == 0)
    def _():
        issue(e_ref, buf_a, 0)

    @pl.when(i % 2 == 0)
    def _():
        step(buf_a, 0, buf_b, 1)

    @pl.when(i % 2 == 1)
    def _():
        step(buf_b, 1, buf_a, 0)


def peer_apply(e, h, gw, uv, layer):
    t, d = h.shape
    ns = PEER_SLABS
    tt = PEER_STEP_TOK
    n_steps = t // tt
    h3 = h.reshape(t, ns, LANES)
    group = np.arange(PEER_ROWS) // ns
    expand = jnp.asarray(group[None, :] == np.arange(PEER_SEL)[:, None], BF16)
    out = pl.pallas_call(
        functools.partial(_peer_apply_kernel, layer=layer),
        grid=(n_steps,),
        in_specs=[
            pl.BlockSpec((tt, PEER_SEL), lambda i: (i, 0), memory_space=pltpu.SMEM),
            pl.BlockSpec((tt, PEER_SEL), lambda i: (jnp.minimum(i + 1, n_steps - 1), 0), memory_space=pltpu.SMEM),
            pl.BlockSpec((tt, ns, LANES), lambda i: (i, 0, 0)),
            pl.BlockSpec((tt, PEER_SEL), lambda i: (i, 0)),
            pl.BlockSpec((PEER_SEL, PEER_ROWS), lambda i: (0, 0)),
            pl.BlockSpec((PEER_ROWS, PEER_SEL), lambda i: (0, 0)),
            pl.BlockSpec(memory_space=pl.ANY),
        ],
        out_specs=pl.BlockSpec((tt, ns, LANES), lambda i: (i, 0, 0)),
        out_shape=jax.ShapeDtypeStruct((t, ns, LANES), F32),
        scratch_shapes=[
            pltpu.VMEM((tt * PEER_SEL, 2, ns, LANES), BF16),
            pltpu.VMEM((tt * PEER_SEL, 2, ns, LANES), BF16),
            pltpu.VMEM((tt, PEER_ROWS), F32),
            pltpu.VMEM((tt, PEER_ROWS), F32),
            pltpu.SemaphoreType.DMA((2,)),
        ],
        compiler_params=pltpu.CompilerParams(dimension_semantics=("arbitrary",), vmem_limit_bytes=VMEM_LIMIT_BYTES,
                                             disable_bounds_checks=True),
        name="peer_apply",
    )(e, e, h3, gw, expand, expand.T, uv)
    return out.reshape(t, d)


def peer_tables(peer_u, peer_v):
    depth, n_exp, _ = peer_u.shape
    shape = (depth, n_exp, PEER_SLABS, LANES)
    return jnp.stack([peer_u.astype(BF16).reshape(shape), peer_v.astype(BF16).reshape(shape)], axis=2)


def peer_layer(h, w_q, keys, uv, layer):
    q = matmul(h, w_q.astype(BF16), tm=1024, tn=1024, out_dtype=F32)
    e, gw = peer_topk(q, keys)
    return peer_apply(e, h, gw, uv, layer)


def kernel(x_prompt, x_sample, cache_ckv, cache_krope, state_C, state_n, state_m, c, c_ctx, mod_w, mod_b, norm_mix, norm_ffn, mla_w_in, mla_g_q, mla_g_kv, mla_w_uq, mla_g_qn, mla_w_uk, mla_w_uv, mla_g_kn, mla_w_o, ml_w_in, ml_b_gate, ml_g_h, ml_w_o, peer_w_q, peer_keys, peer_u, peer_v):
    d = D_MODEL
    x = jnp.concatenate([x_prompt.reshape(T_CTX, d), x_sample.reshape(T_LAT, d)], axis=0)
    cvec = jnp.concatenate([c_ctx.reshape(1, d), c, jnp.zeros((GROUP_PAD - N_GROUPS, d), c.dtype)], axis=0)
    mod = mod_vectors(cvec, mod_w, mod_b)
    modrows = mod.reshape(mod.shape[0] * GROUP_PAD * N_MOD, 1, d)
    uv = peer_tables(peer_u, peer_v)

    (h,) = resid_modulate(x, modrows, norm_g=norm_mix[0], shift=(0, 0), scale=(0, 1))
    y, new_ckv, new_krope = mla_layer(h, cache_ckv[:, 0], cache_krope[:, 0], mla_w_in[0], mla_g_q[0], mla_g_kv[0],
                                      mla_w_uq[0], mla_g_qn[0], mla_w_uk[0], mla_w_uv[0], mla_g_kn[0], mla_w_o[0])
    x, h = resid_modulate(x, modrows, y=y, gate=(0, 2), norm_g=norm_ffn[0], shift=(0, 3), scale=(0, 4))
    y = peer_layer(h, peer_w_q[0], peer_keys[0], uv, 0)

    x, h = resid_modulate(x, modrows, y=y, gate=(0, 5), norm_g=norm_mix[1], shift=(1, 0), scale=(1, 1))
    y, new_c, new_n, new_m = mlstm_layer(h, state_C, state_n, state_m, ml_w_in[0], ml_b_gate[0], ml_g_h[0], ml_w_o[0])
    x, h = resid_modulate(x, modrows, y=y, gate=(1, 2), norm_g=norm_ffn[1], shift=(1, 3), scale=(1, 4))
    y = peer_layer(h, peer_w_q[1], peer_keys[1], uv, 1)
    (x,) = resid_modulate(x, modrows, y=y, gate=(1, 5))

    y_prompt = x[:T_CTX].reshape(BATCH, SEQ, d)
    y_sample = x[T_CTX:].reshape(DEC_BATCH, DEC_SEQ, d)
    return (y_prompt, y_sample, new_ckv, new_krope, new_c, new_n, new_m)
```

```python
import functools

import numpy as np
import jax
import jax.numpy as jnp
from jax import lax
from jax.experimental import pallas as pl
from jax.experimental.pallas import tpu as pltpu

F32 = jnp.float32
BF16 = jnp.bfloat16
I32 = jnp.int32

D_MODEL = 2048
BATCH, SEQ = 32, 256
DEC_BATCH, DEC_SEQ = 8, 1024
PAST_LEN = 512
GRID_W = 64
N_MOD = 6
NORM_EPS = 1e-6
MLA_HEADS = 16
MLA_Q_RANK = 512
MLA_KV_RANK = 512
MLA_NOPE = 128
MLA_ROPE = 64
MLA_QK_DIM = MLA_NOPE + MLA_ROPE
MLA_HEAD_PAD = 256
ROPE_BASE = 10000.0
ML_HEADS = 8
ML_DV = D_MODEL // ML_HEADS
ML_DK = ML_DV // 2
ML_CHUNK = 64
PEER_HEADS = 8
PEER_NKEYS = 128
PEER_QDIM = 128
PEER_TOPK = 16
PEER_SEL = PEER_HEADS * PEER_TOPK

T_CTX = BATCH * SEQ
T_LAT = DEC_BATCH * DEC_SEQ
T_ALL = T_CTX + T_LAT
N_GROUPS = 1 + DEC_BATCH
GROUP_PAD = 16
ROW_TILE = 256

VMEM_LIMIT_BYTES = 56 * 1024 * 1024
LANES = 128
SUBLANES = 8

NEG_INF = float("-inf")


def _params(*sem):
    return pltpu.CompilerParams(dimension_semantics=sem, vmem_limit_bytes=VMEM_LIMIT_BYTES)


def _mod_kernel(c_ref, w_ref, b_ref, o_ref):
    c = c_ref[...]
    a = (c * jax.nn.sigmoid(c)).astype(BF16)
    o_ref[0] = jnp.dot(a, w_ref[0].astype(BF16), preferred_element_type=F32) + b_ref[0]


def mod_vectors(cvec, mod_w, mod_b):
    depth, d, n = mod_w.shape
    tn = 1024
    return pl.pallas_call(
        _mod_kernel,
        grid=(depth, n // tn),
        in_specs=[
            pl.BlockSpec((GROUP_PAD, d), lambda l, j: (0, 0)),
            pl.BlockSpec((1, d, tn), lambda l, j: (l, 0, j)),
            pl.BlockSpec((1, 1, tn), lambda l, j: (l, 0, j)),
        ],
        out_specs=pl.BlockSpec((1, GROUP_PAD, tn), lambda l, j: (l, 0, j)),
        out_shape=jax.ShapeDtypeStruct((depth, GROUP_PAD, n), F32),
        compiler_params=_params("arbitrary", "arbitrary"),
        name="mod_vectors",
    )(cvec, mod_w, mod_b.reshape(depth, 1, n))


def _group_of_tile(i):
    ctx_tiles = T_CTX // ROW_TILE
    tiles_per_lat = DEC_SEQ // ROW_TILE
    return jnp.where(i < ctx_tiles, 0, 1 + (i - ctx_tiles) // tiles_per_lat)


def _resmod_kernel(*refs, has_res, has_mod):
    refs = list(refs)
    x_ref = refs.pop(0)
    x = x_ref[...]
    if has_res:
        y_ref = refs.pop(0)
        gate_ref = refs.pop(0)
        x = x + gate_ref[0] * y_ref[...]
    if has_mod:
        g_ref, sh_ref, sc_ref = refs.pop(0), refs.pop(0), refs.pop(0)
    if has_res:
        xo_ref = refs.pop(0)
        xo_ref[...] = x
    if has_mod:
        h_ref = refs.pop(0)
        ms = jnp.mean(x * x, axis=-1, keepdims=True)
        yn = x * lax.rsqrt(ms + NORM_EPS) * g_ref[...]
        h_ref[...] = (yn * (1.0 + sc_ref[0]) + sh_ref[0]).astype(h_ref.dtype)


def resid_modulate(x, modrows, *, y=None, gate=None, norm_g=None, shift=None, scale=None):
    t, d = x.shape
    has_res = y is not None
    has_mod = norm_g is not None
    row_spec = pl.BlockSpec((ROW_TILE, d), lambda i: (i, 0))

    def mod_spec(layer_k):
        layer, k = layer_k
        return pl.BlockSpec((1, 1, d), lambda i: ((layer * GROUP_PAD + _group_of_tile(i)) * N_MOD + k, 0, 0))

    args, in_specs = [x], [row_spec]
    if has_res:
        args += [y, modrows]
        in_specs += [row_spec, mod_spec(gate)]
    if has_mod:
        args += [norm_g.reshape(1, d), modrows, modrows]
        in_specs += [pl.BlockSpec((1, d), lambda i: (0, 0)), mod_spec(shift), mod_spec(scale)]
    out_shape, out_specs = [], []
    if has_res:
        out_shape.append(jax.ShapeDtypeStruct((t, d), F32))
        out_specs.append(row_spec)
    if has_mod:
        out_shape.append(jax.ShapeDtypeStruct((t, d), BF16))
        out_specs.append(row_spec)
    outs = pl.pallas_call(
        functools.partial(_resmod_kernel, has_res=has_res, has_mod=has_mod),
        grid=(t // ROW_TILE,),
        in_specs=in_specs,
        out_specs=out_specs,
        out_shape=out_shape,
        compiler_params=_params("arbitrary"),
        name="resid_modulate",
    )(*args)
    return outs


def _mm_kernel(x_ref, w_ref, o_ref):
    o_ref[...] = jnp.dot(x_ref[...].astype(BF16), w_ref[...], preferred_element_type=F32).astype(o_ref.dtype)


def matmul(x, w, *, tm, tn, out_dtype, n_cols=None):
    m, k = x.shape
    n = w.shape[1] if n_cols is None else n_cols
    return pl.pallas_call(
        _mm_kernel,
        grid=(m // tm, n // tn),
        in_specs=[pl.BlockSpec((tm, k), lambda i, j: (i, 0)), pl.BlockSpec((k, tn), lambda i, j: (0, j))],
        out_specs=pl.BlockSpec((tm, tn), lambda i, j: (i, j)),
        out_shape=jax.ShapeDtypeStruct((m, n), out_dtype),
        compiler_params=_params("arbitrary", "arbitrary"),
        name="matmul",
    )(x, w)


def _mm2_kernel(xa_ref, xb_ref, w_ref, o_ref, *, tiles_a):
    i = pl.program_id(0)

    @pl.when(i < tiles_a)
    def _():
        o_ref[...] = jnp.dot(xa_ref[...], w_ref[...], preferred_element_type=F32).astype(o_ref.dtype)

    @pl.when(i >= tiles_a)
    def _():
        o_ref[...] = jnp.dot(xb_ref[...], w_ref[...], preferred_element_type=F32).astype(o_ref.dtype)


def matmul_stacked(xa, xb, w, *, tm, tn, out_dtype):
    ma, k = xa.shape
    mb = xb.shape[0]
    n = w.shape[1]
    tiles_a = ma // tm
    last_a = tiles_a - 1
    return pl.pallas_call(
        functools.partial(_mm2_kernel, tiles_a=tiles_a),
        grid=((ma + mb) // tm, n // tn),
        in_specs=[
            pl.BlockSpec((tm, k), lambda i, j: (jnp.minimum(i, last_a), 0)),
            pl.BlockSpec((tm, k), lambda i, j: (jnp.maximum(i - tiles_a, 0), 0)),
            pl.BlockSpec((k, tn), lambda i, j: (0, j)),
        ],
        out_specs=pl.BlockSpec((tm, tn), lambda i, j: (i, j)),
        out_shape=jax.ShapeDtypeStruct((ma + mb, n), out_dtype),
        compiler_params=_params("arbitrary", "arbitrary"),
        name="matmul_stacked",
    )(xa, xb, w)


def _rms(x, n):
    return lax.rsqrt(jnp.sum(x * x, axis=-1, keepdims=True) / n + NORM_EPS)


def _mla_in_kernel(h_ref, w_ref, gq_ref, gkv_ref, cq_ref, ckv_ref, kr_ref):
    a = jnp.dot(h_ref[...], w_ref[...], preferred_element_type=F32)
    cq = a[:, :MLA_Q_RANK]
    ckv = a[:, MLA_Q_RANK:MLA_Q_RANK + MLA_KV_RANK]
    cq_ref[...] = (cq * _rms(cq, MLA_Q_RANK) * gq_ref[...]).astype(cq_ref.dtype)
    ckv_ref[...] = ckv * _rms(ckv, MLA_KV_RANK) * gkv_ref[...]
    kr_ref[...] = a[:, MLA_Q_RANK + MLA_KV_RANK:]


def mla_in_proj(h, w_ext, g_q, g_kv):
    t, d = h.shape
    n = w_ext.shape[1]
    tm = 512
    return pl.pallas_call(
        _mla_in_kernel,
        grid=(t // tm,),
        in_specs=[
            pl.BlockSpec((tm, d), lambda i: (i, 0)),
            pl.BlockSpec((d, n), lambda i: (0, 0)),
            pl.BlockSpec((1, MLA_Q_RANK), lambda i: (0, 0)),
            pl.BlockSpec((1, MLA_KV_RANK), lambda i: (0, 0)),
        ],
        out_specs=[
            pl.BlockSpec((tm, MLA_Q_RANK), lambda i: (i, 0)),
            pl.BlockSpec((tm, MLA_KV_RANK), lambda i: (i, 0)),
            pl.BlockSpec((tm, LANES), lambda i: (i, 0)),
        ],
        out_shape=[
            jax.ShapeDtypeStruct((t, MLA_Q_RANK), BF16),
            jax.ShapeDtypeStruct((t, MLA_KV_RANK), F32),
            jax.ShapeDtypeStruct((t, LANES), F32),
        ],
        compiler_params=_params("arbitrary"),
        name="mla_in_proj",
    )(h, w_ext, g_q.reshape(1, -1), g_kv.reshape(1, -1))


def _rope_block(x, c, s1, s2):
    return x * c + pltpu.roll(x, 96, 1) * s1 + pltpu.roll(x, 32, 1) * s2


def _q_up_kernel(cq_ref, w_ref, g_ref, c_ref, s1_ref, s2_ref, q_ref, *, heads):
    a = jnp.dot(cq_ref[...], w_ref[...], preferred_element_type=F32)
    g = g_ref[...]
    for hh in range(heads):
        base = hh * MLA_HEAD_PAD
        nope = a[:, base:base + MLA_NOPE]
        rp = a[:, base + MLA_NOPE:base + MLA_HEAD_PAD]
        ss = jnp.sum(nope * nope, axis=-1, keepdims=True) + jnp.sum(rp * rp, axis=-1, keepdims=True)
        r = lax.rsqrt(ss / MLA_QK_DIM + NORM_EPS)
        xr = _rope_block(rp * r * g[:, MLA_NOPE:], c_ref[...], s1_ref[...], s2_ref[...])
        q_ref[:, base:base + MLA_NOPE] = (nope * r * g[:, :MLA_NOPE]).astype(q_ref.dtype)
        q_ref[:, base + MLA_NOPE:base + MLA_HEAD_PAD] = xr.astype(q_ref.dtype)


def mla_q_up(cq, w_uq_ext, g_qn_ext, rope_c, rope_s1, rope_s2):
    t, r = cq.shape
    n = w_uq_ext.shape[1]
    tm, heads = 512, 2
    tn = heads * MLA_HEAD_PAD
    tab = pl.BlockSpec((tm, LANES), lambda i, j: (i, 0))
    return pl.pallas_call(
        functools.partial(_q_up_kernel, heads=heads),
        grid=(t // tm, n // tn),
        in_specs=[
            pl.BlockSpec((tm, r), lambda i, j: (i, 0)),
            pl.BlockSpec((r, tn), lambda i, j: (0, j)),
            pl.BlockSpec((1, MLA_HEAD_PAD), lambda i, j: (0, 0)),
            tab, tab, tab,
        ],
        out_specs=pl.BlockSpec((tm, tn), lambda i, j: (i, j)),
        out_shape=jax.ShapeDtypeStruct((t, n), BF16),
        compiler_params=_params("arbitrary", "arbitrary"),
        name="mla_q_up",
    )(cq, w_uq_ext, g_qn_ext, rope_c, rope_s1, rope_s2)


def _kv_up_kernel(ckv_ref, kr_ref, wk_ref, wv_ref, g_ref, c_ref, s1_ref, s2_ref, k_ref, v_ref, *, heads):
    ckv = ckv_ref[...].astype(BF16)
    kn = jnp.dot(ckv, wk_ref[...], preferred_element_type=F32)
    v_ref[...] = jnp.dot(ckv, wv_ref[...], preferred_element_type=F32).astype(v_ref.dtype)
    g = g_ref[...]
    lane = lax.broadcasted_iota(I32, (1, LANES), 1)
    kr = jnp.where(lane < MLA_ROPE, kr_ref[...], 0.0)
    ss_r = jnp.sum(kr * kr, axis=-1, keepdims=True)
    krot = _rope_block(kr * g[:, MLA_NOPE:], c_ref[...], s1_ref[...], s2_ref[...])
    for hh in range(heads):
        nope = kn[:, hh * MLA_NOPE:(hh + 1) * MLA_NOPE]
        r = lax.rsqrt((jnp.sum(nope * nope, axis=-1, keepdims=True) + ss_r) / MLA_QK_DIM + NORM_EPS)
        base = hh * MLA_HEAD_PAD
        k_ref[:, base:base + MLA_NOPE] = (nope * r * g[:, :MLA_NOPE]).astype(k_ref.dtype)
        k_ref[:, base + MLA_NOPE:base + MLA_HEAD_PAD] = (krot * r).astype(k_ref.dtype)


def mla_kv_up(ckv, kr, w_uk, w_uv, g_kn_ext, rope_c, rope_s1, rope_s2):
    t, r = ckv.shape
    tm, heads = 512, 2
    tab = pl.BlockSpec((tm, LANES), lambda i, j: (i, 0))
    return pl.pallas_call(
        functools.partial(_kv_up_kernel, heads=heads),
        grid=(t // tm, MLA_HEADS // heads),
        in_specs=[
            pl.BlockSpec((tm, r), lambda i, j: (i, 0)),
            tab,
            pl.BlockSpec((r, heads * MLA_NOPE), lambda i, j: (0, j)),
            pl.BlockSpec((r, heads * MLA_NOPE), lambda i, j: (0, j)),
            pl.BlockSpec((1, MLA_HEAD_PAD), lambda i, j: (0, 0)),
            tab, tab, tab,
        ],
        out_specs=[
            pl.BlockSpec((tm, heads * MLA_HEAD_PAD), lambda i, j: (i, j)),
            pl.BlockSpec((tm, heads * MLA_NOPE), lambda i, j: (i, j)),
        ],
        out_shape=[
            jax.ShapeDtypeStruct((t, MLA_HEADS * MLA_HEAD_PAD), BF16),
            jax.ShapeDtypeStruct((t, MLA_HEADS * MLA_NOPE), BF16),
        ],
        compiler_params=_params("arbitrary", "arbitrary"),
        name="mla_kv_up",
    )(ckv, kr, w_uk, w_uv, g_kn_ext, rope_c, rope_s1, rope_s2)


ATTN_HEADS_PER_STEP = 4


def _attn_kernel(*refs, nseg):
    q_ref = refs[0]
    k_refs = refs[1:1 + nseg]
    v_refs = refs[1 + nseg:1 + 2 * nseg]
    o_ref = refs[-1]
    scale = MLA_QK_DIM ** -0.5
    nt = (((1,), (1,)), ((), ()))
    for hh in range(ATTN_HEADS_PER_STEP):
        qk_cols = slice(hh * MLA_HEAD_PAD, (hh + 1) * MLA_HEAD_PAD)
        v_cols = slice(hh * MLA_NOPE, (hh + 1) * MLA_NOPE)
        q = q_ref[:, qk_cols]
        s = [lax.dot_general(q, k[:, qk_cols], nt, preferred_element_type=F32) * scale for k in k_refs]
        m = functools.reduce(jnp.maximum, [jnp.max(x, axis=-1, keepdims=True) for x in s])
        e = [jnp.exp(x - m) for x in s]
        inv = 1.0 / functools.reduce(lambda a, b: a + b, [jnp.sum(x, axis=-1, keepdims=True) for x in e])
        o = functools.reduce(
            lambda a, b: a + b,
            [jnp.dot((x * inv).astype(BF16), v[:, v_cols], preferred_element_type=F32) for x, v in zip(e, v_refs)])
        o_ref[:, v_cols] = o.astype(o_ref.dtype)


def mla_attention(q, k, v, *, q_row0, n_batch, s_q, segs):
    tq = 256
    nq = s_q // tq
    nseg = len(segs)
    q_blk0 = q_row0 // tq
    hp = ATTN_HEADS_PER_STEP
    in_specs = [pl.BlockSpec((tq, hp * MLA_HEAD_PAD), lambda b, h, i: (q_blk0 + b * nq + i, h))]
    for row0, length in segs:
        in_specs.append(pl.BlockSpec((length, hp * MLA_HEAD_PAD), lambda b, h, i, o=row0 // length: (o + b, h)))
    for row0, length in segs:
        in_specs.append(pl.BlockSpec((length, hp * MLA_NOPE), lambda b, h, i, o=row0 // length: (o + b, h)))
    return pl.pallas_call(
        functools.partial(_attn_kernel, nseg=nseg),
        grid=(n_batch, MLA_HEADS // hp, nq),
        in_specs=in_specs,
        out_specs=pl.BlockSpec((tq, hp * MLA_NOPE), lambda b, h, i: (b * nq + i, h)),
        out_shape=jax.ShapeDtypeStruct((n_batch * s_q, MLA_HEADS * MLA_NOPE), BF16),
        compiler_params=_params("arbitrary", "arbitrary", "arbitrary"),
        name="mla_attention",
    )(q, *([k] * nseg), *([v] * nseg))


def _rope_tables():
    nf = MLA_ROPE // 4
    inv_freq = jnp.power(ROPE_BASE, -jnp.arange(nf, dtype=F32) / nf)
    tok = jnp.arange(DEC_SEQ)
    row = (tok // GRID_W).astype(F32)[:, None] * inv_freq[None, :]
    col = (tok % GRID_W).astype(F32)[:, None] * inv_freq[None, :]
    ang = jnp.concatenate([row, col], axis=-1)
    cos, sin = jnp.cos(ang), jnp.sin(ang)
    z32 = jnp.zeros_like(cos)
    z64 = jnp.zeros((DEC_SEQ, 64), F32)
    c = jnp.concatenate([cos, cos, z64], axis=-1)
    s1 = jnp.concatenate([-sin, z32, z64], axis=-1)
    s2 = jnp.concatenate([z32, sin, z64], axis=-1)
    ident_c = jnp.concatenate([jnp.ones((1, 64), F32), jnp.zeros((1, 64), F32)], axis=-1)

    def full(lat, ident):
        n_cache = DEC_BATCH * PAST_LEN
        return jnp.concatenate([
            jnp.broadcast_to(ident, (T_CTX, LANES)),
            jnp.tile(lat, (DEC_BATCH, 1)),
            jnp.broadcast_to(ident, (n_cache, LANES)),
        ], axis=0)

    zero = jnp.zeros((1, LANES), F32)
    return full(c, ident_c), full(s1, zero), full(s2, zero)


_ROPE_PERM = np.concatenate([np.arange(0, 16), np.arange(32, 48), np.arange(16, 32), np.arange(48, 64)])


def mla_layer(h, cache_ckv, cache_krope, w_in, g_q, g_kv, w_uq, g_qn, w_uk, w_uv, g_kn, w_o):
    d = h.shape[1]
    perm = _ROPE_PERM
    n_lat = MLA_Q_RANK + MLA_KV_RANK
    w_in_ext = jnp.concatenate([w_in[:, :n_lat], w_in[:, n_lat:][:, perm], w_in[:, n_lat:]], axis=1).astype(BF16)
    cq, ckv, kr = mla_in_proj(h, w_in_ext, g_q, g_kv)

    w3 = w_uq.reshape(MLA_Q_RANK, MLA_HEADS, MLA_QK_DIM)
    w_uq_ext = jnp.concatenate([
        w3[:, :, :MLA_NOPE], w3[:, :, MLA_NOPE:][:, :, perm],
        jnp.zeros((MLA_Q_RANK, MLA_HEADS, MLA_HEAD_PAD - MLA_QK_DIM), w_uq.dtype)], axis=-1)
    w_uq_ext = w_uq_ext.reshape(MLA_Q_RANK, MLA_HEADS * MLA_HEAD_PAD).astype(BF16)

    def gain_ext(g):
        return jnp.concatenate([g[:MLA_NOPE], g[MLA_NOPE:][perm],
                                jnp.zeros((MLA_HEAD_PAD - MLA_QK_DIM,), g.dtype)]).reshape(1, MLA_HEAD_PAD)

    rope_c, rope_s1, rope_s2 = _rope_tables()
    q = mla_q_up(cq, w_uq_ext, gain_ext(g_qn), rope_c[:T_ALL], rope_s1[:T_ALL], rope_s2[:T_ALL])

    n_cache = DEC_BATCH * PAST_LEN
    ckv_all = jnp.concatenate([ckv, cache_ckv.reshape(n_cache, MLA_KV_RANK)], axis=0)
    kr_cache = cache_krope.reshape(n_cache, MLA_ROPE)
    kr_cache = jnp.concatenate([kr_cache[:, perm], kr_cache], axis=1)
    kr_all = jnp.concatenate([kr, kr_cache], axis=0)
    k, v = mla_kv_up(ckv_all, kr_all, w_uk.astype(BF16), w_uv.astype(BF16), gain_ext(g_kn),
                     rope_c, rope_s1, rope_s2)

    o_ctx = mla_attention(q, k, v, q_row0=0, n_batch=BATCH, s_q=SEQ, segs=[(0, SEQ)])
    o_lat = mla_attention(q, k, v, q_row0=T_CTX, n_batch=DEC_BATCH, s_q=DEC_SEQ,
                          segs=[(T_ALL, PAST_LEN), (T_CTX, DEC_SEQ)])
    y = matmul_stacked(o_ctx, o_lat, w_o.astype(BF16), tm=1024, tn=1024, out_dtype=F32)
    new_ckv = ckv[:T_CTX].reshape(BATCH, 1, SEQ, MLA_KV_RANK)
    new_krope = kr[:T_CTX, MLA_ROPE:].reshape(BATCH, 1, SEQ, MLA_ROPE)
    return y, new_ckv, new_krope


def _log_sigmoid(x):
    return jnp.minimum(x, 0.0) - jnp.log(1.0 + jnp.exp(-jnp.abs(x)))


def _gates_kernel(h_ref, wg_ref, wgt_ref, b_ref, bt_ref, gc_ref, gr_ref):
    h = h_ref[...]
    gc = jnp.dot(h, wg_ref[...], preferred_element_type=F32) + b_ref[...]
    gr = lax.dot_general(wgt_ref[...], h, (((1,), (1,)), ((), ())), preferred_element_type=F32) + bt_ref[...]
    lane = lax.broadcasted_iota(I32, (1, LANES), 1)
    is_f = ((lane >= ML_HEADS) & (lane < 2 * ML_HEADS)) | ((lane >= 3 * ML_HEADS) & (lane < 4 * ML_HEADS))
    gc_ref[...] = jnp.where(is_f, _log_sigmoid(gc), gc)
    row = lax.broadcasted_iota(I32, (4 * ML_HEADS, 1), 0)
    is_fr = ((row >= ML_HEADS) & (row < 2 * ML_HEADS)) | ((row >= 3 * ML_HEADS) & (row < 4 * ML_HEADS))
    gr_ref[...] = jnp.where(is_fr, _log_sigmoid(gr), gr)


def mlstm_gates(h, w_g, b_gate):
    t, d = h.shape
    ng = 4 * ML_HEADS
    tm = 512
    wg = jnp.concatenate([w_g, jnp.zeros((d, LANES - ng), w_g.dtype)], axis=1).astype(BF16)
    wgt = w_g.T.astype(BF16)
    b = jnp.concatenate([b_gate, jnp.zeros((LANES - ng,), b_gate.dtype)]).reshape(1, LANES)
    bt = b_gate.reshape(ng, 1)
    return pl.pallas_call(
        _gates_kernel,
        grid=(t // tm,),
        in_specs=[
            pl.BlockSpec((tm, d), lambda i: (i, 0)),
            pl.BlockSpec((d, LANES), lambda i: (0, 0)),
            pl.BlockSpec((ng, d), lambda i: (0, 0)),
            pl.BlockSpec((1, LANES), lambda i: (0, 0)),
            pl.BlockSpec((ng, 1), lambda i: (0, 0)),
        ],
        out_specs=[pl.BlockSpec((tm, LANES), lambda i: (i, 0)), pl.BlockSpec((ng, tm), lambda i: (0, i))],
        out_shape=[jax.ShapeDtypeStruct((t, LANES), F32), jax.ShapeDtypeStruct((ng, t), F32)],
        compiler_params=_params("arbitrary"),
        name="mlstm_gates",
    )(h, wg, wgt, b, bt)


def _mlstm_kernel(*refs, nc, has_init, emit_state):
    refs = list(refs)
    q_ref, k_ref, v_ref, o_ref, gc_ref, gr_ref, gh_ref = [refs.pop(0) for _ in range(7)]
    if has_init:
        c0_ref, n0_ref, m0_ref = [refs.pop(0) for _ in range(3)]
    y_ref = refs.pop(0)
    if emit_state:
        cf_ref, nf_ref, mf_ref = [refs.pop(0) for _ in range(3)]
    mem_s, nrm_s, m_s, hs_s = refs[0:2], refs[2:4], refs[4:6], refs[6:8]

    L = ML_CHUNK
    tt = lax.broadcasted_iota(I32, (L, L), 0)
    ss = lax.broadcasted_iota(I32, (L, L), 1)
    q_scale = ML_DK ** -0.5
    nt = (((1,), (1,)), ((), ()))
    tn = (((0,), (0,)), ((), ()))

    for d in range(2):
        if has_init:
            mem_s[d][...] = c0_ref[0, d, 0]
            nrm_s[d][...] = n0_ref[0, d, 0]
            m_s[d][...] = m0_ref[0, d, 0]
        else:
            mem_s[d][...] = jnp.zeros(mem_s[d].shape, F32)
            nrm_s[d][...] = jnp.zeros(nrm_s[d].shape, F32)
            m_s[d][...] = jnp.zeros(m_s[d].shape, F32)

    def chunk(d, c):
        causal = (ss <= tt) if d == 0 else (ss >= tt)
        causal_t = (tt <= ss) if d == 0 else (tt >= ss)
        last = L - 1 if d == 0 else 0
        gcol = gc_ref[0, 0, c]
        grow = gr_ref[0, 0, c]
        i_col = gcol[:, 2 * d:2 * d + 1]
        f_col = gcol[:, 2 * d + 1:2 * d + 2]
        i_row = grow[2 * d:2 * d + 1, :]
        f_row = grow[2 * d + 1:2 * d + 2, :]
        cum_col = jnp.sum(jnp.where(causal, jnp.broadcast_to(f_row, (L, L)), 0.0), axis=1, keepdims=True)
        cum_row = jnp.sum(jnp.where(causal_t, jnp.broadcast_to(f_col, (L, L)), 0.0), axis=0, keepdims=True)
        total = cum_col[last:last + 1, :]
        m_prev = m_s[d][:, 0:1]
        dmat = jnp.where(causal, cum_col - cum_row + i_row, NEG_INF)
        inter = cum_col + m_prev
        m_t = jnp.maximum(inter, jnp.max(dmat, axis=1, keepdims=True))
        w_inter = jnp.exp(inter - m_t)
        rows = pl.ds(pl.multiple_of(c * L, L), L)
        qf = q_ref[rows, :] * q_scale
        kf = k_ref[rows, :]
        vb = v_ref[rows, :].astype(BF16)
        qb = qf.astype(BF16)
        qk = lax.dot_general(qb, kf.astype(BF16), nt, preferred_element_type=F32)
        a = jnp.exp(dmat - m_t) * qk
        mem = mem_s[d][...]
        nrm = nrm_s[d][...]
        num = (w_inter * jnp.dot(qb, mem.astype(BF16), preferred_element_type=F32)
               + jnp.dot(a.astype(BF16), vb, preferred_element_type=F32))
        den = w_inter * jnp.sum(qf * nrm, axis=1, keepdims=True) + jnp.sum(a, axis=1, keepdims=True)
        hs_s[d][rows, :] = num / jnp.maximum(jnp.abs(den), jnp.exp(-m_t))
        m_new = m_t[last:last + 1, :]
        decay = jnp.exp(total + m_prev - m_new)
        w_s = jnp.exp(total - cum_col + i_col - m_new)
        wk = w_s * kf
        mem_s[d][...] = decay * mem + lax.dot_general(wk.astype(BF16), vb, tn, preferred_element_type=F32)
        nrm_s[d][...] = decay * nrm + jnp.sum(wk, axis=0, keepdims=True)
        m_s[d][...] = jnp.broadcast_to(m_new, m_s[d].shape)

    def both(ci, carry):
        chunk(0, ci)
        chunk(1, nc - 1 - ci)
        return carry

    lax.fori_loop(0, nc, both, 0)
    if emit_state:
        for d in range(2):
            cf_ref[0, d, 0] = mem_s[d][...]
            nf_ref[0, d, 0] = nrm_s[d][...]
            mf_ref[0, d, 0] = m_s[d][...]

    hs = hs_s[0][...] + hs_s[1][...]
    hn = hs * lax.rsqrt(jnp.mean(hs * hs, axis=-1, keepdims=True) + NORM_EPS) * gh_ref[...]
    y_ref[...] = (hn * jax.nn.sigmoid(o_ref[...])).astype(y_ref.dtype)


def mlstm_scan(p, gcol, grow, g_h, *, row0, n_batch, seq, state=None, emit_state=False):
    nc = seq // ML_CHUNK
    rb0 = row0 // seq
    hk = ML_HEADS * ML_DK
    has_init = state is not None
    in_specs = [
        pl.BlockSpec((seq, ML_DK), lambda b, h: (rb0 + b, h)),
        pl.BlockSpec((seq, ML_DK), lambda b, h: (rb0 + b, ML_HEADS + h)),
        pl.BlockSpec((seq, ML_DV), lambda b, h: (rb0 + b, 2 * hk // ML_DV + h)),
        pl.BlockSpec((seq, ML_DV), lambda b, h: (rb0 + b, 2 * hk // ML_DV + ML_HEADS + h)),
        pl.BlockSpec((1, 1, nc, ML_CHUNK, 4), lambda b, h: (b, h, 0, 0, 0)),
        pl.BlockSpec((1, 1, nc, 4, ML_CHUNK), lambda b, h: (b, h, 0, 0, 0)),
        pl.BlockSpec((1, ML_DV), lambda b, h: (0, h)),
    ]
    args = [p, p, p, p, gcol, grow, g_h.reshape(1, -1)]
    c_spec = pl.BlockSpec((1, 2, 1, ML_DK, ML_DV), lambda b, h: (b, 0, h, 0, 0))
    n_spec = pl.BlockSpec((1, 2, 1, 1, ML_DK), lambda b, h: (b, 0, h, 0, 0))
    m_spec = pl.BlockSpec((1, 2, 1, 1, LANES), lambda b, h: (b, 0, h, 0, 0))
    if has_init:
        in_specs += [c_spec, n_spec, m_spec]
        args += list(state)
    out_specs = [pl.BlockSpec((seq, ML_DV), lambda b, h: (b, h))]
    out_shape = [jax.ShapeDtypeStruct((n_batch * seq, ML_HEADS * ML_DV), BF16)]
    if emit_state:
        out_specs += [c_spec, n_spec, m_spec]
        out_shape += [
            jax.ShapeDtypeStruct((n_batch, 2, ML_HEADS, ML_DK, ML_DV), F32),
            jax.ShapeDtypeStruct((n_batch, 2, ML_HEADS, 1, ML_DK), F32),
            jax.ShapeDtypeStruct((n_batch, 2, ML_HEADS, 1, LANES), F32),
        ]
    return pl.pallas_call(
        functools.partial(_mlstm_kernel, nc=nc, has_init=has_init, emit_state=emit_state),
        grid=(n_batch, ML_HEADS),
        in_specs=in_specs,
        out_specs=out_specs,
        out_shape=out_shape,
        scratch_shapes=[
            pltpu.VMEM((ML_DK, ML_DV), F32), pltpu.VMEM((ML_DK, ML_DV), F32),
            pltpu.VMEM((1, ML_DK), F32), pltpu.VMEM((1, ML_DK), F32),
            pltpu.VMEM((1, LANES), F32), pltpu.VMEM((1, LANES), F32),
            pltpu.VMEM((seq, ML_DV), F32), pltpu.VMEM((seq, ML_DV), F32),
        ],
        compiler_params=_params("arbitrary", "arbitrary"),
        name="mlstm_scan",
    )(*args)


def _gate_layouts(gc, gr, row0, n_batch, seq):
    nc = seq // ML_CHUNK
    n = n_batch * seq
    gcol = gc[row0:row0 + n, :4 * ML_HEADS].reshape(n_batch, nc, ML_CHUNK, 4, ML_HEADS).transpose(0, 4, 1, 2, 3)
    grow = gr[:, row0:row0 + n].reshape(4, ML_HEADS, n_batch, nc, ML_CHUNK).transpose(2, 1, 3, 0, 4)
    return gcol, grow


def mlstm_layer(h, state_c, state_n, state_m, w_in, b_gate, g_h, w_o):
    hk = ML_HEADS * ML_DK
    hv = ML_HEADS * ML_DV
    n_main = 2 * hk + 2 * hv
    p = matmul(h, w_in.astype(BF16), tm=1024, tn=1024 + 512, out_dtype=F32, n_cols=n_main)
    gc, gr = mlstm_gates(h, w_in[:, n_main:], b_gate)

    gcol, grow = _gate_layouts(gc, gr, 0, BATCH, SEQ)
    y_ctx, cf, nf, mf = mlstm_scan(p, gcol, grow, g_h, row0=0, n_batch=BATCH, seq=SEQ, emit_state=True)

    gcol, grow = _gate_layouts(gc, gr, T_CTX, DEC_BATCH, DEC_SEQ)
    c0 = state_c[:, 0]
    n0 = state_n[:, 0].reshape(DEC_BATCH, 2, ML_HEADS, 1, ML_DK)
    m0 = jnp.broadcast_to(state_m[:, 0].reshape(DEC_BATCH, 2, ML_HEADS, 1, 1), (DEC_BATCH, 2, ML_HEADS, 1, LANES))
    (y_lat,) = mlstm_scan(p, gcol, grow, g_h, row0=T_CTX, n_batch=DEC_BATCH, seq=DEC_SEQ, state=(c0, n0, m0))

    y = matmul_stacked(y_ctx, y_lat, w_o.astype(BF16), tm=1024, tn=1024, out_dtype=F32)
    new_c = cf.reshape(BATCH, 1, 2, ML_HEADS, ML_DK, ML_DV)
    new_n = nf.reshape(BATCH, 1, 2, ML_HEADS, ML_DK)
    new_m = mf[..., 0, 0].reshape(BATCH, 1, 2, ML_HEADS)
    return y, new_c, new_n, new_m


def _topk_rows(x, payload, n_out):
    rows = x.shape[0]
    iota = lax.broadcasted_iota(I32, x.shape, 0)
    vals, outs = [], []
    for _ in range(n_out):
        m = jnp.max(x, axis=0, keepdims=True)
        pos = jnp.min(jnp.where(x == m, iota, rows), axis=0, keepdims=True)
        sel = iota == pos
        vals.append(m)
        outs.append(pos if payload is None else jnp.max(jnp.where(sel, payload, -1), axis=0, keepdims=True))
        x = jnp.where(sel, NEG_INF, x)
    return jnp.concatenate(vals, axis=0), jnp.concatenate(outs, axis=0)


def _pair_candidates(a, b, combine, fill):
    k = PEER_TOPK
    row = lax.broadcasted_iota(I32, (SUBLANES, a.shape[1]), 0)
    blocks = [combine(a[0:1, :], b), combine(a[1:2, :], b[0:SUBLANES, :])]
    for i in range(2, SUBLANES):
        blocks.append(jnp.where(row < k // (i + 1), combine(a[i:i + 1, :], b[0:SUBLANES, :]), fill))
    blocks.append(combine(a[SUBLANES:k, :], b[0:1, :]))
    return jnp.concatenate(blocks, axis=0)


PEER_TILE = 128


def _peer_head_topk(qh, keys_ref):
    half = PEER_QDIM // 2
    nt = (((1,), (1,)), ((), ()))
    sv, si = [], []
    for p in range(2):
        qs = qh[:, p * half:(p + 1) * half].astype(BF16)
        sc = lax.dot_general(keys_ref[p], qs, nt, preferred_element_type=F32)
        v, i = _topk_rows(sc, None, PEER_TOPK)
        sv.append(v)
        si.append(i)
    cand = _pair_candidates(sv[0], sv[1], lambda x, y: x + y, NEG_INF)
    cidx = _pair_candidates(si[0], si[1], lambda x, y: x * PEER_NKEYS + y, -1)
    best, eidx = _topk_rows(cand, cidx, PEER_TOPK)
    ex = jnp.exp(best - best[0:1, :])
    return eidx, ex / jnp.sum(ex, axis=0, keepdims=True)


def _peer_topk_kernel(q_ref, keys_ref, e_ref, gw_ref):
    parts = [_peer_head_topk(q_ref[hd], keys_ref) for hd in range(PEER_HEADS)]
    e_ref[...] = jnp.concatenate([p[0] for p in parts], axis=0).T
    gw_ref[...] = jnp.concatenate([p[1] for p in parts], axis=0).T


def peer_topk(q3, keys, n_tokens):
    tt = PEER_TILE
    return pl.pallas_call(
        _peer_topk_kernel,
        grid=(n_tokens // tt,),
        in_specs=[pl.BlockSpec((PEER_HEADS, tt, PEER_QDIM), lambda i: (0, i, 0)),
                  pl.BlockSpec(keys.shape, lambda i: (0, 0, 0))],
        out_specs=[pl.BlockSpec((tt, PEER_SEL), lambda i: (i, 0)), pl.BlockSpec((tt, PEER_SEL), lambda i: (i, 0))],
        out_shape=[jax.ShapeDtypeStruct((n_tokens, PEER_SEL), I32), jax.ShapeDtypeStruct((n_tokens, PEER_SEL), F32)],
        compiler_params=_params("arbitrary"),
        name="peer_topk",
    )(q3, keys)


def _peer_query_kernel(x_ref, w_ref, o_ref):
    acc = jnp.dot(x_ref[...], w_ref[...], preferred_element_type=F32)
    for hd in range(PEER_HEADS):
        o_ref[hd] = acc[:, hd * PEER_QDIM:(hd + 1) * PEER_QDIM]


def peer_query(h, w_q):
    t, d = h.shape
    tm = min(1024, t)
    return pl.pallas_call(
        _peer_query_kernel,
        grid=(t // tm,),
        in_specs=[pl.BlockSpec((tm, d), lambda i: (i, 0)), pl.BlockSpec(w_q.shape, lambda i: (0, 0))],
        out_specs=pl.BlockSpec((PEER_HEADS, tm, PEER_QDIM), lambda i: (0, i, 0)),
        out_shape=jax.ShapeDtypeStruct((PEER_HEADS, t, PEER_QDIM), F32),
        compiler_params=_params("arbitrary"),
        name="peer_query",
    )(h, w_q)


PEER_STEP_TOK = 16
PEER_SLABS = D_MODEL // LANES
PEER_ROWS = PEER_SEL * PEER_SLABS
_ERF_GELU_C = 0.7071067811865476


PEER_STEPS_PER_TILE = PEER_TILE // PEER_STEP_TOK
PEER_PRE_TILES = 2


def _peer_apply_kernel(e0_ref, gw0_ref, q_ref, keys_ref, h_ref, expand_ref, expand_t_ref, uv_ref, o_ref,
                       buf_a, buf_b, zs, wexp, e_smem, gw_tiles, stage_e, stage_g, e_vm, sem, sem_e, *, layer):
    i = pl.program_id(0)
    n_steps = pl.num_programs(0)
    ns = PEER_SLABS
    tt = PEER_STEP_TOK
    spt = PEER_STEPS_PER_TILE
    n_tiles = n_steps // spt
    nt = (((1,), (1,)), ((), ()))

    def gather_copy(idx, buf, n, s):
        return pltpu.make_async_copy(uv_ref.at[layer, idx], buf.at[n], sem.at[s])

    def id_row0(step):
        return ((step // spt) % 2) * PEER_TILE + (step % spt) * tt

    def issue_token(row0, tok, k_lo, k_hi, buf, s):
        for k in range(k_lo, k_hi):
            gather_copy(e_smem[row0 + tok, k], buf, tok * PEER_SEL + k, s).start(priority=k % 2)

    def wait(buf, s):
        pltpu.make_async_copy(buf, buf, sem.at[s]).wait()

    def compute_and_issue(buf, row0, nxt, s_nxt):
        half = PEER_SEL // 2
        sub = lax.broadcasted_iota(I32, (ns, PEER_ROWS), 0)
        col = lax.broadcasted_iota(I32, (ns, PEER_ROWS), 1)
        diag = (col % ns) == sub
        for t in range(tt):
            issue_token(row0, t, 0, half, nxt, s_nxt)
            u_t = buf[pl.ds(t * PEER_SEL, PEER_SEL), 0].reshape(PEER_ROWS, LANES)
            y = lax.dot_general(h_ref[t], u_t, nt, preferred_element_type=F32)
            zs[pl.ds(t, 1), :] = jnp.sum(jnp.where(diag, y, 0.0), axis=0, keepdims=True)
        z = zs[...]
        z_hi = z.astype(BF16)
        z_lo = (z - z_hi.astype(F32)).astype(BF16)
        act = (jnp.dot(z_hi, expand_t_ref[...], preferred_element_type=F32)
               + jnp.dot(z_lo, expand_t_ref[...], preferred_element_type=F32))
        gelu = 0.5 * act * (1.0 + lax.erf(act * _ERF_GELU_C))
        gw = gw_tiles[(i // spt) % 2, pl.ds(pl.multiple_of((i % spt) * tt, tt), tt), :]
        w = (gw * gelu).astype(BF16)
        wexp[...] = jnp.dot(w, expand_ref[...], preferred_element_type=F32)
        for t in range(tt):
            issue_token(row0, t, half, PEER_SEL, nxt, s_nxt)
            wrow = wexp[pl.ds(t, 1), :]
            wbig = jnp.where(diag, jnp.broadcast_to(wrow, (ns, PEER_ROWS)), 0.0).astype(BF16)
            v_t = buf[pl.ds(t * PEER_SEL, PEER_SEL), 1].reshape(PEER_ROWS, LANES)
            o_ref[t] = jnp.dot(wbig, v_t, preferred_element_type=F32)

    def step(cur, s_cur, nxt, s_nxt):
        wait(cur, s_cur)
        compute_and_issue(cur, id_row0(jnp.minimum(i + 1, n_steps - 1)), nxt, s_nxt)

        @pl.when(i == n_steps - 1)
        def _():
            wait(nxt, s_nxt)

    def load_ids(src, row0):
        cp = pltpu.make_async_copy(src, e_smem.at[pl.ds(row0, src.shape[0])], sem_e.at[0])
        cp.start()
        cp.wait()

    @pl.when(i == 0)
    def _():
        load_ids(e0_ref, 0)
        gw_tiles[...] = gw0_ref[...].reshape(gw_tiles.shape)
        for tok in range(tt):
            issue_token(0, tok, 0, PEER_SEL, buf_a, 0)

    @pl.when(i % 2 == 0)
    def _():
        step(buf_a, 0, buf_b, 1)

    @pl.when(i % 2 == 1)
    def _():
        step(buf_b, 1, buf_a, 0)

    ahead = i + spt + 1
    tile_t = ahead // spt
    head = ahead % spt

    @pl.when((tile_t >= PEER_PRE_TILES) & (tile_t < n_tiles))
    def _():
        eidx, g = _peer_head_topk(q_ref[0], keys_ref)
        rows = pl.ds(pl.multiple_of(head * PEER_TOPK, PEER_TOPK), PEER_TOPK)
        stage_e[rows, :] = eidx
        stage_g[rows, :] = g

        @pl.when(head == spt - 1)
        def _():
            slot = tile_t % 2
            gw_tiles[slot] = stage_g[...].T
            e_vm[...] = stage_e[...].T
            load_ids(e_vm, pl.multiple_of(slot * PEER_TILE, PEER_TILE))


def peer_apply(e0, gw0, q3, keys, h, uv, layer):
    t, d = h.shape
    ns = PEER_SLABS
    tt = PEER_STEP_TOK
    spt = PEER_STEPS_PER_TILE
    assert spt == PEER_HEADS and e0.shape[0] == PEER_PRE_TILES * PEER_TILE
    n_steps = t // tt
    n_tiles = n_steps // spt
    h3 = h.reshape(t, ns, LANES)
    group = np.arange(PEER_ROWS) // ns
    expand = jnp.asarray(group[None, :] == np.arange(PEER_SEL)[:, None], BF16)
    out = pl.pallas_call(
        functools.partial(_peer_apply_kernel, layer=layer),
        grid=(n_steps,),
        in_specs=[
            pl.BlockSpec(memory_space=pl.ANY),
            pl.BlockSpec(gw0.shape, lambda i: (0, 0)),
            pl.BlockSpec((1, PEER_TILE, PEER_QDIM),
                         lambda i: ((i + spt + 1) % spt, jnp.minimum((i + spt + 1) // spt, n_tiles - 1), 0)),
            pl.BlockSpec(keys.shape, lambda i: (0, 0, 0)),
            pl.BlockSpec((tt, ns, LANES), lambda i: (i, 0, 0)),
            pl.BlockSpec((PEER_SEL, PEER_ROWS), lambda i: (0, 0)),
            pl.BlockSpec((PEER_ROWS, PEER_SEL), lambda i: (0, 0)),
            pl.BlockSpec(memory_space=pl.ANY),
        ],
        out_specs=pl.BlockSpec((tt, ns, LANES), lambda i: (i, 0, 0)),
        out_shape=jax.ShapeDtypeStruct((t, ns, LANES), F32),
        scratch_shapes=[
            pltpu.VMEM((tt * PEER_SEL, 2, ns, LANES), BF16),
            pltpu.VMEM((tt * PEER_SEL, 2, ns, LANES), BF16),
            pltpu.VMEM((tt, PEER_ROWS), F32),
            pltpu.VMEM((tt, PEER_ROWS), F32),
            pltpu.SMEM((2 * PEER_TILE, PEER_SEL), I32),
            pltpu.VMEM((2, PEER_TILE, PEER_SEL), F32),
            pltpu.VMEM((PEER_SEL, PEER_TILE), I32),
            pltpu.VMEM((PEER_SEL, PEER_TILE), F32),
            pltpu.VMEM((PEER_TILE, PEER_SEL), I32),
            pltpu.SemaphoreType.DMA((2,)),
            pltpu.SemaphoreType.DMA((1,)),
        ],
        compiler_params=pltpu.CompilerParams(dimension_semantics=("arbitrary",), vmem_limit_bytes=VMEM_LIMIT_BYTES,
                                             disable_bounds_checks=True),
        name="peer_apply",
    )(e0, gw0, q3, keys, h3, expand, expand.T, uv)
    return out.reshape(t, d)


def peer_tables(peer_u, peer_v):
    depth, n_exp, _ = peer_u.shape
    shape = (depth, n_exp, PEER_SLABS, LANES)
    return jnp.stack([peer_u.astype(BF16).reshape(shape), peer_v.astype(BF16).reshape(shape)], axis=2)


def peer_layer(h, w_q, keys, uv, layer):
    q3 = peer_query(h, w_q.astype(BF16))
    keys = keys.astype(BF16)
    e0, gw0 = peer_topk(q3, keys, PEER_PRE_TILES * PEER_TILE)
    return peer_apply(e0, gw0, q3, keys, h, uv, layer)


def kernel(x_prompt, x_sample, cache_ckv, cache_krope, state_C, state_n, state_m, c, c_ctx, mod_w, mod_b, norm_mix, norm_ffn, mla_w_in, mla_g_q, mla_g_kv, mla_w_uq, mla_g_qn, mla_w_uk, mla_w_uv, mla_g_kn, mla_w_o, ml_w_in, ml_b_gate, ml_g_h, ml_w_o, peer_w_q, peer_keys, peer_u, peer_v):
    d = D_MODEL
    x = jnp.concatenate([x_prompt.reshape(T_CTX, d), x_sample.reshape(T_LAT, d)], axis=0)
    cvec = jnp.concatenate([c_ctx.reshape(1, d), c, jnp.zeros((GROUP_PAD - N_GROUPS, d), c.dtype)], axis=0)
    mod = mod_vectors(cvec, mod_w, mod_b)
    modrows = mod.reshape(mod.shape[0] * GROUP_PAD * N_MOD, 1, d)
    uv = peer_tables(peer_u, peer_v)

    (h,) = resid_modulate(x, modrows, norm_g=norm_mix[0], shift=(0, 0), scale=(0, 1))
    y, new_ckv, new_krope = mla_layer(h, cache_ckv[:, 0], cache_krope[:, 0], mla_w_in[0], mla_g_q[0], mla_g_kv[0],
                                      mla_w_uq[0], mla_g_qn[0], mla_w_uk[0], mla_w_uv[0], mla_g_kn[0], mla_w_o[0])
    x, h = resid_modulate(x, modrows, y=y, gate=(0, 2), norm_g=norm_ffn[0], shift=(0, 3), scale=(0, 4))
    y = peer_layer(h, peer_w_q[0], peer_keys[0], uv, 0)

    x, h = resid_modulate(x, modrows, y=y, gate=(0, 5), norm_g=norm_mix[1], shift=(1, 0), scale=(1, 1))
    y, new_c, new_n, new_m = mlstm_layer(h, state_C, state_n, state_m, ml_w_in[0], ml_b_gate[0], ml_g_h[0], ml_w_o[0])
    x, h = resid_modulate(x, modrows, y=y, gate=(1, 2), norm_g=norm_ffn[1], shift=(1, 3), scale=(1, 4))
    y = peer_layer(h, peer_w_q[1], peer_keys[1], uv, 1)
    (x,) = resid_modulate(x, modrows, y=y, gate=(1, 5))

    y_prompt = x[:T_CTX].reshape(BATCH, SEQ, d)
    y_sample = x[T_CTX:].reshape(DEC_BATCH, DEC_SEQ, d)
    return (y_prompt, y_sample, new_ckv, new_krope, new_c, new_n, new_m)
```

```python
import functools

import numpy as np
import jax
import jax.numpy as jnp
from jax import lax
from jax.experimental import pallas as pl
from jax.experimental.pallas import tpu as pltpu

F32 = jnp.float32
BF16 = jnp.bfloat16
I32 = jnp.int32

D_MODEL = 2048
BATCH, SEQ = 32, 256
DEC_BATCH, DEC_SEQ = 8, 1024
PAST_LEN = 512
GRID_W = 64
N_MOD = 6
NORM_EPS = 1e-6
MLA_HEADS = 16
MLA_Q_RANK = 512
MLA_KV_RANK = 512
MLA_NOPE = 128
MLA_ROPE = 64
MLA_QK_DIM = MLA_NOPE + MLA_ROPE
MLA_HEAD_PAD = 256
ROPE_BASE = 10000.0
ML_HEADS = 8
ML_DV = D_MODEL // ML_HEADS
ML_DK = ML_DV // 2
ML_CHUNK = 64
PEER_HEADS = 8
PEER_NKEYS = 128
PEER_QDIM = 128
PEER_TOPK = 16
PEER_SEL = PEER_HEADS * PEER_TOPK

T_CTX = BATCH * SEQ
T_LAT = DEC_BATCH * DEC_SEQ
T_ALL = T_CTX + T_LAT
N_GROUPS = 1 + DEC_BATCH
GROUP_PAD = 16
ROW_TILE = 256

VMEM_LIMIT_BYTES = 56 * 1024 * 1024
LANES = 128
SUBLANES = 8

NEG_INF = float("-inf")


def _params(*sem):
    return pltpu.CompilerParams(dimension_semantics=sem, vmem_limit_bytes=VMEM_LIMIT_BYTES)


def _mod_kernel(c_ref, w_ref, b_ref, o_ref):
    c = c_ref[...]
    a = (c * jax.nn.sigmoid(c)).astype(BF16)
    o_ref[0] = jnp.dot(a, w_ref[0].astype(BF16), preferred_element_type=F32) + b_ref[0]


def mod_vectors(cvec, mod_w, mod_b):
    depth, d, n = mod_w.shape
    tn = 1024
    return pl.pallas_call(
        _mod_kernel,
        grid=(depth, n // tn),
        in_specs=[
            pl.BlockSpec((GROUP_PAD, d), lambda l, j: (0, 0)),
            pl.BlockSpec((1, d, tn), lambda l, j: (l, 0, j)),
            pl.BlockSpec((1, 1, tn), lambda l, j: (l, 0, j)),
        ],
        out_specs=pl.BlockSpec((1, GROUP_PAD, tn), lambda l, j: (l, 0, j)),
        out_shape=jax.ShapeDtypeStruct((depth, GROUP_PAD, n), F32),
        compiler_params=_params("arbitrary", "arbitrary"),
        name="mod_vectors",
    )(cvec, mod_w, mod_b.reshape(depth, 1, n))


def _group_of_tile(i):
    ctx_tiles = T_CTX // ROW_TILE
    tiles_per_lat = DEC_SEQ // ROW_TILE
    return jnp.where(i < ctx_tiles, 0, 1 + (i - ctx_tiles) // tiles_per_lat)


CTX_TILES = T_CTX // ROW_TILE


def _row_tile(parts, i):
    if len(parts) == 1:
        return parts[0][...]
    return jnp.where(i < CTX_TILES, parts[0][...], parts[1][...])


def _resmod_kernel(*refs, nx, ny, has_w, has_mod, split_out):
    i = pl.program_id(0)
    refs = list(refs)
    x = _row_tile([refs.pop(0) for _ in range(nx)], i)
    if ny:
        y = _row_tile([refs.pop(0) for _ in range(ny)], i)
        if has_w:
            y = jnp.dot(y, refs.pop(0)[...], preferred_element_type=F32)
        gate_ref = refs.pop(0)
        x = x + gate_ref[0] * y
    if has_mod:
        g_ref, sh_ref, sc_ref = refs.pop(0), refs.pop(0), refs.pop(0)
    if ny:
        if split_out:
            xa_ref, xb_ref = refs.pop(0), refs.pop(0)

            @pl.when(i < CTX_TILES)
            def _():
                xa_ref[...] = x

            @pl.when(i >= CTX_TILES)
            def _():
                xb_ref[...] = x
        else:
            refs.pop(0)[...] = x
    if has_mod:
        h_ref = refs.pop(0)
        ms = jnp.mean(x * x, axis=-1, keepdims=True)
        yn = x * lax.rsqrt(ms + NORM_EPS) * g_ref[...]
        h_ref[...] = (yn * (1.0 + sc_ref[0]) + sh_ref[0]).astype(h_ref.dtype)


def resid_modulate(x, modrows, *, y=None, w=None, gate=None, norm_g=None, shift=None, scale=None, split_out=False):
    xs = list(x) if isinstance(x, (tuple, list)) else [x]
    ys = [] if y is None else (list(y) if isinstance(y, (tuple, list)) else [y])
    d = xs[0].shape[1]
    t = sum(a.shape[0] for a in xs)
    has_mod = norm_g is not None
    row_spec = pl.BlockSpec((ROW_TILE, d), lambda i: (i, 0))
    ctx_spec = pl.BlockSpec((ROW_TILE, d), lambda i: (jnp.minimum(i, CTX_TILES - 1), 0))
    lat_spec = pl.BlockSpec((ROW_TILE, d), lambda i: (jnp.maximum(i - CTX_TILES, 0), 0))

    def part_specs(parts):
        return [row_spec] if len(parts) == 1 else [ctx_spec, lat_spec]

    def mod_spec(layer_k):
        layer, k = layer_k
        return pl.BlockSpec((1, 1, d), lambda i: ((layer * GROUP_PAD + _group_of_tile(i)) * N_MOD + k, 0, 0))

    args, in_specs = list(xs), part_specs(xs)
    if ys:
        args += ys
        in_specs += part_specs(ys)
        if w is not None:
            args.append(w)
            in_specs.append(pl.BlockSpec(w.shape, lambda i: (0, 0)))
        args.append(modrows)
        in_specs.append(mod_spec(gate))
    if has_mod:
        args += [norm_g.reshape(1, d), modrows, modrows]
        in_specs += [pl.BlockSpec((1, d), lambda i: (0, 0)), mod_spec(shift), mod_spec(scale)]
    out_shape, out_specs = [], []
    if ys:
        if split_out:
            out_shape += [jax.ShapeDtypeStruct((T_CTX, d), F32), jax.ShapeDtypeStruct((t - T_CTX, d), F32)]
            out_specs += [ctx_spec, lat_spec]
        else:
            out_shape.append(jax.ShapeDtypeStruct((t, d), F32))
            out_specs.append(row_spec)
    if has_mod:
        out_shape.append(jax.ShapeDtypeStruct((t, d), BF16))
        out_specs.append(row_spec)
    outs = pl.pallas_call(
        functools.partial(_resmod_kernel, nx=len(xs), ny=len(ys), has_w=w is not None, has_mod=has_mod,
                          split_out=split_out),
        grid=(t // ROW_TILE,),
        in_specs=in_specs,
        out_specs=out_specs,
        out_shape=out_shape,
        compiler_params=_params("arbitrary"),
        name="resid_modulate",
    )(*args)
    return outs


def _mm_kernel(x_ref, w_ref, o_ref):
    o_ref[...] = jnp.dot(x_ref[...].astype(BF16), w_ref[...], preferred_element_type=F32).astype(o_ref.dtype)


def matmul(x, w, *, tm, tn, out_dtype, n_cols=None):
    m, k = x.shape
    n = w.shape[1] if n_cols is None else n_cols
    return pl.pallas_call(
        _mm_kernel,
        grid=(m // tm, n // tn),
        in_specs=[pl.BlockSpec((tm, k), lambda i, j: (i, 0)), pl.BlockSpec((k, tn), lambda i, j: (0, j))],
        out_specs=pl.BlockSpec((tm, tn), lambda i, j: (i, j)),
        out_shape=jax.ShapeDtypeStruct((m, n), out_dtype),
        compiler_params=_params("arbitrary", "arbitrary"),
        name="matmul",
    )(x, w)


def _rms(x, n):
    return lax.rsqrt(jnp.sum(x * x, axis=-1, keepdims=True) / n + NORM_EPS)


def _mla_in_kernel(h_ref, w_ref, gq_ref, gkv_ref, cq_ref, ckv_ref, kr_ref):
    a = jnp.dot(h_ref[...], w_ref[...], preferred_element_type=F32)
    cq = a[:, :MLA_Q_RANK]
    ckv = a[:, MLA_Q_RANK:MLA_Q_RANK + MLA_KV_RANK]
    cq_ref[...] = (cq * _rms(cq, MLA_Q_RANK) * gq_ref[...]).astype(cq_ref.dtype)
    ckv_ref[...] = ckv * _rms(ckv, MLA_KV_RANK) * gkv_ref[...]
    kr_ref[...] = a[:, MLA_Q_RANK + MLA_KV_RANK:]


def mla_in_proj(h, w_ext, g_q, g_kv):
    t, d = h.shape
    n = w_ext.shape[1]
    tm = 512
    return pl.pallas_call(
        _mla_in_kernel,
        grid=(t // tm,),
        in_specs=[
            pl.BlockSpec((tm, d), lambda i: (i, 0)),
            pl.BlockSpec((d, n), lambda i: (0, 0)),
            pl.BlockSpec((1, MLA_Q_RANK), lambda i: (0, 0)),
            pl.BlockSpec((1, MLA_KV_RANK), lambda i: (0, 0)),
        ],
        out_specs=[
            pl.BlockSpec((tm, MLA_Q_RANK), lambda i: (i, 0)),
            pl.BlockSpec((tm, MLA_KV_RANK), lambda i: (i, 0)),
            pl.BlockSpec((tm, LANES), lambda i: (i, 0)),
        ],
        out_shape=[
            jax.ShapeDtypeStruct((t, MLA_Q_RANK), BF16),
            jax.ShapeDtypeStruct((t, MLA_KV_RANK), F32),
            jax.ShapeDtypeStruct((t, LANES), F32),
        ],
        compiler_params=_params("arbitrary"),
        name="mla_in_proj",
    )(h, w_ext, g_q.reshape(1, -1), g_kv.reshape(1, -1))


def _rope_block(x, c, s1, s2):
    return x * c + pltpu.roll(x, 96, 1) * s1 + pltpu.roll(x, 32, 1) * s2


def _q_up_kernel(cq_ref, w_ref, g_ref, c_ref, s1_ref, s2_ref, q_ref, *, heads):
    a = jnp.dot(cq_ref[...], w_ref[...], preferred_element_type=F32)
    g = g_ref[...]
    for hh in range(heads):
        base = hh * MLA_HEAD_PAD
        nope = a[:, base:base + MLA_NOPE]
        rp = a[:, base + MLA_NOPE:base + MLA_HEAD_PAD]
        ss = jnp.sum(nope * nope, axis=-1, keepdims=True) + jnp.sum(rp * rp, axis=-1, keepdims=True)
        r = lax.rsqrt(ss / MLA_QK_DIM + NORM_EPS)
        xr = _rope_block(rp * r * g[:, MLA_NOPE:], c_ref[...], s1_ref[...], s2_ref[...])
        q_ref[:, base:base + MLA_NOPE] = (nope * r * g[:, :MLA_NOPE]).astype(q_ref.dtype)
        q_ref[:, base + MLA_NOPE:base + MLA_HEAD_PAD] = xr.astype(q_ref.dtype)


def mla_q_up(cq, w_uq_ext, g_qn_ext, rope_c, rope_s1, rope_s2):
    t, r = cq.shape
    n = w_uq_ext.shape[1]
    tm, heads = 512, 2
    tn = heads * MLA_HEAD_PAD
    tab = pl.BlockSpec((tm, LANES), lambda i, j: (i, 0))
    return pl.pallas_call(
        functools.partial(_q_up_kernel, heads=heads),
        grid=(t // tm, n // tn),
        in_specs=[
            pl.BlockSpec((tm, r), lambda i, j: (i, 0)),
            pl.BlockSpec((r, tn), lambda i, j: (0, j)),
            pl.BlockSpec((1, MLA_HEAD_PAD), lambda i, j: (0, 0)),
            tab, tab, tab,
        ],
        out_specs=pl.BlockSpec((tm, tn), lambda i, j: (i, j)),
        out_shape=jax.ShapeDtypeStruct((t, n), BF16),
        compiler_params=_params("arbitrary", "arbitrary"),
        name="mla_q_up",
    )(cq, w_uq_ext, g_qn_ext, rope_c, rope_s1, rope_s2)


def _kv_up_kernel(ckv_ref, kr_ref, wk_ref, wv_ref, g_ref, c_ref, s1_ref, s2_ref, k_ref, v_ref, *, heads):
    ckv = ckv_ref[...].astype(BF16)
    kn = jnp.dot(ckv, wk_ref[...], preferred_element_type=F32)
    v_ref[...] = jnp.dot(ckv, wv_ref[...], preferred_element_type=F32).astype(v_ref.dtype)
    g = g_ref[...]
    lane = lax.broadcasted_iota(I32, (1, LANES), 1)
    kr = jnp.where(lane < MLA_ROPE, kr_ref[...], 0.0)
    ss_r = jnp.sum(kr * kr, axis=-1, keepdims=True)
    krot = _rope_block(kr * g[:, MLA_NOPE:], c_ref[...], s1_ref[...], s2_ref[...])
    for hh in range(heads):
        nope = kn[:, hh * MLA_NOPE:(hh + 1) * MLA_NOPE]
        r = lax.rsqrt((jnp.sum(nope * nope, axis=-1, keepdims=True) + ss_r) / MLA_QK_DIM + NORM_EPS)
        base = hh * MLA_HEAD_PAD
        k_ref[:, base:base + MLA_NOPE] = (nope * r * g[:, :MLA_NOPE]).astype(k_ref.dtype)
        k_ref[:, base + MLA_NOPE:base + MLA_HEAD_PAD] = (krot * r).astype(k_ref.dtype)


def mla_kv_up(ckv, kr, w_uk, w_uv, g_kn_ext, rope_c, rope_s1, rope_s2):
    t, r = ckv.shape
    tm, heads = 512, 2
    tab = pl.BlockSpec((tm, LANES), lambda i, j: (i, 0))
    return pl.pallas_call(
        functools.partial(_kv_up_kernel, heads=heads),
        grid=(t // tm, MLA_HEADS // heads),
        in_specs=[
            pl.BlockSpec((tm, r), lambda i, j: (i, 0)),
            tab,
            pl.BlockSpec((r, heads * MLA_NOPE), lambda i, j: (0, j)),
            pl.BlockSpec((r, heads * MLA_NOPE), lambda i, j: (0, j)),
            pl.BlockSpec((1, MLA_HEAD_PAD), lambda i, j: (0, 0)),
            tab, tab, tab,
        ],
        out_specs=[
            pl.BlockSpec((tm, heads * MLA_HEAD_PAD), lambda i, j: (i, j)),
            pl.BlockSpec((tm, heads * MLA_NOPE), lambda i, j: (i, j)),
        ],
        out_shape=[
            jax.ShapeDtypeStruct((t, MLA_HEADS * MLA_HEAD_PAD), BF16),
            jax.ShapeDtypeStruct((t, MLA_HEADS * MLA_NOPE), BF16),
        ],
        compiler_params=_params("arbitrary", "arbitrary"),
        name="mla_kv_up",
    )(ckv, kr, w_uk, w_uv, g_kn_ext, rope_c, rope_s1, rope_s2)


ATTN_HEADS_PER_STEP = 4


def _attn_kernel(*refs, nseg):
    q_ref = refs[0]
    k_refs = refs[1:1 + nseg]
    v_refs = refs[1 + nseg:1 + 2 * nseg]
    o_ref = refs[-1]
    scale = MLA_QK_DIM ** -0.5
    nt = (((1,), (1,)), ((), ()))
    for hh in range(ATTN_HEADS_PER_STEP):
        qk_cols = slice(hh * MLA_HEAD_PAD, (hh + 1) * MLA_HEAD_PAD)
        v_cols = slice(hh * MLA_NOPE, (hh + 1) * MLA_NOPE)
        q = q_ref[:, qk_cols]
        s = [lax.dot_general(q, k[:, qk_cols], nt, preferred_element_type=F32) * scale for k in k_refs]
        m = functools.reduce(jnp.maximum, [jnp.max(x, axis=-1, keepdims=True) for x in s])
        e = [jnp.exp(x - m) for x in s]
        inv = 1.0 / functools.reduce(lambda a, b: a + b, [jnp.sum(x, axis=-1, keepdims=True) for x in e])
        o = functools.reduce(
            lambda a, b: a + b,
            [jnp.dot((x * inv).astype(BF16), v[:, v_cols], preferred_element_type=F32) for x, v in zip(e, v_refs)])
        o_ref[:, v_cols] = o.astype(o_ref.dtype)


def mla_attention(q, k, v, *, q_row0, n_batch, s_q, segs):
    tq = 256
    nq = s_q // tq
    nseg = len(segs)
    q_blk0 = q_row0 // tq
    hp = ATTN_HEADS_PER_STEP
    in_specs = [pl.BlockSpec((tq, hp * MLA_HEAD_PAD), lambda b, h, i: (q_blk0 + b * nq + i, h))]
    for row0, length in segs:
        in_specs.append(pl.BlockSpec((length, hp * MLA_HEAD_PAD), lambda b, h, i, o=row0 // length: (o + b, h)))
    for row0, length in segs:
        in_specs.append(pl.BlockSpec((length, hp * MLA_NOPE), lambda b, h, i, o=row0 // length: (o + b, h)))
    return pl.pallas_call(
        functools.partial(_attn_kernel, nseg=nseg),
        grid=(n_batch, MLA_HEADS // hp, nq),
        in_specs=in_specs,
        out_specs=pl.BlockSpec((tq, hp * MLA_NOPE), lambda b, h, i: (b * nq + i, h)),
        out_shape=jax.ShapeDtypeStruct((n_batch * s_q, MLA_HEADS * MLA_NOPE), BF16),
        compiler_params=_params("arbitrary", "arbitrary", "arbitrary"),
        name="mla_attention",
    )(q, *([k] * nseg), *([v] * nseg))


def _rope_tables():
    nf = MLA_ROPE // 4
    inv_freq = jnp.power(ROPE_BASE, -jnp.arange(nf, dtype=F32) / nf)
    tok = jnp.arange(DEC_SEQ)
    row = (tok // GRID_W).astype(F32)[:, None] * inv_freq[None, :]
    col = (tok % GRID_W).astype(F32)[:, None] * inv_freq[None, :]
    ang = jnp.concatenate([row, col], axis=-1)
    cos, sin = jnp.cos(ang), jnp.sin(ang)
    z32 = jnp.zeros_like(cos)
    z64 = jnp.zeros((DEC_SEQ, 64), F32)
    c = jnp.concatenate([cos, cos, z64], axis=-1)
    s1 = jnp.concatenate([-sin, z32, z64], axis=-1)
    s2 = jnp.concatenate([z32, sin, z64], axis=-1)
    ident_c = jnp.concatenate([jnp.ones((1, 64), F32), jnp.zeros((1, 64), F32)], axis=-1)

    def full(lat, ident):
        n_cache = DEC_BATCH * PAST_LEN
        return jnp.concatenate([
            jnp.broadcast_to(ident, (T_CTX, LANES)),
            jnp.tile(lat, (DEC_BATCH, 1)),
            jnp.broadcast_to(ident, (n_cache, LANES)),
        ], axis=0)

    zero = jnp.zeros((1, LANES), F32)
    return full(c, ident_c), full(s1, zero), full(s2, zero)


_ROPE_PERM = np.concatenate([np.arange(0, 16), np.arange(32, 48), np.arange(16, 32), np.arange(48, 64)])


def mla_layer(h, cache_ckv, cache_krope, w_in, g_q, g_kv, w_uq, g_qn, w_uk, w_uv, g_kn):
    d = h.shape[1]
    perm = _ROPE_PERM
    n_lat = MLA_Q_RANK + MLA_KV_RANK
    w_in_ext = jnp.concatenate([w_in[:, :n_lat], w_in[:, n_lat:][:, perm], w_in[:, n_lat:]], axis=1).astype(BF16)
    cq, ckv, kr = mla_in_proj(h, w_in_ext, g_q, g_kv)

    w3 = w_uq.reshape(MLA_Q_RANK, MLA_HEADS, MLA_QK_DIM)
    w_uq_ext = jnp.concatenate([
        w3[:, :, :MLA_NOPE], w3[:, :, MLA_NOPE:][:, :, perm],
        jnp.zeros((MLA_Q_RANK, MLA_HEADS, MLA_HEAD_PAD - MLA_QK_DIM), w_uq.dtype)], axis=-1)
    w_uq_ext = w_uq_ext.reshape(MLA_Q_RANK, MLA_HEADS * MLA_HEAD_PAD).astype(BF16)

    def gain_ext(g):
        return jnp.concatenate([g[:MLA_NOPE], g[MLA_NOPE:][perm],
                                jnp.zeros((MLA_HEAD_PAD - MLA_QK_DIM,), g.dtype)]).reshape(1, MLA_HEAD_PAD)

    rope_c, rope_s1, rope_s2 = _rope_tables()
    q = mla_q_up(cq, w_uq_ext, gain_ext(g_qn), rope_c[:T_ALL], rope_s1[:T_ALL], rope_s2[:T_ALL])

    n_cache = DEC_BATCH * PAST_LEN
    ckv_all = jnp.concatenate([ckv, cache_ckv.reshape(n_cache, MLA_KV_RANK)], axis=0)
    kr_cache = cache_krope.reshape(n_cache, MLA_ROPE)
    kr_cache = jnp.concatenate([kr_cache[:, perm], kr_cache], axis=1)
    kr_all = jnp.concatenate([kr, kr_cache], axis=0)
    k, v = mla_kv_up(ckv_all, kr_all, w_uk.astype(BF16), w_uv.astype(BF16), gain_ext(g_kn),
                     rope_c, rope_s1, rope_s2)

    o_ctx = mla_attention(q, k, v, q_row0=0, n_batch=BATCH, s_q=SEQ, segs=[(0, SEQ)])
    o_lat = mla_attention(q, k, v, q_row0=T_CTX, n_batch=DEC_BATCH, s_q=DEC_SEQ,
                          segs=[(T_ALL, PAST_LEN), (T_CTX, DEC_SEQ)])
    y = (o_ctx, o_lat)
    new_ckv = ckv[:T_CTX].reshape(BATCH, 1, SEQ, MLA_KV_RANK)
    new_krope = kr[:T_CTX, MLA_ROPE:].reshape(BATCH, 1, SEQ, MLA_ROPE)
    return y, new_ckv, new_krope


def _log_sigmoid(x):
    return jnp.minimum(x, 0.0) - jnp.log(1.0 + jnp.exp(-jnp.abs(x)))


def _gates_kernel(h_ref, wg_ref, wgt_ref, b_ref, bt_ref, gc_ref, gr_ref):
    h = h_ref[...]
    gc = jnp.dot(h, wg_ref[...], preferred_element_type=F32) + b_ref[...]
    gr = lax.dot_general(wgt_ref[...], h, (((1,), (1,)), ((), ())), preferred_element_type=F32) + bt_ref[...]
    lane = lax.broadcasted_iota(I32, (1, LANES), 1)
    is_f = ((lane >= ML_HEADS) & (lane < 2 * ML_HEADS)) | ((lane >= 3 * ML_HEADS) & (lane < 4 * ML_HEADS))
    gc_ref[...] = jnp.where(is_f, _log_sigmoid(gc), gc)
    row = lax.broadcasted_iota(I32, (4 * ML_HEADS, 1), 0)
    is_fr = ((row >= ML_HEADS) & (row < 2 * ML_HEADS)) | ((row >= 3 * ML_HEADS) & (row < 4 * ML_HEADS))
    gr_ref[...] = jnp.where(is_fr, _log_sigmoid(gr), gr)


def mlstm_gates(h, w_g, b_gate):
    t, d = h.shape
    ng = 4 * ML_HEADS
    tm = 512
    wg = jnp.concatenate([w_g, jnp.zeros((d, LANES - ng), w_g.dtype)], axis=1).astype(BF16)
    wgt = w_g.T.astype(BF16)
    b = jnp.concatenate([b_gate, jnp.zeros((LANES - ng,), b_gate.dtype)]).reshape(1, LANES)
    bt = b_gate.reshape(ng, 1)
    return pl.pallas_call(
        _gates_kernel,
        grid=(t // tm,),
        in_specs=[
            pl.BlockSpec((tm, d), lambda i: (i, 0)),
            pl.BlockSpec((d, LANES), lambda i: (0, 0)),
            pl.BlockSpec((ng, d), lambda i: (0, 0)),
            pl.BlockSpec((1, LANES), lambda i: (0, 0)),
            pl.BlockSpec((ng, 1), lambda i: (0, 0)),
        ],
        out_specs=[pl.BlockSpec((tm, LANES), lambda i: (i, 0)), pl.BlockSpec((ng, tm), lambda i: (0, i))],
        out_shape=[jax.ShapeDtypeStruct((t, LANES), F32), jax.ShapeDtypeStruct((ng, t), F32)],
        compiler_params=_params("arbitrary"),
        name="mlstm_gates",
    )(h, wg, wgt, b, bt)


ML_HEADS_PER_STEP = 2


def _mlstm_kernel(*refs, nc, has_init, emit_state):
    refs = list(refs)
    q_ref, k_ref, v_ref, o_ref, gc_ref, gr_ref, gh_ref = [refs.pop(0) for _ in range(7)]
    if has_init:
        c0_ref, n0_ref, m0_ref = [refs.pop(0) for _ in range(3)]
    y_ref = refs.pop(0)
    if emit_state:
        cf_ref, nf_ref, mf_ref = [refs.pop(0) for _ in range(3)]
    n_chain = 2 * ML_HEADS_PER_STEP
    mem_s, nrm_s, m_s, hs_s = (refs[j * n_chain:(j + 1) * n_chain] for j in range(4))

    L = ML_CHUNK
    tt = lax.broadcasted_iota(I32, (L, L), 0)
    ss = lax.broadcasted_iota(I32, (L, L), 1)
    q_scale = ML_DK ** -0.5
    nt = (((1,), (1,)), ((), ()))
    tn = (((0,), (0,)), ((), ()))

    for hh in range(ML_HEADS_PER_STEP):
        for d in range(2):
            ch = 2 * hh + d
            if has_init:
                mem_s[ch][...] = c0_ref[0, d, hh]
                nrm_s[ch][...] = n0_ref[0, d, hh]
                m_s[ch][...] = m0_ref[0, d, hh]
            else:
                mem_s[ch][...] = jnp.zeros(mem_s[ch].shape, F32)
                nrm_s[ch][...] = jnp.zeros(nrm_s[ch].shape, F32)
                m_s[ch][...] = jnp.zeros(m_s[ch].shape, F32)

    def chunk(hh, d, c):
        ch = 2 * hh + d
        kq = slice(hh * ML_DK, (hh + 1) * ML_DK)
        kv = slice(hh * ML_DV, (hh + 1) * ML_DV)
        causal = (ss <= tt) if d == 0 else (ss >= tt)
        causal_t = (tt <= ss) if d == 0 else (tt >= ss)
        last = L - 1 if d == 0 else 0
        gcol = gc_ref[0, hh, c]
        grow = gr_ref[0, hh, c]
        i_col = gcol[:, 2 * d:2 * d + 1]
        f_col = gcol[:, 2 * d + 1:2 * d + 2]
        i_row = grow[2 * d:2 * d + 1, :]
        f_row = grow[2 * d + 1:2 * d + 2, :]
        cum_col = jnp.sum(jnp.where(causal, jnp.broadcast_to(f_row, (L, L)), 0.0), axis=1, keepdims=True)
        cum_row = jnp.sum(jnp.where(causal_t, jnp.broadcast_to(f_col, (L, L)), 0.0), axis=0, keepdims=True)
        total = cum_col[last:last + 1, :]
        m_prev = m_s[ch][:, 0:1]
        dmat = jnp.where(causal, cum_col - cum_row + i_row, NEG_INF)
        inter = cum_col + m_prev
        m_t = jnp.maximum(inter, jnp.max(dmat, axis=1, keepdims=True))
        w_inter = jnp.exp(inter - m_t)
        rows = pl.ds(pl.multiple_of(c * L, L), L)
        qf = q_ref[rows, kq] * q_scale
        kf = k_ref[rows, kq]
        vb = v_ref[rows, kv].astype(BF16)
        qb = qf.astype(BF16)
        qk = lax.dot_general(qb, kf.astype(BF16), nt, preferred_element_type=F32)
        a = jnp.exp(dmat - m_t) * qk
        mem = mem_s[ch][...]
        nrm = nrm_s[ch][...]
        num = (w_inter * jnp.dot(qb, mem.astype(BF16), preferred_element_type=F32)
               + jnp.dot(a.astype(BF16), vb, preferred_element_type=F32))
        den = w_inter * jnp.sum(qf * nrm, axis=1, keepdims=True) + jnp.sum(a, axis=1, keepdims=True)
        hs_s[ch][rows, :] = num / jnp.maximum(jnp.abs(den), jnp.exp(-m_t))
        m_new = m_t[last:last + 1, :]
        decay = jnp.exp(total + m_prev - m_new)
        w_s = jnp.exp(total - cum_col + i_col - m_new)
        wk = w_s * kf
        mem_s[ch][...] = decay * mem + lax.dot_general(wk.astype(BF16), vb, tn, preferred_element_type=F32)
        nrm_s[ch][...] = decay * nrm + jnp.sum(wk, axis=0, keepdims=True)
        m_s[ch][...] = jnp.broadcast_to(m_new, m_s[ch].shape)

    def all_chains(ci, carry):
        for hh in range(ML_HEADS_PER_STEP):
            chunk(hh, 0, ci)
            chunk(hh, 1, nc - 1 - ci)
        return carry

    lax.fori_loop(0, nc, all_chains, 0)
    for hh in range(ML_HEADS_PER_STEP):
        if emit_state:
            for d in range(2):
                cf_ref[0, d, hh] = mem_s[2 * hh + d][...]
                nf_ref[0, d, hh] = nrm_s[2 * hh + d][...]
                mf_ref[0, d, hh] = m_s[2 * hh + d][...]
        kv = slice(hh * ML_DV, (hh + 1) * ML_DV)
        hs = hs_s[2 * hh][...] + hs_s[2 * hh + 1][...]
        hn = hs * lax.rsqrt(jnp.mean(hs * hs, axis=-1, keepdims=True) + NORM_EPS) * gh_ref[:, kv]
        y_ref[:, kv] = (hn * jax.nn.sigmoid(o_ref[:, kv])).astype(y_ref.dtype)


def mlstm_scan(p, gcol, grow, g_h, *, row0, n_batch, seq, state=None, emit_state=False):
    nc = seq // ML_CHUNK
    rb0 = row0 // seq
    hps = ML_HEADS_PER_STEP
    groups = ML_HEADS // hps
    has_init = state is not None
    in_specs = [
        pl.BlockSpec((seq, hps * ML_DK), lambda b, h: (rb0 + b, h)),
        pl.BlockSpec((seq, hps * ML_DK), lambda b, h: (rb0 + b, groups + h)),
        pl.BlockSpec((seq, hps * ML_DV), lambda b, h: (rb0 + b, groups + h)),
        pl.BlockSpec((seq, hps * ML_DV), lambda b, h: (rb0 + b, 2 * groups + h)),
        pl.BlockSpec((1, hps, nc, ML_CHUNK, 4), lambda b, h: (b, h, 0, 0, 0)),
        pl.BlockSpec((1, hps, nc, 4, ML_CHUNK), lambda b, h: (b, h, 0, 0, 0)),
        pl.BlockSpec((1, hps * ML_DV), lambda b, h: (0, h)),
    ]
    args = [p, p, p, p, gcol, grow, g_h.reshape(1, -1)]
    c_spec = pl.BlockSpec((1, 2, hps, ML_DK, ML_DV), lambda b, h: (b, 0, h, 0, 0))
    n_spec = pl.BlockSpec((1, 2, hps, 1, ML_DK), lambda b, h: (b, 0, h, 0, 0))
    m_spec = pl.BlockSpec((1, 2, hps, 1, LANES), lambda b, h: (b, 0, h, 0, 0))
    if has_init:
        in_specs += [c_spec, n_spec, m_spec]
        args += list(state)
    out_specs = [pl.BlockSpec((seq, hps * ML_DV), lambda b, h: (b, h))]
    out_shape = [jax.ShapeDtypeStruct((n_batch * seq, ML_HEADS * ML_DV), BF16)]
    if emit_state:
        out_specs += [c_spec, n_spec, m_spec]
        out_shape += [
            jax.ShapeDtypeStruct((n_batch, 2, ML_HEADS, ML_DK, ML_DV), F32),
            jax.ShapeDtypeStruct((n_batch, 2, ML_HEADS, 1, ML_DK), F32),
            jax.ShapeDtypeStruct((n_batch, 2, ML_HEADS, 1, LANES), F32),
        ]
    return pl.pallas_call(
        functools.partial(_mlstm_kernel, nc=nc, has_init=has_init, emit_state=emit_state),
        grid=(n_batch, groups),
        in_specs=in_specs,
        out_specs=out_specs,
        out_shape=out_shape,
        scratch_shapes=(
            [pltpu.VMEM((ML_DK, ML_DV), F32)] * (2 * hps) + [pltpu.VMEM((1, ML_DK), F32)] * (2 * hps)
            + [pltpu.VMEM((1, LANES), F32)] * (2 * hps) + [pltpu.VMEM((seq, ML_DV), F32)] * (2 * hps)),
        compiler_params=_params("arbitrary", "arbitrary"),
        name="mlstm_scan",
    )(*args)


def _gate_layouts(gc, gr, row0, n_batch, seq):
    nc = seq // ML_CHUNK
    n = n_batch * seq
    gcol = gc[row0:row0 + n, :4 * ML_HEADS].reshape(n_batch, nc, ML_CHUNK, 4, ML_HEADS).transpose(0, 4, 1, 2, 3)
    grow = gr[:, row0:row0 + n].reshape(4, ML_HEADS, n_batch, nc, ML_CHUNK).transpose(2, 1, 3, 0, 4)
    return gcol, grow


def mlstm_layer(h, state_c, state_n, state_m, w_in, b_gate, g_h):
    hk = ML_HEADS * ML_DK
    hv = ML_HEADS * ML_DV
    n_main = 2 * hk + 2 * hv
    p = matmul(h, w_in.astype(BF16), tm=1024, tn=1024 + 512, out_dtype=F32, n_cols=n_main)
    gc, gr = mlstm_gates(h, w_in[:, n_main:], b_gate)

    gcol, grow = _gate_layouts(gc, gr, 0, BATCH, SEQ)
    y_ctx, cf, nf, mf = mlstm_scan(p, gcol, grow, g_h, row0=0, n_batch=BATCH, seq=SEQ, emit_state=True)

    gcol, grow = _gate_layouts(gc, gr, T_CTX, DEC_BATCH, DEC_SEQ)
    c0 = state_c[:, 0]
    n0 = state_n[:, 0].reshape(DEC_BATCH, 2, ML_HEADS, 1, ML_DK)
    m0 = jnp.broadcast_to(state_m[:, 0].reshape(DEC_BATCH, 2, ML_HEADS, 1, 1), (DEC_BATCH, 2, ML_HEADS, 1, LANES))
    (y_lat,) = mlstm_scan(p, gcol, grow, g_h, row0=T_CTX, n_batch=DEC_BATCH, seq=DEC_SEQ, state=(c0, n0, m0))

    y = (y_ctx, y_lat)
    new_c = cf.reshape(BATCH, 1, 2, ML_HEADS, ML_DK, ML_DV)
    new_n = nf.reshape(BATCH, 1, 2, ML_HEADS, ML_DK)
    new_m = mf[..., 0, 0].reshape(BATCH, 1, 2, ML_HEADS)
    return y, new_c, new_n, new_m


def _topk_rows(x, payload, n_out):
    rows = x.shape[0]
    iota = lax.broadcasted_iota(I32, x.shape, 0)
    vals, outs = [], []
    for _ in range(n_out):
        m = jnp.max(x, axis=0, keepdims=True)
        pos = jnp.min(jnp.where(x == m, iota, rows), axis=0, keepdims=True)
        sel = iota == pos
        vals.append(m)
        outs.append(pos if payload is None else jnp.max(jnp.where(sel, payload, -1), axis=0, keepdims=True))
        x = jnp.where(sel, NEG_INF, x)
    return jnp.concatenate(vals, axis=0), jnp.concatenate(outs, axis=0)


def _pair_candidates(a, b, combine, fill):
    k = PEER_TOPK
    row = lax.broadcasted_iota(I32, (SUBLANES, a.shape[1]), 0)
    blocks = [combine(a[0:1, :], b), combine(a[1:2, :], b[0:SUBLANES, :])]
    for i in range(2, SUBLANES):
        blocks.append(jnp.where(row < k // (i + 1), combine(a[i:i + 1, :], b[0:SUBLANES, :]), fill))
    blocks.append(combine(a[SUBLANES:k, :], b[0:1, :]))
    return jnp.concatenate(blocks, axis=0)


PEER_TILE = 128


def _peer_head_topk(qh, keys_ref):
    half = PEER_QDIM // 2
    nt = (((1,), (1,)), ((), ()))
    sv, si = [], []
    for p in range(2):
        qs = qh[:, p * half:(p + 1) * half].astype(BF16)
        sc = lax.dot_general(keys_ref[p], qs, nt, preferred_element_type=F32)
        v, i = _topk_rows(sc, None, PEER_TOPK)
        sv.append(v)
        si.append(i)
    cand = _pair_candidates(sv[0], sv[1], lambda x, y: x + y, NEG_INF)
    cidx = _pair_candidates(si[0], si[1], lambda x, y: x * PEER_NKEYS + y, -1)
    best, eidx = _topk_rows(cand, cidx, PEER_TOPK)
    ex = jnp.exp(best - best[0:1, :])
    return eidx, ex / jnp.sum(ex, axis=0, keepdims=True)


def _peer_topk_kernel(q_ref, keys_ref, e_ref, gw_ref):
    parts = [_peer_head_topk(q_ref[hd], keys_ref) for hd in range(PEER_HEADS)]
    e_ref[...] = jnp.concatenate([p[0] for p in parts], axis=0).T
    gw_ref[...] = jnp.concatenate([p[1] for p in parts], axis=0).T


def peer_topk(q3, keys, n_tokens):
    tt = PEER_TILE
    return pl.pallas_call(
        _peer_topk_kernel,
        grid=(n_tokens // tt,),
        in_specs=[pl.BlockSpec((PEER_HEADS, tt, PEER_QDIM), lambda i: (0, i, 0)),
                  pl.BlockSpec(keys.shape, lambda i: (0, 0, 0))],
        out_specs=[pl.BlockSpec((tt, PEER_SEL), lambda i: (i, 0)), pl.BlockSpec((tt, PEER_SEL), lambda i: (i, 0))],
        out_shape=[jax.ShapeDtypeStruct((n_tokens, PEER_SEL), I32), jax.ShapeDtypeStruct((n_tokens, PEER_SEL), F32)],
        compiler_params=_params("arbitrary"),
        name="peer_topk",
    )(q3, keys)


def _peer_query_kernel(x_ref, w_ref, o_ref):
    acc = jnp.dot(x_ref[...], w_ref[...], preferred_element_type=F32)
    for hd in range(PEER_HEADS):
        o_ref[hd] = acc[:, hd * PEER_QDIM:(hd + 1) * PEER_QDIM]


def peer_query(h, w_q):
    t, d = h.shape
    tm = min(1024, t)
    return pl.pallas_call(
        _peer_query_kernel,
        grid=(t // tm,),
        in_specs=[pl.BlockSpec((tm, d), lambda i: (i, 0)), pl.BlockSpec(w_q.shape, lambda i: (0, 0))],
        out_specs=pl.BlockSpec((PEER_HEADS, tm, PEER_QDIM), lambda i: (0, i, 0)),
        out_shape=jax.ShapeDtypeStruct((PEER_HEADS, t, PEER_QDIM), F32),
        compiler_params=_params("arbitrary"),
        name="peer_query",
    )(h, w_q)


PEER_STEP_TOK = 16
PEER_SLABS = D_MODEL // LANES
PEER_ROWS = PEER_SEL * PEER_SLABS
_ERF_GELU_C = 0.7071067811865476


def _peer_apply_kernel(e_ref, en_ref, h_ref, gw_ref, expand_ref, expand_t_ref, uv_ref, o_ref,
                       buf_a, buf_b, zs, wexp, sem, *, layer):
    i = pl.program_id(0)
    n_steps = pl.num_programs(0)
    ns = PEER_SLABS
    tt = PEER_STEP_TOK
    nt = (((1,), (1,)), ((), ()))

    def gather_copy(idx, buf, n, s):
        return pltpu.make_async_copy(uv_ref.at[layer, idx], buf.at[n], sem.at[s])

    def issue(idx_ref, buf, s):
        for tok in range(tt):
            for k in range(PEER_SEL):
                gather_copy(idx_ref[tok, k], buf, tok * PEER_SEL + k, s).start(priority=k % 2)

    def wait(buf, s):
        pltpu.make_async_copy(buf, buf, sem.at[s]).wait()

    def compute(buf):
        sub = lax.broadcasted_iota(I32, (ns, PEER_ROWS), 0)
        col = lax.broadcasted_iota(I32, (ns, PEER_ROWS), 1)
        diag = (col % ns) == sub
        for t in range(tt):
            u_t = buf[pl.ds(t * PEER_SEL, PEER_SEL), 0].reshape(PEER_ROWS, LANES)
            y = lax.dot_general(h_ref[t], u_t, nt, preferred_element_type=F32)
            zs[pl.ds(t, 1), :] = jnp.sum(jnp.where(diag, y, 0.0), axis=0, keepdims=True)
        z = zs[...]
        z_hi = z.astype(BF16)
        z_lo = (z - z_hi.astype(F32)).astype(BF16)
        act = (jnp.dot(z_hi, expand_t_ref[...], preferred_element_type=F32)
               + jnp.dot(z_lo, expand_t_ref[...], preferred_element_type=F32))
        gelu = 0.5 * act * (1.0 + lax.erf(act * _ERF_GELU_C))
        w = (gw_ref[...] * gelu).astype(BF16)
        wexp[...] = jnp.dot(w, expand_ref[...], preferred_element_type=F32)
        for t in range(tt):
            wrow = wexp[pl.ds(t, 1), :]
            wbig = jnp.where(diag, jnp.broadcast_to(wrow, (ns, PEER_ROWS)), 0.0).astype(BF16)
            v_t = buf[pl.ds(t * PEER_SEL, PEER_SEL), 1].reshape(PEER_ROWS, LANES)
            o_ref[t] = jnp.dot(wbig, v_t, preferred_element_type=F32)

    def step(cur, s_cur, nxt, s_nxt):
        wait(cur, s_cur)
        issue(en_ref, nxt, s_nxt)
        compute(cur)

        @pl.when(i == n_steps - 1)
        def _():
            wait(nxt, s_nxt)

    @pl.when(i == 0)
    def _():
        issue(e_ref, buf_a, 0)

    @pl.when(i % 2 == 0)
    def _():
        step(buf_a, 0, buf_b, 1)

    @pl.when(i % 2 == 1)
    def _():
        step(buf_b, 1, buf_a, 0)


def peer_apply(e, h, gw, uv, layer):
    t, d = h.shape
    ns = PEER_SLABS
    tt = PEER_STEP_TOK
    n_steps = t // tt
    h3 = h.reshape(t, ns, LANES)
    group = np.arange(PEER_ROWS) // ns
    expand = jnp.asarray(group[None, :] == np.arange(PEER_SEL)[:, None], BF16)
    out = pl.pallas_call(
        functools.partial(_peer_apply_kernel, layer=layer),
        grid=(n_steps,),
        in_specs=[
            pl.BlockSpec((tt, PEER_SEL), lambda i: (i, 0), memory_space=pltpu.SMEM),
            pl.BlockSpec((tt, PEER_SEL), lambda i: (jnp.minimum(i + 1, n_steps - 1), 0), memory_space=pltpu.SMEM),
            pl.BlockSpec((tt, ns, LANES), lambda i: (i, 0, 0)),
            pl.BlockSpec((tt, PEER_SEL), lambda i: (i, 0)),
            pl.BlockSpec((PEER_SEL, PEER_ROWS), lambda i: (0, 0)),
            pl.BlockSpec((PEER_ROWS, PEER_SEL), lambda i: (0, 0)),
            pl.BlockSpec(memory_space=pl.ANY),
        ],
        out_specs=pl.BlockSpec((tt, ns, LANES), lambda i: (i, 0, 0)),
        out_shape=jax.ShapeDtypeStruct((t, ns, LANES), F32),
        scratch_shapes=[
            pltpu.VMEM((tt * PEER_SEL, 2, ns, LANES), BF16),
            pltpu.VMEM((tt * PEER_SEL, 2, ns, LANES), BF16),
            pltpu.VMEM((tt, PEER_ROWS), F32),
            pltpu.VMEM((tt, PEER_ROWS), F32),
            pltpu.SemaphoreType.DMA((2,)),
        ],
        compiler_params=pltpu.CompilerParams(dimension_semantics=("arbitrary",), vmem_limit_bytes=VMEM_LIMIT_BYTES,
                                             disable_bounds_checks=True),
        name="peer_apply",
    )(e, e, h3, gw, expand, expand.T, uv)
    return out.reshape(t, d)


def peer_tables(peer_u, peer_v):
    depth, n_exp, _ = peer_u.shape
    shape = (depth, n_exp, PEER_SLABS, LANES)
    return jnp.stack([peer_u.astype(BF16).reshape(shape), peer_v.astype(BF16).reshape(shape)], axis=2)


def peer_layer(h, w_q, keys, uv, layer):
    q3 = peer_query(h, w_q.astype(BF16))
    e, gw = peer_topk(q3, keys.astype(BF16), h.shape[0])
    return peer_apply(e, h, gw, uv, layer)


def kernel(x_prompt, x_sample, cache_ckv, cache_krope, state_C, state_n, state_m, c, c_ctx, mod_w, mod_b, norm_mix, norm_ffn, mla_w_in, mla_g_q, mla_g_kv, mla_w_uq, mla_g_qn, mla_w_uk, mla_w_uv, mla_g_kn, mla_w_o, ml_w_in, ml_b_gate, ml_g_h, ml_w_o, peer_w_q, peer_keys, peer_u, peer_v):
    d = D_MODEL
    x = (x_prompt.reshape(T_CTX, d), x_sample.reshape(T_LAT, d))
    cvec = jnp.concatenate([c_ctx.reshape(1, d), c, jnp.zeros((GROUP_PAD - N_GROUPS, d), c.dtype)], axis=0)
    mod = mod_vectors(cvec, mod_w, mod_b)
    modrows = mod.reshape(mod.shape[0] * GROUP_PAD * N_MOD, 1, d)
    uv = peer_tables(peer_u, peer_v)

    (h,) = resid_modulate(x, modrows, norm_g=norm_mix[0], shift=(0, 0), scale=(0, 1))
    y, new_ckv, new_krope = mla_layer(h, cache_ckv[:, 0], cache_krope[:, 0], mla_w_in[0], mla_g_q[0], mla_g_kv[0],
                                      mla_w_uq[0], mla_g_qn[0], mla_w_uk[0], mla_w_uv[0], mla_g_kn[0])
    x, h = resid_modulate(x, modrows, y=y, w=mla_w_o[0].astype(BF16), gate=(0, 2),
                          norm_g=norm_ffn[0], shift=(0, 3), scale=(0, 4))
    y = peer_layer(h, peer_w_q[0], peer_keys[0], uv, 0)

    x, h = resid_modulate(x, modrows, y=y, gate=(0, 5), norm_g=norm_mix[1], shift=(1, 0), scale=(1, 1))
    y, new_c, new_n, new_m = mlstm_layer(h, state_C, state_n, state_m, ml_w_in[0], ml_b_gate[0], ml_g_h[0])
    x, h = resid_modulate(x, modrows, y=y, w=ml_w_o[0].astype(BF16), gate=(1, 2),
                          norm_g=norm_ffn[1], shift=(1, 3), scale=(1, 4))
    y = peer_layer(h, peer_w_q[1], peer_keys[1], uv, 1)
    y_prompt, y_sample = resid_modulate(x, modrows, y=y, gate=(1, 5), split_out=True)

    return (y_prompt.reshape(BATCH, SEQ, d), y_sample.reshape(DEC_BATCH, DEC_SEQ, d),
            new_ckv, new_krope, new_c, new_n, new_m)
```

```python
import functools

import numpy as np
import jax
import jax.numpy as jnp
from jax import lax
from jax.experimental import pallas as pl
from jax.experimental.pallas import tpu as pltpu

F32 = jnp.float32
BF16 = jnp.bfloat16
I32 = jnp.int32

D_MODEL = 2048
BATCH, SEQ = 32, 256
DEC_BATCH, DEC_SEQ = 8, 1024
PAST_LEN = 512
GRID_W = 64
N_MOD = 6
NORM_EPS = 1e-6
MLA_HEADS = 16
MLA_Q_RANK = 512
MLA_KV_RANK = 512
MLA_NOPE = 128
MLA_ROPE = 64
MLA_QK_DIM = MLA_NOPE + MLA_ROPE
MLA_HEAD_PAD = 256
ROPE_BASE = 10000.0
ML_HEADS = 8
ML_DV = D_MODEL // ML_HEADS
ML_DK = ML_DV // 2
ML_CHUNK = 64
PEER_HEADS = 8
PEER_NKEYS = 128
PEER_QDIM = 128
PEER_TOPK = 16
PEER_SEL = PEER_HEADS * PEER_TOPK

T_CTX = BATCH * SEQ
T_LAT = DEC_BATCH * DEC_SEQ
T_ALL = T_CTX + T_LAT
N_GROUPS = 1 + DEC_BATCH
GROUP_PAD = 16
ROW_TILE = 256

VMEM_LIMIT_BYTES = 56 * 1024 * 1024
LANES = 128
SUBLANES = 8

NEG_INF = float("-inf")


def _params(*sem):
    return pltpu.CompilerParams(dimension_semantics=sem, vmem_limit_bytes=VMEM_LIMIT_BYTES)


def _mod_kernel(c_ref, w_ref, b_ref, o_ref):
    c = c_ref[...]
    a = (c * jax.nn.sigmoid(c)).astype(BF16)
    o_ref[0] = jnp.dot(a, w_ref[0].astype(BF16), preferred_element_type=F32) + b_ref[0]


def mod_vectors(cvec, mod_w, mod_b):
    depth, d, n = mod_w.shape
    tn = 1024
    return pl.pallas_call(
        _mod_kernel,
        grid=(depth, n // tn),
        in_specs=[
            pl.BlockSpec((GROUP_PAD, d), lambda l, j: (0, 0)),
            pl.BlockSpec((1, d, tn), lambda l, j: (l, 0, j)),
            pl.BlockSpec((1, 1, tn), lambda l, j: (l, 0, j)),
        ],
        out_specs=pl.BlockSpec((1, GROUP_PAD, tn), lambda l, j: (l, 0, j)),
        out_shape=jax.ShapeDtypeStruct((depth, GROUP_PAD, n), F32),
        compiler_params=_params("arbitrary", "arbitrary"),
        name="mod_vectors",
    )(cvec, mod_w, mod_b.reshape(depth, 1, n))


def _group_of_tile(i):
    ctx_tiles = T_CTX // ROW_TILE
    tiles_per_lat = DEC_SEQ // ROW_TILE
    return jnp.where(i < ctx_tiles, 0, 1 + (i - ctx_tiles) // tiles_per_lat)


CTX_TILES = T_CTX // ROW_TILE


def _row_tile(parts, i):
    if len(parts) == 1:
        return parts[0][...]
    return jnp.where(i < CTX_TILES, parts[0][...], parts[1][...])


def _resmod_kernel(*refs, nx, ny, has_w, has_mod, split_out):
    i = pl.program_id(0)
    refs = list(refs)
    x = _row_tile([refs.pop(0) for _ in range(nx)], i)
    if ny:
        y = _row_tile([refs.pop(0) for _ in range(ny)], i)
        if has_w:
            y = jnp.dot(y, refs.pop(0)[...], preferred_element_type=F32)
        gate_ref = refs.pop(0)
        x = x + gate_ref[0] * y
    if has_mod:
        g_ref, sh_ref, sc_ref = refs.pop(0), refs.pop(0), refs.pop(0)
    if ny:
        if split_out:
            xa_ref, xb_ref = refs.pop(0), refs.pop(0)

            @pl.when(i < CTX_TILES)
            def _():
                xa_ref[...] = x

            @pl.when(i >= CTX_TILES)
            def _():
                xb_ref[...] = x
        else:
            refs.pop(0)[...] = x
    if has_mod:
        h_ref = refs.pop(0)
        ms = jnp.mean(x * x, axis=-1, keepdims=True)
        yn = x * lax.rsqrt(ms + NORM_EPS) * g_ref[...]
        h_ref[...] = (yn * (1.0 + sc_ref[0]) + sh_ref[0]).astype(h_ref.dtype)


def resid_modulate(x, modrows, *, y=None, w=None, gate=None, norm_g=None, shift=None, scale=None, split_out=False):
    xs = list(x) if isinstance(x, (tuple, list)) else [x]
    ys = [] if y is None else (list(y) if isinstance(y, (tuple, list)) else [y])
    d = xs[0].shape[1]
    t = sum(a.shape[0] for a in xs)
    has_mod = norm_g is not None
    row_spec = pl.BlockSpec((ROW_TILE, d), lambda i: (i, 0))
    ctx_spec = pl.BlockSpec((ROW_TILE, d), lambda i: (jnp.minimum(i, CTX_TILES - 1), 0))
    lat_spec = pl.BlockSpec((ROW_TILE, d), lambda i: (jnp.maximum(i - CTX_TILES, 0), 0))

    def part_specs(parts):
        return [row_spec] if len(parts) == 1 else [ctx_spec, lat_spec]

    def mod_spec(layer_k):
        layer, k = layer_k
        return pl.BlockSpec((1, 1, d), lambda i: ((layer * GROUP_PAD + _group_of_tile(i)) * N_MOD + k, 0, 0))

    args, in_specs = list(xs), part_specs(xs)
    if ys:
        args += ys
        in_specs += part_specs(ys)
        if w is not None:
            args.append(w)
            in_specs.append(pl.BlockSpec(w.shape, lambda i: (0, 0)))
        args.append(modrows)
        in_specs.append(mod_spec(gate))
    if has_mod:
        args += [norm_g.reshape(1, d), modrows, modrows]
        in_specs += [pl.BlockSpec((1, d), lambda i: (0, 0)), mod_spec(shift), mod_spec(scale)]
    out_shape, out_specs = [], []
    if ys:
        if split_out:
            out_shape += [jax.ShapeDtypeStruct((T_CTX, d), F32), jax.ShapeDtypeStruct((t - T_CTX, d), F32)]
            out_specs += [ctx_spec, lat_spec]
        else:
            out_shape.append(jax.ShapeDtypeStruct((t, d), F32))
            out_specs.append(row_spec)
    if has_mod:
        out_shape.append(jax.ShapeDtypeStruct((t, d), BF16))
        out_specs.append(row_spec)
    outs = pl.pallas_call(
        functools.partial(_resmod_kernel, nx=len(xs), ny=len(ys), has_w=w is not None, has_mod=has_mod,
                          split_out=split_out),
        grid=(t // ROW_TILE,),
        in_specs=in_specs,
        out_specs=out_specs,
        out_shape=out_shape,
        compiler_params=_params("arbitrary"),
        name="resid_modulate",
    )(*args)
    return outs


def _mm_kernel(x_ref, w_ref, o_ref):
    o_ref[...] = jnp.dot(x_ref[...].astype(BF16), w_ref[...], preferred_element_type=F32).astype(o_ref.dtype)


def matmul(x, w, *, tm, tn, out_dtype, n_cols=None):
    m, k = x.shape
    n = w.shape[1] if n_cols is None else n_cols
    return pl.pallas_call(
        _mm_kernel,
        grid=(m // tm, n // tn),
        in_specs=[pl.BlockSpec((tm, k), lambda i, j: (i, 0)), pl.BlockSpec((k, tn), lambda i, j: (0, j))],
        out_specs=pl.BlockSpec((tm, tn), lambda i, j: (i, j)),
        out_shape=jax.ShapeDtypeStruct((m, n), out_dtype),
        compiler_params=_params("arbitrary", "arbitrary"),
        name="matmul",
    )(x, w)


def _rms(x, n):
    return lax.rsqrt(jnp.sum(x * x, axis=-1, keepdims=True) / n + NORM_EPS)


def _mla_in_kernel(h_ref, w_ref, gq_ref, gkv_ref, cq_ref, ckv_ref, kr_ref):
    a = jnp.dot(h_ref[...], w_ref[...], preferred_element_type=F32)
    cq = a[:, :MLA_Q_RANK]
    ckv = a[:, MLA_Q_RANK:MLA_Q_RANK + MLA_KV_RANK]
    cq_ref[...] = (cq * _rms(cq, MLA_Q_RANK) * gq_ref[...]).astype(cq_ref.dtype)
    ckv_ref[...] = ckv * _rms(ckv, MLA_KV_RANK) * gkv_ref[...]
    kr_ref[...] = a[:, MLA_Q_RANK + MLA_KV_RANK:]


def mla_in_proj(h, w_ext, g_q, g_kv):
    t, d = h.shape
    n = w_ext.shape[1]
    tm = 512
    return pl.pallas_call(
        _mla_in_kernel,
        grid=(t // tm,),
        in_specs=[
            pl.BlockSpec((tm, d), lambda i: (i, 0)),
            pl.BlockSpec((d, n), lambda i: (0, 0)),
            pl.BlockSpec((1, MLA_Q_RANK), lambda i: (0, 0)),
            pl.BlockSpec((1, MLA_KV_RANK), lambda i: (0, 0)),
        ],
        out_specs=[
            pl.BlockSpec((tm, MLA_Q_RANK), lambda i: (i, 0)),
            pl.BlockSpec((tm, MLA_KV_RANK), lambda i: (i, 0)),
            pl.BlockSpec((tm, LANES), lambda i: (i, 0)),
        ],
        out_shape=[
            jax.ShapeDtypeStruct((t, MLA_Q_RANK), BF16),
            jax.ShapeDtypeStruct((t, MLA_KV_RANK), F32),
            jax.ShapeDtypeStruct((t, LANES), F32),
        ],
        compiler_params=_params("arbitrary"),
        name="mla_in_proj",
    )(h, w_ext, g_q.reshape(1, -1), g_kv.reshape(1, -1))


def _rope_block(x, c, s1, s2):
    return x * c + pltpu.roll(x, 96, 1) * s1 + pltpu.roll(x, 32, 1) * s2


def _q_up_kernel(cq_ref, w_ref, g_ref, c_ref, s1_ref, s2_ref, q_ref, *, heads):
    a = jnp.dot(cq_ref[...], w_ref[...], preferred_element_type=F32)
    g = g_ref[...]
    for hh in range(heads):
        base = hh * MLA_HEAD_PAD
        nope = a[:, base:base + MLA_NOPE]
        rp = a[:, base + MLA_NOPE:base + MLA_HEAD_PAD]
        ss = jnp.sum(nope * nope, axis=-1, keepdims=True) + jnp.sum(rp * rp, axis=-1, keepdims=True)
        r = lax.rsqrt(ss / MLA_QK_DIM + NORM_EPS)
        xr = _rope_block(rp * r * g[:, MLA_NOPE:], c_ref[...], s1_ref[...], s2_ref[...])
        q_ref[:, base:base + MLA_NOPE] = (nope * r * g[:, :MLA_NOPE]).astype(q_ref.dtype)
        q_ref[:, base + MLA_NOPE:base + MLA_HEAD_PAD] = xr.astype(q_ref.dtype)


def mla_q_up(cq, w_uq_ext, g_qn_ext, rope_c, rope_s1, rope_s2):
    t, r = cq.shape
    n = w_uq_ext.shape[1]
    tm, heads = 512, 4
    tn = heads * MLA_HEAD_PAD
    tab = pl.BlockSpec((tm, LANES), lambda i, j: (i, 0))
    return pl.pallas_call(
        functools.partial(_q_up_kernel, heads=heads),
        grid=(t // tm, n // tn),
        in_specs=[
            pl.BlockSpec((tm, r), lambda i, j: (i, 0)),
            pl.BlockSpec((r, tn), lambda i, j: (0, j)),
            pl.BlockSpec((1, MLA_HEAD_PAD), lambda i, j: (0, 0)),
            tab, tab, tab,
        ],
        out_specs=pl.BlockSpec((tm, tn), lambda i, j: (i, j)),
        out_shape=jax.ShapeDtypeStruct((t, n), BF16),
        compiler_params=_params("arbitrary", "arbitrary"),
        name="mla_q_up",
    )(cq, w_uq_ext, g_qn_ext, rope_c, rope_s1, rope_s2)


def _kv_up_kernel(ckv_ref, kr_ref, wk_ref, wv_ref, g_ref, c_ref, s1_ref, s2_ref, k_ref, v_ref, *, heads):
    ckv = ckv_ref[...].astype(BF16)
    kn = jnp.dot(ckv, wk_ref[...], preferred_element_type=F32)
    v_ref[...] = jnp.dot(ckv, wv_ref[...], preferred_element_type=F32).astype(v_ref.dtype)
    g = g_ref[...]
    lane = lax.broadcasted_iota(I32, (1, LANES), 1)
    kr = jnp.where(lane < MLA_ROPE, kr_ref[...], 0.0)
    ss_r = jnp.sum(kr * kr, axis=-1, keepdims=True)
    krot = _rope_block(kr * g[:, MLA_NOPE:], c_ref[...], s1_ref[...], s2_ref[...])
    for hh in range(heads):
        nope = kn[:, hh * MLA_NOPE:(hh + 1) * MLA_NOPE]
        r = lax.rsqrt((jnp.sum(nope * nope, axis=-1, keepdims=True) + ss_r) / MLA_QK_DIM + NORM_EPS)
        base = hh * MLA_HEAD_PAD
        k_ref[:, base:base + MLA_NOPE] = (nope * r * g[:, :MLA_NOPE]).astype(k_ref.dtype)
        k_ref[:, base + MLA_NOPE:base + MLA_HEAD_PAD] = (krot * r).astype(k_ref.dtype)


def mla_kv_up(ckv, kr, w_uk, w_uv, g_kn_ext, rope_c, rope_s1, rope_s2):
    t, r = ckv.shape
    tm, heads = 512, 4
    tab = pl.BlockSpec((tm, LANES), lambda i, j: (i, 0))
    return pl.pallas_call(
        functools.partial(_kv_up_kernel, heads=heads),
        grid=(t // tm, MLA_HEADS // heads),
        in_specs=[
            pl.BlockSpec((tm, r), lambda i, j: (i, 0)),
            tab,
            pl.BlockSpec((r, heads * MLA_NOPE), lambda i, j: (0, j)),
            pl.BlockSpec((r, heads * MLA_NOPE), lambda i, j: (0, j)),
            pl.BlockSpec((1, MLA_HEAD_PAD), lambda i, j: (0, 0)),
            tab, tab, tab,
        ],
        out_specs=[
            pl.BlockSpec((tm, heads * MLA_HEAD_PAD), lambda i, j: (i, j)),
            pl.BlockSpec((tm, heads * MLA_NOPE), lambda i, j: (i, j)),
        ],
        out_shape=[
            jax.ShapeDtypeStruct((t, MLA_HEADS * MLA_HEAD_PAD), BF16),
            jax.ShapeDtypeStruct((t, MLA_HEADS * MLA_NOPE), BF16),
        ],
        compiler_params=_params("arbitrary", "arbitrary"),
        name="mla_kv_up",
    )(ckv, kr, w_uk, w_uv, g_kn_ext, rope_c, rope_s1, rope_s2)


ATTN_HEADS_PER_STEP = 4


def _attn_kernel(*refs, nseg):
    q_ref = refs[0]
    k_refs = refs[1:1 + nseg]
    v_refs = refs[1 + nseg:1 + 2 * nseg]
    o_ref = refs[-1]
    scale = MLA_QK_DIM ** -0.5
    nt = (((1,), (1,)), ((), ()))
    for hh in range(ATTN_HEADS_PER_STEP):
        qk_cols = slice(hh * MLA_HEAD_PAD, (hh + 1) * MLA_HEAD_PAD)
        v_cols = slice(hh * MLA_NOPE, (hh + 1) * MLA_NOPE)
        q = q_ref[:, qk_cols]
        s = [lax.dot_general(q, k[:, qk_cols], nt, preferred_element_type=F32) * scale for k in k_refs]
        m = functools.reduce(jnp.maximum, [jnp.max(x, axis=-1, keepdims=True) for x in s])
        e = [jnp.exp(x - m) for x in s]
        inv = 1.0 / functools.reduce(lambda a, b: a + b, [jnp.sum(x, axis=-1, keepdims=True) for x in e])
        o = functools.reduce(
            lambda a, b: a + b,
            [jnp.dot((x * inv).astype(BF16), v[:, v_cols], preferred_element_type=F32) for x, v in zip(e, v_refs)])
        o_ref[:, v_cols] = o.astype(o_ref.dtype)


def mla_attention(q, k, v, *, q_row0, n_batch, s_q, segs):
    tq = 256
    nq = s_q // tq
    nseg = len(segs)
    q_blk0 = q_row0 // tq
    hp = ATTN_HEADS_PER_STEP
    in_specs = [pl.BlockSpec((tq, hp * MLA_HEAD_PAD), lambda b, h, i: (q_blk0 + b * nq + i, h))]
    for row0, length in segs:
        in_specs.append(pl.BlockSpec((length, hp * MLA_HEAD_PAD), lambda b, h, i, o=row0 // length: (o + b, h)))
    for row0, length in segs:
        in_specs.append(pl.BlockSpec((length, hp * MLA_NOPE), lambda b, h, i, o=row0 // length: (o + b, h)))
    return pl.pallas_call(
        functools.partial(_attn_kernel, nseg=nseg),
        grid=(n_batch, MLA_HEADS // hp, nq),
        in_specs=in_specs,
        out_specs=pl.BlockSpec((tq, hp * MLA_NOPE), lambda b, h, i: (b * nq + i, h)),
        out_shape=jax.ShapeDtypeStruct((n_batch * s_q, MLA_HEADS * MLA_NOPE), BF16),
        compiler_params=_params("arbitrary", "arbitrary", "arbitrary"),
        name="mla_attention",
    )(q, *([k] * nseg), *([v] * nseg))


def _rope_tables():
    nf = MLA_ROPE // 4
    inv_freq = jnp.power(ROPE_BASE, -jnp.arange(nf, dtype=F32) / nf)
    tok = jnp.arange(DEC_SEQ)
    row = (tok // GRID_W).astype(F32)[:, None] * inv_freq[None, :]
    col = (tok % GRID_W).astype(F32)[:, None] * inv_freq[None, :]
    ang = jnp.concatenate([row, col], axis=-1)
    cos, sin = jnp.cos(ang), jnp.sin(ang)
    z32 = jnp.zeros_like(cos)
    z64 = jnp.zeros((DEC_SEQ, 64), F32)
    c = jnp.concatenate([cos, cos, z64], axis=-1)
    s1 = jnp.concatenate([-sin, z32, z64], axis=-1)
    s2 = jnp.concatenate([z32, sin, z64], axis=-1)
    ident_c = jnp.concatenate([jnp.ones((1, 64), F32), jnp.zeros((1, 64), F32)], axis=-1)

    def full(lat, ident):
        n_cache = DEC_BATCH * PAST_LEN
        return jnp.concatenate([
            jnp.broadcast_to(ident, (T_CTX, LANES)),
            jnp.tile(lat, (DEC_BATCH, 1)),
            jnp.broadcast_to(ident, (n_cache, LANES)),
        ], axis=0)

    zero = jnp.zeros((1, LANES), F32)
    return full(c, ident_c), full(s1, zero), full(s2, zero)


_ROPE_PERM = np.concatenate([np.arange(0, 16), np.arange(32, 48), np.arange(16, 32), np.arange(48, 64)])


def mla_layer(h, cache_ckv, cache_krope, w_in, g_q, g_kv, w_uq, g_qn, w_uk, w_uv, g_kn):
    d = h.shape[1]
    perm = _ROPE_PERM
    n_lat = MLA_Q_RANK + MLA_KV_RANK
    w_in_ext = jnp.concatenate([w_in[:, :n_lat], w_in[:, n_lat:][:, perm], w_in[:, n_lat:]], axis=1).astype(BF16)
    cq, ckv, kr = mla_in_proj(h, w_in_ext, g_q, g_kv)

    w3 = w_uq.reshape(MLA_Q_RANK, MLA_HEADS, MLA_QK_DIM)
    w_uq_ext = jnp.concatenate([
        w3[:, :, :MLA_NOPE], w3[:, :, MLA_NOPE:][:, :, perm],
        jnp.zeros((MLA_Q_RANK, MLA_HEADS, MLA_HEAD_PAD - MLA_QK_DIM), w_uq.dtype)], axis=-1)
    w_uq_ext = w_uq_ext.reshape(MLA_Q_RANK, MLA_HEADS * MLA_HEAD_PAD).astype(BF16)

    def gain_ext(g):
        return jnp.concatenate([g[:MLA_NOPE], g[MLA_NOPE:][perm],
                                jnp.zeros((MLA_HEAD_PAD - MLA_QK_DIM,), g.dtype)]).reshape(1, MLA_HEAD_PAD)

    rope_c, rope_s1, rope_s2 = _rope_tables()
    q = mla_q_up(cq, w_uq_ext, gain_ext(g_qn), rope_c[:T_ALL], rope_s1[:T_ALL], rope_s2[:T_ALL])

    n_cache = DEC_BATCH * PAST_LEN
    ckv_all = jnp.concatenate([ckv, cache_ckv.reshape(n_cache, MLA_KV_RANK)], axis=0)
    kr_cache = cache_krope.reshape(n_cache, MLA_ROPE)
    kr_cache = jnp.concatenate([kr_cache[:, perm], kr_cache], axis=1)
    kr_all = jnp.concatenate([kr, kr_cache], axis=0)
    k, v = mla_kv_up(ckv_all, kr_all, w_uk.astype(BF16), w_uv.astype(BF16), gain_ext(g_kn),
                     rope_c, rope_s1, rope_s2)

    o_ctx = mla_attention(q, k, v, q_row0=0, n_batch=BATCH, s_q=SEQ, segs=[(0, SEQ)])
    o_lat = mla_attention(q, k, v, q_row0=T_CTX, n_batch=DEC_BATCH, s_q=DEC_SEQ,
                          segs=[(T_ALL, PAST_LEN), (T_CTX, DEC_SEQ)])
    y = (o_ctx, o_lat)
    new_ckv = ckv[:T_CTX].reshape(BATCH, 1, SEQ, MLA_KV_RANK)
    new_krope = kr[:T_CTX, MLA_ROPE:].reshape(BATCH, 1, SEQ, MLA_ROPE)
    return y, new_ckv, new_krope


def _log_sigmoid(x):
    return jnp.minimum(x, 0.0) - jnp.log(1.0 + jnp.exp(-jnp.abs(x)))


def _gates_kernel(h_ref, wg_ref, wgt_ref, b_ref, bt_ref, gc_ref, gr_ref):
    h = h_ref[...]
    gc = jnp.dot(h, wg_ref[...], preferred_element_type=F32) + b_ref[...]
    gr = lax.dot_general(wgt_ref[...], h, (((1,), (1,)), ((), ())), preferred_element_type=F32) + bt_ref[...]
    lane = lax.broadcasted_iota(I32, (1, LANES), 1)
    is_f = ((lane >= ML_HEADS) & (lane < 2 * ML_HEADS)) | ((lane >= 3 * ML_HEADS) & (lane < 4 * ML_HEADS))
    gc_ref[...] = jnp.where(is_f, _log_sigmoid(gc), gc)
    row = lax.broadcasted_iota(I32, (4 * ML_HEADS, 1), 0)
    is_fr = ((row >= ML_HEADS) & (row < 2 * ML_HEADS)) | ((row >= 3 * ML_HEADS) & (row < 4 * ML_HEADS))
    gr_ref[...] = jnp.where(is_fr, _log_sigmoid(gr), gr)


def mlstm_gates(h, w_g, b_gate):
    t, d = h.shape
    ng = 4 * ML_HEADS
    tm = 512
    wg = jnp.concatenate([w_g, jnp.zeros((d, LANES - ng), w_g.dtype)], axis=1).astype(BF16)
    wgt = w_g.T.astype(BF16)
    b = jnp.concatenate([b_gate, jnp.zeros((LANES - ng,), b_gate.dtype)]).reshape(1, LANES)
    bt = b_gate.reshape(ng, 1)
    return pl.pallas_call(
        _gates_kernel,
        grid=(t // tm,),
        in_specs=[
            pl.BlockSpec((tm, d), lambda i: (i, 0)),
            pl.BlockSpec((d, LANES), lambda i: (0, 0)),
            pl.BlockSpec((ng, d), lambda i: (0, 0)),
            pl.BlockSpec((1, LANES), lambda i: (0, 0)),
            pl.BlockSpec((ng, 1), lambda i: (0, 0)),
        ],
        out_specs=[pl.BlockSpec((tm, LANES), lambda i: (i, 0)), pl.BlockSpec((ng, tm), lambda i: (0, i))],
        out_shape=[jax.ShapeDtypeStruct((t, LANES), F32), jax.ShapeDtypeStruct((ng, t), F32)],
        compiler_params=_params("arbitrary"),
        name="mlstm_gates",
    )(h, wg, wgt, b, bt)


ML_HEADS_PER_STEP = 2


def _mlstm_kernel(*refs, nc, has_init, emit_state):
    refs = list(refs)
    q_ref, k_ref, v_ref, o_ref, gc_ref, gr_ref, gh_ref = [refs.pop(0) for _ in range(7)]
    if has_init:
        c0_ref, n0_ref, m0_ref = [refs.pop(0) for _ in range(3)]
    y_ref = refs.pop(0)
    if emit_state:
        cf_ref, nf_ref, mf_ref = [refs.pop(0) for _ in range(3)]
    n_chain = 2 * ML_HEADS_PER_STEP
    mem_s, nrm_s, m_s, hs_s = (refs[j * n_chain:(j + 1) * n_chain] for j in range(4))

    L = ML_CHUNK
    tt = lax.broadcasted_iota(I32, (L, L), 0)
    ss = lax.broadcasted_iota(I32, (L, L), 1)
    q_scale = ML_DK ** -0.5
    nt = (((1,), (1,)), ((), ()))
    tn = (((0,), (0,)), ((), ()))

    for hh in range(ML_HEADS_PER_STEP):
        for d in range(2):
            ch = 2 * hh + d
            if has_init:
                mem_s[ch][...] = c0_ref[0, d, hh]
                nrm_s[ch][...] = n0_ref[0, d, hh]
                m_s[ch][...] = m0_ref[0, d, hh]
            else:
                mem_s[ch][...] = jnp.zeros(mem_s[ch].shape, F32)
                nrm_s[ch][...] = jnp.zeros(nrm_s[ch].shape, F32)
                m_s[ch][...] = jnp.zeros(m_s[ch].shape, F32)

    def chunk(hh, d, c):
        ch = 2 * hh + d
        kq = slice(hh * ML_DK, (hh + 1) * ML_DK)
        kv = slice(hh * ML_DV, (hh + 1) * ML_DV)
        causal = (ss <= tt) if d == 0 else (ss >= tt)
        causal_t = (tt <= ss) if d == 0 else (tt >= ss)
        last = L - 1 if d == 0 else 0
        gcol = gc_ref[0, hh, c]
        grow = gr_ref[0, hh, c]
        i_col = gcol[:, 2 * d:2 * d + 1]
        f_col = gcol[:, 2 * d + 1:2 * d + 2]
        i_row = grow[2 * d:2 * d + 1, :]
        f_row = grow[2 * d + 1:2 * d + 2, :]
        cum_col = jnp.sum(jnp.where(causal, jnp.broadcast_to(f_row, (L, L)), 0.0), axis=1, keepdims=True)
        cum_row = jnp.sum(jnp.where(causal_t, jnp.broadcast_to(f_col, (L, L)), 0.0), axis=0, keepdims=True)
        total = cum_col[last:last + 1, :]
        m_prev = m_s[ch][:, 0:1]
        dmat = jnp.where(causal, cum_col - cum_row + i_row, NEG_INF)
        inter = cum_col + m_prev
        m_t = jnp.maximum(inter, jnp.max(dmat, axis=1, keepdims=True))
        w_inter = jnp.exp(inter - m_t)
        rows = pl.ds(pl.multiple_of(c * L, L), L)
        qf = q_ref[rows, kq] * q_scale
        kf = k_ref[rows, kq]
        vb = v_ref[rows, kv].astype(BF16)
        qb = qf.astype(BF16)
        qk = lax.dot_general(qb, kf.astype(BF16), nt, preferred_element_type=F32)
        a = jnp.exp(dmat - m_t) * qk
        mem = mem_s[ch][...]
        nrm = nrm_s[ch][...]
        num = (w_inter * jnp.dot(qb, mem.astype(BF16), preferred_element_type=F32)
               + jnp.dot(a.astype(BF16), vb, preferred_element_type=F32))
        den = w_inter * jnp.sum(qf * nrm, axis=1, keepdims=True) + jnp.sum(a, axis=1, keepdims=True)
        hs_s[ch][rows, :] = num / jnp.maximum(jnp.abs(den), jnp.exp(-m_t))
        m_new = m_t[last:last + 1, :]
        decay = jnp.exp(total + m_prev - m_new)
        w_s = jnp.exp(total - cum_col + i_col - m_new)
        wk = w_s * kf
        mem_s[ch][...] = decay * mem + lax.dot_general(wk.astype(BF16), vb, tn, preferred_element_type=F32)
        nrm_s[ch][...] = decay * nrm + jnp.sum(wk, axis=0, keepdims=True)
        m_s[ch][...] = jnp.broadcast_to(m_new, m_s[ch].shape)

    def all_chains(ci, carry):
        for hh in range(ML_HEADS_PER_STEP):
            chunk(hh, 0, ci)
            chunk(hh, 1, nc - 1 - ci)
        return carry

    lax.fori_loop(0, nc, all_chains, 0)
    for hh in range(ML_HEADS_PER_STEP):
        if emit_state:
            for d in range(2):
                cf_ref[0, d, hh] = mem_s[2 * hh + d][...]
                nf_ref[0, d, hh] = nrm_s[2 * hh + d][...]
                mf_ref[0, d, hh] = m_s[2 * hh + d][...]
        kv = slice(hh * ML_DV, (hh + 1) * ML_DV)
        hs = hs_s[2 * hh][...] + hs_s[2 * hh + 1][...]
        hn = hs * lax.rsqrt(jnp.mean(hs * hs, axis=-1, keepdims=True) + NORM_EPS) * gh_ref[:, kv]
        y_ref[:, kv] = (hn * jax.nn.sigmoid(o_ref[:, kv])).astype(y_ref.dtype)


def mlstm_scan(p, gcol, grow, g_h, *, row0, n_batch, seq, state=None, emit_state=False):
    nc = seq // ML_CHUNK
    rb0 = row0 // seq
    hps = ML_HEADS_PER_STEP
    groups = ML_HEADS // hps
    has_init = state is not None
    in_specs = [
        pl.BlockSpec((seq, hps * ML_DK), lambda b, h: (rb0 + b, h)),
        pl.BlockSpec((seq, hps * ML_DK), lambda b, h: (rb0 + b, groups + h)),
        pl.BlockSpec((seq, hps * ML_DV), lambda b, h: (rb0 + b, groups + h)),
        pl.BlockSpec((seq, hps * ML_DV), lambda b, h: (rb0 + b, 2 * groups + h)),
        pl.BlockSpec((1, hps, nc, ML_CHUNK, 4), lambda b, h: (b, h, 0, 0, 0)),
        pl.BlockSpec((1, hps, nc, 4, ML_CHUNK), lambda b, h: (b, h, 0, 0, 0)),
        pl.BlockSpec((1, hps * ML_DV), lambda b, h: (0, h)),
    ]
    args = [p, p, p, p, gcol, grow, g_h.reshape(1, -1)]
    c_spec = pl.BlockSpec((1, 2, hps, ML_DK, ML_DV), lambda b, h: (b, 0, h, 0, 0))
    n_spec = pl.BlockSpec((1, 2, hps, 1, ML_DK), lambda b, h: (b, 0, h, 0, 0))
    m_spec = pl.BlockSpec((1, 2, hps, 1, LANES), lambda b, h: (b, 0, h, 0, 0))
    if has_init:
        in_specs += [c_spec, n_spec, m_spec]
        args += list(state)
    out_specs = [pl.BlockSpec((seq, hps * ML_DV), lambda b, h: (b, h))]
    out_shape = [jax.ShapeDtypeStruct((n_batch * seq, ML_HEADS * ML_DV), BF16)]
    if emit_state:
        out_specs += [c_spec, n_spec, m_spec]
        out_shape += [
            jax.ShapeDtypeStruct((n_batch, 2, ML_HEADS, ML_DK, ML_DV), F32),
            jax.ShapeDtypeStruct((n_batch, 2, ML_HEADS, 1, ML_DK), F32),
            jax.ShapeDtypeStruct((n_batch, 2, ML_HEADS, 1, LANES), F32),
        ]
    return pl.pallas_call(
        functools.partial(_mlstm_kernel, nc=nc, has_init=has_init, emit_state=emit_state),
        grid=(n_batch, groups),
        in_specs=in_specs,
        out_specs=out_specs,
        out_shape=out_shape,
        scratch_shapes=(
            [pltpu.VMEM((ML_DK, ML_DV), F32)] * (2 * hps) + [pltpu.VMEM((1, ML_DK), F32)] * (2 * hps)
            + [pltpu.VMEM((1, LANES), F32)] * (2 * hps) + [pltpu.VMEM((seq, ML_DV), F32)] * (2 * hps)),
        compiler_params=_params("arbitrary", "arbitrary"),
        name="mlstm_scan",
    )(*args)


def _gate_layouts(gc, gr, row0, n_batch, seq):
    nc = seq // ML_CHUNK
    n = n_batch * seq
    gcol = gc[row0:row0 + n, :4 * ML_HEADS].reshape(n_batch, nc, ML_CHUNK, 4, ML_HEADS).transpose(0, 4, 1, 2, 3)
    grow = gr[:, row0:row0 + n].reshape(4, ML_HEADS, n_batch, nc, ML_CHUNK).transpose(2, 1, 3, 0, 4)
    return gcol, grow


def mlstm_layer(h, state_c, state_n, state_m, w_in, b_gate, g_h):
    hk = ML_HEADS * ML_DK
    hv = ML_HEADS * ML_DV
    n_main = 2 * hk + 2 * hv
    p = matmul(h, w_in.astype(BF16), tm=1024, tn=1024 + 512, out_dtype=F32, n_cols=n_main)
    gc, gr = mlstm_gates(h, w_in[:, n_main:], b_gate)

    gcol, grow = _gate_layouts(gc, gr, 0, BATCH, SEQ)
    y_ctx, cf, nf, mf = mlstm_scan(p, gcol, grow, g_h, row0=0, n_batch=BATCH, seq=SEQ, emit_state=True)

    gcol, grow = _gate_layouts(gc, gr, T_CTX, DEC_BATCH, DEC_SEQ)
    c0 = state_c[:, 0]
    n0 = state_n[:, 0].reshape(DEC_BATCH, 2, ML_HEADS, 1, ML_DK)
    m0 = jnp.broadcast_to(state_m[:, 0].reshape(DEC_BATCH, 2, ML_HEADS, 1, 1), (DEC_BATCH, 2, ML_HEADS, 1, LANES))
    (y_lat,) = mlstm_scan(p, gcol, grow, g_h, row0=T_CTX, n_batch=DEC_BATCH, seq=DEC_SEQ, state=(c0, n0, m0))

    y = (y_ctx, y_lat)
    new_c = cf.reshape(BATCH, 1, 2, ML_HEADS, ML_DK, ML_DV)
    new_n = nf.reshape(BATCH, 1, 2, ML_HEADS, ML_DK)
    new_m = mf[..., 0, 0].reshape(BATCH, 1, 2, ML_HEADS)
    return y, new_c, new_n, new_m


def _topk_rows(x, payload, n_out):
    rows = x.shape[0]
    iota = lax.broadcasted_iota(I32, x.shape, 0)
    vals, outs = [], []
    for _ in range(n_out):
        m = jnp.max(x, axis=0, keepdims=True)
        pos = jnp.min(jnp.where(x == m, iota, rows), axis=0, keepdims=True)
        sel = iota == pos
        vals.append(m)
        outs.append(pos if payload is None else jnp.max(jnp.where(sel, payload, -1), axis=0, keepdims=True))
        x = jnp.where(sel, NEG_INF, x)
    return jnp.concatenate(vals, axis=0), jnp.concatenate(outs, axis=0)


def _pair_candidates(a, b, combine, fill):
    k = PEER_TOPK
    row = lax.broadcasted_iota(I32, (SUBLANES, a.shape[1]), 0)
    blocks = [combine(a[0:1, :], b), combine(a[1:2, :], b[0:SUBLANES, :])]
    for i in range(2, SUBLANES):
        blocks.append(jnp.where(row < k // (i + 1), combine(a[i:i + 1, :], b[0:SUBLANES, :]), fill))
    blocks.append(combine(a[SUBLANES:k, :], b[0:1, :]))
    return jnp.concatenate(blocks, axis=0)


PEER_TILE = 128


def _peer_head_topk(qh, keys_ref):
    half = PEER_QDIM // 2
    nt = (((1,), (1,)), ((), ()))
    sv, si = [], []
    for p in range(2):
        qs = qh[:, p * half:(p + 1) * half].astype(BF16)
        sc = lax.dot_general(keys_ref[p], qs, nt, preferred_element_type=F32)
        v, i = _topk_rows(sc, None, PEER_TOPK)
        sv.append(v)
        si.append(i)
    cand = _pair_candidates(sv[0], sv[1], lambda x, y: x + y, NEG_INF)
    cidx = _pair_candidates(si[0], si[1], lambda x, y: x * PEER_NKEYS + y, -1)
    best, eidx = _topk_rows(cand, cidx, PEER_TOPK)
    ex = jnp.exp(best - best[0:1, :])
    return eidx, ex / jnp.sum(ex, axis=0, keepdims=True)


def _peer_topk_kernel(q_ref, keys_ref, e_ref, gw_ref):
    parts = [_peer_head_topk(q_ref[hd], keys_ref) for hd in range(PEER_HEADS)]
    e_ref[...] = jnp.concatenate([p[0] for p in parts], axis=0).T
    gw_ref[...] = jnp.concatenate([p[1] for p in parts], axis=0).T


def peer_topk(q3, keys, n_tokens):
    tt = PEER_TILE
    return pl.pallas_call(
        _peer_topk_kernel,
        grid=(n_tokens // tt,),
        in_specs=[pl.BlockSpec((PEER_HEADS, tt, PEER_QDIM), lambda i: (0, i, 0)),
                  pl.BlockSpec(keys.shape, lambda i: (0, 0, 0))],
        out_specs=[pl.BlockSpec((tt, PEER_SEL), lambda i: (i, 0)), pl.BlockSpec((tt, PEER_SEL), lambda i: (i, 0))],
        out_shape=[jax.ShapeDtypeStruct((n_tokens, PEER_SEL), I32), jax.ShapeDtypeStruct((n_tokens, PEER_SEL), F32)],
        compiler_params=_params("arbitrary"),
        name="peer_topk",
    )(q3, keys)


def _peer_query_kernel(x_ref, w_ref, o_ref):
    acc = jnp.dot(x_ref[...], w_ref[...], preferred_element_type=F32)
    for hd in range(PEER_HEADS):
        o_ref[hd] = acc[:, hd * PEER_QDIM:(hd + 1) * PEER_QDIM]


def peer_query(h, w_q):
    t, d = h.shape
    tm = min(1024, t)
    return pl.pallas_call(
        _peer_query_kernel,
        grid=(t // tm,),
        in_specs=[pl.BlockSpec((tm, d), lambda i: (i, 0)), pl.BlockSpec(w_q.shape, lambda i: (0, 0))],
        out_specs=pl.BlockSpec((PEER_HEADS, tm, PEER_QDIM), lambda i: (0, i, 0)),
        out_shape=jax.ShapeDtypeStruct((PEER_HEADS, t, PEER_QDIM), F32),
        compiler_params=_params("arbitrary"),
        name="peer_query",
    )(h, w_q)


PEER_STEP_TOK = 16
PEER_SLABS = D_MODEL // LANES
PEER_ROWS = PEER_SEL * PEER_SLABS
_ERF_GELU_C = 0.7071067811865476


def _peer_apply_kernel(e_ref, en_ref, h_ref, gw_ref, expand_ref, expand_t_ref, uv_ref, o_ref,
                       buf_a, buf_b, zs, wexp, sem, *, layer):
    i = pl.program_id(0)
    n_steps = pl.num_programs(0)
    ns = PEER_SLABS
    tt = PEER_STEP_TOK
    nt = (((1,), (1,)), ((), ()))

    def gather_copy(idx, buf, n, s):
        return pltpu.make_async_copy(uv_ref.at[layer, idx], buf.at[n], sem.at[s])

    def issue(idx_ref, buf, s):
        for tok in range(tt):
            for k in range(PEER_SEL):
                gather_copy(idx_ref[tok, k], buf, tok * PEER_SEL + k, s).start(priority=k % 2)

    def wait(buf, s):
        pltpu.make_async_copy(buf, buf, sem.at[s]).wait()

    def compute(buf):
        sub = lax.broadcasted_iota(I32, (ns, PEER_ROWS), 0)
        col = lax.broadcasted_iota(I32, (ns, PEER_ROWS), 1)
        diag = (col % ns) == sub
        for t in range(tt):
            u_t = buf[pl.ds(t * PEER_SEL, PEER_SEL), 0].reshape(PEER_ROWS, LANES)
            y = lax.dot_general(h_ref[t], u_t, nt, preferred_element_type=F32)
            zs[pl.ds(t, 1), :] = jnp.sum(jnp.where(diag, y, 0.0), axis=0, keepdims=True)
        z = zs[...]
        z_hi = z.astype(BF16)
        z_lo = (z - z_hi.astype(F32)).astype(BF16)
        act = (jnp.dot(z_hi, expand_t_ref[...], preferred_element_type=F32)
               + jnp.dot(z_lo, expand_t_ref[...], preferred_element_type=F32))
        gelu = 0.5 * act * (1.0 + lax.erf(act * _ERF_GELU_C))
        w = (gw_ref[...] * gelu).astype(BF16)
        wexp[...] = jnp.dot(w, expand_ref[...], preferred_element_type=F32)
        for t in range(tt):
            wrow = wexp[pl.ds(t, 1), :]
            wbig = jnp.where(diag, jnp.broadcast_to(wrow, (ns, PEER_ROWS)), 0.0).astype(BF16)
            v_t = buf[pl.ds(t * PEER_SEL, PEER_SEL), 1].reshape(PEER_ROWS, LANES)
            o = jnp.dot(wbig, v_t, preferred_element_type=F32)
            for s in range(ns):
                o_ref[t:t + 1, s * LANES:(s + 1) * LANES] = o[s:s + 1, :]

    def step(cur, s_cur, nxt, s_nxt):
        wait(cur, s_cur)
        issue(en_ref, nxt, s_nxt)
        compute(cur)

        @pl.when(i == n_steps - 1)
        def _():
            wait(nxt, s_nxt)

    @pl.when(i == 0)
    def _():
        issue(e_ref, buf_a, 0)

    @pl.when(i % 2 == 0)
    def _():
        step(buf_a, 0, buf_b, 1)

    @pl.when(i % 2 == 1)
    def _():
        step(buf_b, 1, buf_a, 0)


def peer_apply(e, h, gw, uv, layer):
    t, d = h.shape
    ns = PEER_SLABS
    tt = PEER_STEP_TOK
    n_steps = t // tt
    h3 = h.reshape(t, ns, LANES)
    group = np.arange(PEER_ROWS) // ns
    expand = jnp.asarray(group[None, :] == np.arange(PEER_SEL)[:, None], BF16)
    out = pl.pallas_call(
        functools.partial(_peer_apply_kernel, layer=layer),
        grid=(n_steps,),
        in_specs=[
            pl.BlockSpec((tt, PEER_SEL), lambda i: (i, 0), memory_space=pltpu.SMEM),
            pl.BlockSpec((tt, PEER_SEL), lambda i: (jnp.minimum(i + 1, n_steps - 1), 0), memory_space=pltpu.SMEM),
            pl.BlockSpec((tt, ns, LANES), lambda i: (i, 0, 0)),
            pl.BlockSpec((tt, PEER_SEL), lambda i: (i, 0)),
            pl.BlockSpec((PEER_SEL, PEER_ROWS), lambda i: (0, 0)),
            pl.BlockSpec((PEER_ROWS, PEER_SEL), lambda i: (0, 0)),
            pl.BlockSpec(memory_space=pl.ANY),
        ],
        out_specs=pl.BlockSpec((tt, d), lambda i: (i, 0)),
        out_shape=jax.ShapeDtypeStruct((t, d), F32),
        scratch_shapes=[
            pltpu.VMEM((tt * PEER_SEL, 2, ns, LANES), BF16),
            pltpu.VMEM((tt * PEER_SEL, 2, ns, LANES), BF16),
            pltpu.VMEM((tt, PEER_ROWS), F32),
            pltpu.VMEM((tt, PEER_ROWS), F32),
            pltpu.SemaphoreType.DMA((2,)),
        ],
        compiler_params=pltpu.CompilerParams(dimension_semantics=("arbitrary",), vmem_limit_bytes=VMEM_LIMIT_BYTES,
                                             disable_bounds_checks=True),
        name="peer_apply",
    )(e, e, h3, gw, expand, expand.T, uv)
    return out


def peer_tables(peer_u, peer_v):
    depth, n_exp, _ = peer_u.shape
    shape = (depth, n_exp, PEER_SLABS, LANES)
    return jnp.stack([peer_u.astype(BF16).reshape(shape), peer_v.astype(BF16).reshape(shape)], axis=2)


def peer_layer(h, w_q, keys, uv, layer):
    q3 = peer_query(h, w_q.astype(BF16))
    e, gw = peer_topk(q3, keys.astype(BF16), h.shape[0])
    return peer_apply(e, h, gw, uv, layer)


def kernel(x_prompt, x_sample, cache_ckv, cache_krope, state_C, state_n, state_m, c, c_ctx, mod_w, mod_b, norm_mix, norm_ffn, mla_w_in, mla_g_q, mla_g_kv, mla_w_uq, mla_g_qn, mla_w_uk, mla_w_uv, mla_g_kn, mla_w_o, ml_w_in, ml_b_gate, ml_g_h, ml_w_o, peer_w_q, peer_keys, peer_u, peer_v):
    d = D_MODEL
    x = (x_prompt.reshape(T_CTX, d), x_sample.reshape(T_LAT, d))
    cvec = jnp.concatenate([c_ctx.reshape(1, d), c, jnp.zeros((GROUP_PAD - N_GROUPS, d), c.dtype)], axis=0)
    mod = mod_vectors(cvec, mod_w, mod_b)
    modrows = mod.reshape(mod.shape[0] * GROUP_PAD * N_MOD, 1, d)
    uv = peer_tables(peer_u, peer_v)

    (h,) = resid_modulate(x, modrows, norm_g=norm_mix[0], shift=(0, 0), scale=(0, 1))
    y, new_ckv, new_krope = mla_layer(h, cache_ckv[:, 0], cache_krope[:, 0], mla_w_in[0], mla_g_q[0], mla_g_kv[0],
                                      mla_w_uq[0], mla_g_qn[0], mla_w_uk[0], mla_w_uv[0], mla_g_kn[0])
    x, h = resid_modulate(x, modrows, y=y, w=mla_w_o[0].astype(BF16), gate=(0, 2),
                          norm_g=norm_ffn[0], shift=(0, 3), scale=(0, 4))
    y = peer_layer(h, peer_w_q[0], peer_keys[0], uv, 0)

    x, h = resid_modulate(x, modrows, y=y, gate=(0, 5), norm_g=norm_mix[1], shift=(1, 0), scale=(1, 1))
    y, new_c, new_n, new_m = mlstm_layer(h, state_C, state_n, state_m, ml_w_in[0], ml_b_gate[0], ml_g_h[0])
    x, h = resid_modulate(x, modrows, y=y, w=ml_w_o[0].astype(BF16), gate=(1, 2),
                          norm_g=norm_ffn[1], shift=(1, 3), scale=(1, 4))
    y = peer_layer(h, peer_w_q[1], peer_keys[1], uv, 1)
    y_prompt, y_sample = resid_modulate(x, modrows, y=y, gate=(1, 5), split_out=True)

    return (y_prompt.reshape(BATCH, SEQ, d), y_sample.reshape(DEC_BATCH, DEC_SEQ, d),
            new_ckv, new_krope, new_c, new_n, new_m)
```

```python
import functools

import numpy as np
import jax
import jax.numpy as jnp
from jax import lax
from jax.experimental import pallas as pl
from jax.experimental.pallas import tpu as pltpu

F32 = jnp.float32
BF16 = jnp.bfloat16
I32 = jnp.int32

D_MODEL = 2048
BATCH, SEQ = 32, 256
DEC_BATCH, DEC_SEQ = 8, 1024
PAST_LEN = 512
GRID_W = 64
N_MOD = 6
NORM_EPS = 1e-6
MLA_HEADS = 16
MLA_Q_RANK = 512
MLA_KV_RANK = 512
MLA_NOPE = 128
MLA_ROPE = 64
MLA_QK_DIM = MLA_NOPE + MLA_ROPE
MLA_HEAD_PAD = 256
ROPE_BASE = 10000.0
ML_HEADS = 8
ML_DV = D_MODEL // ML_HEADS
ML_DK = ML_DV // 2
ML_CHUNK = 64
PEER_HEADS = 8
PEER_NKEYS = 128
PEER_QDIM = 128
PEER_TOPK = 16
PEER_SEL = PEER_HEADS * PEER_TOPK

T_CTX = BATCH * SEQ
T_LAT = DEC_BATCH * DEC_SEQ
T_ALL = T_CTX + T_LAT
N_GROUPS = 1 + DEC_BATCH
GROUP_PAD = 16
ROW_TILE = 256

VMEM_LIMIT_BYTES = 56 * 1024 * 1024
LANES = 128
SUBLANES = 8

NEG_INF = float("-inf")


def _params(*sem):
    return pltpu.CompilerParams(dimension_semantics=sem, vmem_limit_bytes=VMEM_LIMIT_BYTES)


def _mod_kernel(c_ref, w_ref, b_ref, o_ref):
    c = c_ref[...]
    a = (c * jax.nn.sigmoid(c)).astype(BF16)
    o_ref[0] = jnp.dot(a, w_ref[0].astype(BF16), preferred_element_type=F32) + b_ref[0]


def mod_vectors(cvec, mod_w, mod_b):
    depth, d, n = mod_w.shape
    tn = 1024
    return pl.pallas_call(
        _mod_kernel,
        grid=(depth, n // tn),
        in_specs=[
            pl.BlockSpec((GROUP_PAD, d), lambda l, j: (0, 0)),
            pl.BlockSpec((1, d, tn), lambda l, j: (l, 0, j)),
            pl.BlockSpec((1, 1, tn), lambda l, j: (l, 0, j)),
        ],
        out_specs=pl.BlockSpec((1, GROUP_PAD, tn), lambda l, j: (l, 0, j)),
        out_shape=jax.ShapeDtypeStruct((depth, GROUP_PAD, n), F32),
        compiler_params=_params("arbitrary", "arbitrary"),
        name="mod_vectors",
    )(cvec, mod_w, mod_b.reshape(depth, 1, n))


def _group_of_tile(i):
    ctx_tiles = T_CTX // ROW_TILE
    tiles_per_lat = DEC_SEQ // ROW_TILE
    return jnp.where(i < ctx_tiles, 0, 1 + (i - ctx_tiles) // tiles_per_lat)


CTX_TILES = T_CTX // ROW_TILE


def _row_tile(parts, i):
    if len(parts) == 1:
        return parts[0][...]
    return jnp.where(i < CTX_TILES, parts[0][...], parts[1][...])


def _resmod_kernel(*refs, nx, ny, has_w, has_mod, split_out):
    i = pl.program_id(0)
    refs = list(refs)
    x = _row_tile([refs.pop(0) for _ in range(nx)], i)
    if ny:
        y = _row_tile([refs.pop(0) for _ in range(ny)], i)
        if has_w:
            y = jnp.dot(y, refs.pop(0)[...], preferred_element_type=F32)
        gate_ref = refs.pop(0)
        x = x + gate_ref[0] * y
    if has_mod:
        g_ref, sh_ref, sc_ref = refs.pop(0), refs.pop(0), refs.pop(0)
    if ny:
        if split_out:
            xa_ref, xb_ref = refs.pop(0), refs.pop(0)

            @pl.when(i < CTX_TILES)
            def _():
                xa_ref[...] = x

            @pl.when(i >= CTX_TILES)
            def _():
                xb_ref[...] = x
        else:
            refs.pop(0)[...] = x
    if has_mod:
        h_ref = refs.pop(0)
        ms = jnp.mean(x * x, axis=-1, keepdims=True)
        yn = x * lax.rsqrt(ms + NORM_EPS) * g_ref[...]
        h_ref[...] = (yn * (1.0 + sc_ref[0]) + sh_ref[0]).astype(h_ref.dtype)


def resid_modulate(x, modrows, *, y=None, w=None, gate=None, norm_g=None, shift=None, scale=None, split_out=False):
    xs = list(x) if isinstance(x, (tuple, list)) else [x]
    ys = [] if y is None else (list(y) if isinstance(y, (tuple, list)) else [y])
    d = xs[0].shape[1]
    t = sum(a.shape[0] for a in xs)
    has_mod = norm_g is not None
    row_spec = pl.BlockSpec((ROW_TILE, d), lambda i: (i, 0))
    ctx_spec = pl.BlockSpec((ROW_TILE, d), lambda i: (jnp.minimum(i, CTX_TILES - 1), 0))
    lat_spec = pl.BlockSpec((ROW_TILE, d), lambda i: (jnp.maximum(i - CTX_TILES, 0), 0))

    def part_specs(parts):
        return [row_spec] if len(parts) == 1 else [ctx_spec, lat_spec]

    def mod_spec(layer_k):
        layer, k = layer_k
        return pl.BlockSpec((1, 1, d), lambda i: ((layer * GROUP_PAD + _group_of_tile(i)) * N_MOD + k, 0, 0))

    args, in_specs = list(xs), part_specs(xs)
    if ys:
        args += ys
        in_specs += part_specs(ys)
        if w is not None:
            args.append(w)
            in_specs.append(pl.BlockSpec(w.shape, lambda i: (0, 0)))
        args.append(modrows)
        in_specs.append(mod_spec(gate))
    if has_mod:
        args += [norm_g.reshape(1, d), modrows, modrows]
        in_specs += [pl.BlockSpec((1, d), lambda i: (0, 0)), mod_spec(shift), mod_spec(scale)]
    out_shape, out_specs = [], []
    if ys:
        if split_out:
            out_shape += [jax.ShapeDtypeStruct((T_CTX, d), F32), jax.ShapeDtypeStruct((t - T_CTX, d), F32)]
            out_specs += [ctx_spec, lat_spec]
        else:
            out_shape.append(jax.ShapeDtypeStruct((t, d), F32))
            out_specs.append(row_spec)
    if has_mod:
        out_shape.append(jax.ShapeDtypeStruct((t, d), BF16))
        out_specs.append(row_spec)
    outs = pl.pallas_call(
        functools.partial(_resmod_kernel, nx=len(xs), ny=len(ys), has_w=w is not None, has_mod=has_mod,
                          split_out=split_out),
        grid=(t // ROW_TILE,),
        in_specs=in_specs,
        out_specs=out_specs,
        out_shape=out_shape,
        compiler_params=_params("arbitrary"),
        name="resid_modulate",
    )(*args)
    return outs


def _mm_kernel(x_ref, w_ref, o_ref):
    o_ref[...] = jnp.dot(x_ref[...].astype(BF16), w_ref[...], preferred_element_type=F32).astype(o_ref.dtype)


def matmul(x, w, *, tm, tn, out_dtype, n_cols=None):
    m, k = x.shape
    n = w.shape[1] if n_cols is None else n_cols
    return pl.pallas_call(
        _mm_kernel,
        grid=(m // tm, n // tn),
        in_specs=[pl.BlockSpec((tm, k), lambda i, j: (i, 0)), pl.BlockSpec((k, tn), lambda i, j: (0, j))],
        out_specs=pl.BlockSpec((tm, tn), lambda i, j: (i, j)),
        out_shape=jax.ShapeDtypeStruct((m, n), out_dtype),
        compiler_params=_params("arbitrary", "arbitrary"),
        name="matmul",
    )(x, w)


def _rms(x, n):
    return lax.rsqrt(jnp.sum(x * x, axis=-1, keepdims=True) / n + NORM_EPS)


def _mla_in_kernel(h_ref, w_ref, gq_ref, gkv_ref, cq_ref, ckv_ref, kr_ref):
    a = jnp.dot(h_ref[...], w_ref[...], preferred_element_type=F32)
    cq = a[:, :MLA_Q_RANK]
    ckv = a[:, MLA_Q_RANK:MLA_Q_RANK + MLA_KV_RANK]
    cq_ref[...] = (cq * _rms(cq, MLA_Q_RANK) * gq_ref[...]).astype(cq_ref.dtype)
    ckv_ref[...] = ckv * _rms(ckv, MLA_KV_RANK) * gkv_ref[...]
    kr_ref[...] = a[:, MLA_Q_RANK + MLA_KV_RANK:]


def mla_in_proj(h, w_ext, g_q, g_kv):
    t, d = h.shape
    n = w_ext.shape[1]
    tm = 512
    return pl.pallas_call(
        _mla_in_kernel,
        grid=(t // tm,),
        in_specs=[
            pl.BlockSpec((tm, d), lambda i: (i, 0)),
            pl.BlockSpec((d, n), lambda i: (0, 0)),
            pl.BlockSpec((1, MLA_Q_RANK), lambda i: (0, 0)),
            pl.BlockSpec((1, MLA_KV_RANK), lambda i: (0, 0)),
        ],
        out_specs=[
            pl.BlockSpec((tm, MLA_Q_RANK), lambda i: (i, 0)),
            pl.BlockSpec((tm, MLA_KV_RANK), lambda i: (i, 0)),
            pl.BlockSpec((tm, LANES), lambda i: (i, 0)),
        ],
        out_shape=[
            jax.ShapeDtypeStruct((t, MLA_Q_RANK), BF16),
            jax.ShapeDtypeStruct((t, MLA_KV_RANK), F32),
            jax.ShapeDtypeStruct((t, LANES), F32),
        ],
        compiler_params=_params("arbitrary"),
        name="mla_in_proj",
    )(h, w_ext, g_q.reshape(1, -1), g_kv.reshape(1, -1))


def _rope_block(x, c, s1, s2):
    return x * c + pltpu.roll(x, 96, 1) * s1 + pltpu.roll(x, 32, 1) * s2


def _q_up_kernel(cq_ref, w_ref, g_ref, c_ref, s1_ref, s2_ref, q_ref, *, heads):
    a = jnp.dot(cq_ref[...], w_ref[...], preferred_element_type=F32)
    g = g_ref[...]
    for hh in range(heads):
        base = hh * MLA_HEAD_PAD
        nope = a[:, base:base + MLA_NOPE]
        rp = a[:, base + MLA_NOPE:base + MLA_HEAD_PAD]
        ss = jnp.sum(nope * nope, axis=-1, keepdims=True) + jnp.sum(rp * rp, axis=-1, keepdims=True)
        r = lax.rsqrt(ss / MLA_QK_DIM + NORM_EPS)
        xr = _rope_block(rp * r * g[:, MLA_NOPE:], c_ref[...], s1_ref[...], s2_ref[...])
        q_ref[:, base:base + MLA_NOPE] = (nope * r * g[:, :MLA_NOPE]).astype(q_ref.dtype)
        q_ref[:, base + MLA_NOPE:base + MLA_HEAD_PAD] = xr.astype(q_ref.dtype)


def mla_q_up(cq, w_uq_ext, g_qn_ext, rope_c, rope_s1, rope_s2):
    t, r = cq.shape
    n = w_uq_ext.shape[1]
    tm, heads = 512, 4
    tn = heads * MLA_HEAD_PAD
    tab = pl.BlockSpec((tm, LANES), lambda i, j: (i, 0))
    return pl.pallas_call(
        functools.partial(_q_up_kernel, heads=heads),
        grid=(t // tm, n // tn),
        in_specs=[
            pl.BlockSpec((tm, r), lambda i, j: (i, 0)),
            pl.BlockSpec((r, tn), lambda i, j: (0, j)),
            pl.BlockSpec((1, MLA_HEAD_PAD), lambda i, j: (0, 0)),
            tab, tab, tab,
        ],
        out_specs=pl.BlockSpec((tm, tn), lambda i, j: (i, j)),
        out_shape=jax.ShapeDtypeStruct((t, n), BF16),
        compiler_params=_params("arbitrary", "arbitrary"),
        name="mla_q_up",
    )(cq, w_uq_ext, g_qn_ext, rope_c, rope_s1, rope_s2)


def _kv_up_kernel(ckv_ref, kr_ref, wk_ref, wv_ref, g_ref, c_ref, s1_ref, s2_ref, k_ref, v_ref, *, heads):
    ckv = ckv_ref[...].astype(BF16)
    kn = jnp.dot(ckv, wk_ref[...], preferred_element_type=F32)
    v_ref[...] = jnp.dot(ckv, wv_ref[...], preferred_element_type=F32).astype(v_ref.dtype)
    g = g_ref[...]
    lane = lax.broadcasted_iota(I32, (1, LANES), 1)
    kr = jnp.where(lane < MLA_ROPE, kr_ref[...], 0.0)
    ss_r = jnp.sum(kr * kr, axis=-1, keepdims=True)
    krot = _rope_block(kr * g[:, MLA_NOPE:], c_ref[...], s1_ref[...], s2_ref[...])
    for hh in range(heads):
        nope = kn[:, hh * MLA_NOPE:(hh + 1) * MLA_NOPE]
        r = lax.rsqrt((jnp.sum(nope * nope, axis=-1, keepdims=True) + ss_r) / MLA_QK_DIM + NORM_EPS)
        base = hh * MLA_HEAD_PAD
        k_ref[:, base:base + MLA_NOPE] = (nope * r * g[:, :MLA_NOPE]).astype(k_ref.dtype)
        k_ref[:, base + MLA_NOPE:base + MLA_HEAD_PAD] = (krot * r).astype(k_ref.dtype)


def mla_kv_up(ckv, kr, w_uk, w_uv, g_kn_ext, rope_c, rope_s1, rope_s2):
    t, r = ckv.shape
    tm, heads = 512, 4
    tab = pl.BlockSpec((tm, LANES), lambda i, j: (i, 0))
    return pl.pallas_call(
        functools.partial(_kv_up_kernel, heads=heads),
        grid=(t // tm, MLA_HEADS // heads),
        in_specs=[
            pl.BlockSpec((tm, r), lambda i, j: (i, 0)),
            tab,
            pl.BlockSpec((r, heads * MLA_NOPE), lambda i, j: (0, j)),
            pl.BlockSpec((r, heads * MLA_NOPE), lambda i, j: (0, j)),
            pl.BlockSpec((1, MLA_HEAD_PAD), lambda i, j: (0, 0)),
            tab, tab, tab,
        ],
        out_specs=[
            pl.BlockSpec((tm, heads * MLA_HEAD_PAD), lambda i, j: (i, j)),
            pl.BlockSpec((tm, heads * MLA_NOPE), lambda i, j: (i, j)),
        ],
        out_shape=[
            jax.ShapeDtypeStruct((t, MLA_HEADS * MLA_HEAD_PAD), BF16),
            jax.ShapeDtypeStruct((t, MLA_HEADS * MLA_NOPE), BF16),
        ],
        compiler_params=_params("arbitrary", "arbitrary"),
        name="mla_kv_up",
    )(ckv, kr, w_uk, w_uv, g_kn_ext, rope_c, rope_s1, rope_s2)


ATTN_HEADS_PER_STEP = 4


def _attn_kernel(*refs, nseg):
    q_ref = refs[0]
    k_refs = refs[1:1 + nseg]
    v_refs = refs[1 + nseg:1 + 2 * nseg]
    o_ref = refs[-1]
    scale = MLA_QK_DIM ** -0.5
    nt = (((1,), (1,)), ((), ()))
    for hh in range(ATTN_HEADS_PER_STEP):
        qk_cols = slice(hh * MLA_HEAD_PAD, (hh + 1) * MLA_HEAD_PAD)
        v_cols = slice(hh * MLA_NOPE, (hh + 1) * MLA_NOPE)
        q = q_ref[:, qk_cols]
        s = [lax.dot_general(q, k[:, qk_cols], nt, preferred_element_type=F32) * scale for k in k_refs]
        m = functools.reduce(jnp.maximum, [jnp.max(x, axis=-1, keepdims=True) for x in s])
        e = [jnp.exp(x - m) for x in s]
        inv = 1.0 / functools.reduce(lambda a, b: a + b, [jnp.sum(x, axis=-1, keepdims=True) for x in e])
        o = functools.reduce(
            lambda a, b: a + b,
            [jnp.dot((x * inv).astype(BF16), v[:, v_cols], preferred_element_type=F32) for x, v in zip(e, v_refs)])
        o_ref[:, v_cols] = o.astype(o_ref.dtype)


def mla_attention(q, k, v, *, q_row0, n_batch, s_q, segs):
    tq = 256
    nq = s_q // tq
    nseg = len(segs)
    q_blk0 = q_row0 // tq
    hp = ATTN_HEADS_PER_STEP
    in_specs = [pl.BlockSpec((tq, hp * MLA_HEAD_PAD), lambda b, h, i: (q_blk0 + b * nq + i, h))]
    for row0, length in segs:
        in_specs.append(pl.BlockSpec((length, hp * MLA_HEAD_PAD), lambda b, h, i, o=row0 // length: (o + b, h)))
    for row0, length in segs:
        in_specs.append(pl.BlockSpec((length, hp * MLA_NOPE), lambda b, h, i, o=row0 // length: (o + b, h)))
    return pl.pallas_call(
        functools.partial(_attn_kernel, nseg=nseg),
        grid=(n_batch, MLA_HEADS // hp, nq),
        in_specs=in_specs,
        out_specs=pl.BlockSpec((tq, hp * MLA_NOPE), lambda b, h, i: (b * nq + i, h)),
        out_shape=jax.ShapeDtypeStruct((n_batch * s_q, MLA_HEADS * MLA_NOPE), BF16),
        compiler_params=_params("arbitrary", "arbitrary", "arbitrary"),
        name="mla_attention",
    )(q, *([k] * nseg), *([v] * nseg))


def _rope_tables():
    nf = MLA_ROPE // 4
    inv_freq = jnp.power(ROPE_BASE, -jnp.arange(nf, dtype=F32) / nf)
    tok = jnp.arange(DEC_SEQ)
    row = (tok // GRID_W).astype(F32)[:, None] * inv_freq[None, :]
    col = (tok % GRID_W).astype(F32)[:, None] * inv_freq[None, :]
    ang = jnp.concatenate([row, col], axis=-1)
    cos, sin = jnp.cos(ang), jnp.sin(ang)
    z32 = jnp.zeros_like(cos)
    z64 = jnp.zeros((DEC_SEQ, 64), F32)
    c = jnp.concatenate([cos, cos, z64], axis=-1)
    s1 = jnp.concatenate([-sin, z32, z64], axis=-1)
    s2 = jnp.concatenate([z32, sin, z64], axis=-1)
    ident_c = jnp.concatenate([jnp.ones((1, 64), F32), jnp.zeros((1, 64), F32)], axis=-1)

    def full(lat, ident):
        n_cache = DEC_BATCH * PAST_LEN
        return jnp.concatenate([
            jnp.broadcast_to(ident, (T_CTX, LANES)),
            jnp.tile(lat, (DEC_BATCH, 1)),
            jnp.broadcast_to(ident, (n_cache, LANES)),
        ], axis=0)

    zero = jnp.zeros((1, LANES), F32)
    return full(c, ident_c), full(s1, zero), full(s2, zero)


_ROPE_PERM = np.concatenate([np.arange(0, 16), np.arange(32, 48), np.arange(16, 32), np.arange(48, 64)])


def mla_layer(h, cache_ckv, cache_krope, w_in, g_q, g_kv, w_uq, g_qn, w_uk, w_uv, g_kn):
    d = h.shape[1]
    perm = _ROPE_PERM
    n_lat = MLA_Q_RANK + MLA_KV_RANK
    w_in_ext = jnp.concatenate([w_in[:, :n_lat], w_in[:, n_lat:][:, perm], w_in[:, n_lat:]], axis=1).astype(BF16)
    cq, ckv, kr = mla_in_proj(h, w_in_ext, g_q, g_kv)

    w3 = w_uq.reshape(MLA_Q_RANK, MLA_HEADS, MLA_QK_DIM)
    w_uq_ext = jnp.concatenate([
        w3[:, :, :MLA_NOPE], w3[:, :, MLA_NOPE:][:, :, perm],
        jnp.zeros((MLA_Q_RANK, MLA_HEADS, MLA_HEAD_PAD - MLA_QK_DIM), w_uq.dtype)], axis=-1)
    w_uq_ext = w_uq_ext.reshape(MLA_Q_RANK, MLA_HEADS * MLA_HEAD_PAD).astype(BF16)

    def gain_ext(g):
        return jnp.concatenate([g[:MLA_NOPE], g[MLA_NOPE:][perm],
                                jnp.zeros((MLA_HEAD_PAD - MLA_QK_DIM,), g.dtype)]).reshape(1, MLA_HEAD_PAD)

    rope_c, rope_s1, rope_s2 = _rope_tables()
    q = mla_q_up(cq, w_uq_ext, gain_ext(g_qn), rope_c[:T_ALL], rope_s1[:T_ALL], rope_s2[:T_ALL])

    n_cache = DEC_BATCH * PAST_LEN
    ckv_all = jnp.concatenate([ckv, cache_ckv.reshape(n_cache, MLA_KV_RANK)], axis=0)
    kr_cache = cache_krope.reshape(n_cache, MLA_ROPE)
    kr_cache = jnp.concatenate([kr_cache[:, perm], kr_cache], axis=1)
    kr_all = jnp.concatenate([kr, kr_cache], axis=0)
    k, v = mla_kv_up(ckv_all, kr_all, w_uk.astype(BF16), w_uv.astype(BF16), gain_ext(g_kn),
                     rope_c, rope_s1, rope_s2)

    o_ctx = mla_attention(q, k, v, q_row0=0, n_batch=BATCH, s_q=SEQ, segs=[(0, SEQ)])
    o_lat = mla_attention(q, k, v, q_row0=T_CTX, n_batch=DEC_BATCH, s_q=DEC_SEQ,
                          segs=[(T_ALL, PAST_LEN), (T_CTX, DEC_SEQ)])
    y = (o_ctx, o_lat)
    new_ckv = ckv[:T_CTX].reshape(BATCH, 1, SEQ, MLA_KV_RANK)
    new_krope = kr[:T_CTX, MLA_ROPE:].reshape(BATCH, 1, SEQ, MLA_ROPE)
    return y, new_ckv, new_krope


def _log_sigmoid(x):
    return jnp.minimum(x, 0.0) - jnp.log(1.0 + jnp.exp(-jnp.abs(x)))


def _gates_kernel(h_ref, wg_ref, wgt_ref, b_ref, bt_ref, gc_ref, gr_ref):
    h = h_ref[...]
    gc = jnp.dot(h, wg_ref[...], preferred_element_type=F32) + b_ref[...]
    gr = lax.dot_general(wgt_ref[...], h, (((1,), (1,)), ((), ())), preferred_element_type=F32) + bt_ref[...]
    lane = lax.broadcasted_iota(I32, (1, LANES), 1)
    is_f = ((lane >= ML_HEADS) & (lane < 2 * ML_HEADS)) | ((lane >= 3 * ML_HEADS) & (lane < 4 * ML_HEADS))
    gc_ref[...] = jnp.where(is_f, _log_sigmoid(gc), gc)
    row = lax.broadcasted_iota(I32, (4 * ML_HEADS, 1), 0)
    is_fr = ((row >= ML_HEADS) & (row < 2 * ML_HEADS)) | ((row >= 3 * ML_HEADS) & (row < 4 * ML_HEADS))
    gr_ref[...] = jnp.where(is_fr, _log_sigmoid(gr), gr)


def mlstm_gates(h, w_g, b_gate):
    t, d = h.shape
    ng = 4 * ML_HEADS
    tm = 512
    wg = jnp.concatenate([w_g, jnp.zeros((d, LANES - ng), w_g.dtype)], axis=1).astype(BF16)
    wgt = w_g.T.astype(BF16)
    b = jnp.concatenate([b_gate, jnp.zeros((LANES - ng,), b_gate.dtype)]).reshape(1, LANES)
    bt = b_gate.reshape(ng, 1)
    return pl.pallas_call(
        _gates_kernel,
        grid=(t // tm,),
        in_specs=[
            pl.BlockSpec((tm, d), lambda i: (i, 0)),
            pl.BlockSpec((d, LANES), lambda i: (0, 0)),
            pl.BlockSpec((ng, d), lambda i: (0, 0)),
            pl.BlockSpec((1, LANES), lambda i: (0, 0)),
            pl.BlockSpec((ng, 1), lambda i: (0, 0)),
        ],
        out_specs=[pl.BlockSpec((tm, LANES), lambda i: (i, 0)), pl.BlockSpec((ng, tm), lambda i: (0, i))],
        out_shape=[jax.ShapeDtypeStruct((t, LANES), F32), jax.ShapeDtypeStruct((ng, t), F32)],
        compiler_params=_params("arbitrary"),
        name="mlstm_gates",
    )(h, wg, wgt, b, bt)


ML_HEADS_PER_STEP = 2


def _mlstm_kernel(*refs, nc, has_init, emit_state):
    refs = list(refs)
    q_ref, k_ref, v_ref, o_ref, gc_ref, gr_ref, gh_ref = [refs.pop(0) for _ in range(7)]
    if has_init:
        c0_ref, n0_ref, m0_ref = [refs.pop(0) for _ in range(3)]
    y_ref = refs.pop(0)
    if emit_state:
        cf_ref, nf_ref, mf_ref = [refs.pop(0) for _ in range(3)]
    n_chain = 2 * ML_HEADS_PER_STEP
    mem_s, nrm_s, m_s, hs_s = (refs[j * n_chain:(j + 1) * n_chain] for j in range(4))

    L = ML_CHUNK
    tt = lax.broadcasted_iota(I32, (L, L), 0)
    ss = lax.broadcasted_iota(I32, (L, L), 1)
    q_scale = ML_DK ** -0.5
    nt = (((1,), (1,)), ((), ()))
    tn = (((0,), (0,)), ((), ()))

    for hh in range(ML_HEADS_PER_STEP):
        for d in range(2):
            ch = 2 * hh + d
            if has_init:
                mem_s[ch][...] = c0_ref[0, d, hh]
                nrm_s[ch][...] = n0_ref[0, d, hh]
                m_s[ch][...] = m0_ref[0, d, hh]
            else:
                mem_s[ch][...] = jnp.zeros(mem_s[ch].shape, F32)
                nrm_s[ch][...] = jnp.zeros(nrm_s[ch].shape, F32)
                m_s[ch][...] = jnp.zeros(m_s[ch].shape, F32)

    def chunk(hh, d, c):
        ch = 2 * hh + d
        kq = slice(hh * ML_DK, (hh + 1) * ML_DK)
        kv = slice(hh * ML_DV, (hh + 1) * ML_DV)
        causal = (ss <= tt) if d == 0 else (ss >= tt)
        causal_t = (tt <= ss) if d == 0 else (tt >= ss)
        last = L - 1 if d == 0 else 0
        gcol = gc_ref[0, hh, c]
        grow = gr_ref[0, hh, c]
        i_col = gcol[:, 2 * d:2 * d + 1]
        f_col = gcol[:, 2 * d + 1:2 * d + 2]
        i_row = grow[2 * d:2 * d + 1, :]
        f_row = grow[2 * d + 1:2 * d + 2, :]
        cum_col = jnp.sum(jnp.where(causal, jnp.broadcast_to(f_row, (L, L)), 0.0), axis=1, keepdims=True)
        cum_row = jnp.sum(jnp.where(causal_t, jnp.broadcast_to(f_col, (L, L)), 0.0), axis=0, keepdims=True)
        total = cum_col[last:last + 1, :]
        m_prev = m_s[ch][:, 0:1]
        dmat = jnp.where(causal, cum_col - cum_row + i_row, NEG_INF)
        inter = cum_col + m_prev
        m_t = jnp.maximum(inter, jnp.max(dmat, axis=1, keepdims=True))
        w_inter = jnp.exp(inter - m_t)
        rows = pl.ds(pl.multiple_of(c * L, L), L)
        qf = q_ref[rows, kq] * q_scale
        kf = k_ref[rows, kq]
        vb = v_ref[rows, kv].astype(BF16)
        qb = qf.astype(BF16)
        qk = lax.dot_general(qb, kf.astype(BF16), nt, preferred_element_type=F32)
        a = jnp.exp(dmat - m_t) * qk
        mem = mem_s[ch][...]
        nrm = nrm_s[ch][...]
        num = (w_inter * jnp.dot(qb, mem.astype(BF16), preferred_element_type=F32)
               + jnp.dot(a.astype(BF16), vb, preferred_element_type=F32))
        den = w_inter * jnp.sum(qf * nrm, axis=1, keepdims=True) + jnp.sum(a, axis=1, keepdims=True)
        hs_s[ch][rows, :] = num / jnp.maximum(jnp.abs(den), jnp.exp(-m_t))
        m_new = m_t[last:last + 1, :]
        decay = jnp.exp(total + m_prev - m_new)
        w_s = jnp.exp(total - cum_col + i_col - m_new)
        wk = w_s * kf
        mem_s[ch][...] = decay * mem + lax.dot_general(wk.astype(BF16), vb, tn, preferred_element_type=F32)
        nrm_s[ch][...] = decay * nrm + jnp.sum(wk, axis=0, keepdims=True)
        m_s[ch][...] = jnp.broadcast_to(m_new, m_s[ch].shape)

    def all_chains(ci, carry):
        for hh in range(ML_HEADS_PER_STEP):
            chunk(hh, 0, ci)
            chunk(hh, 1, nc - 1 - ci)
        return carry

    lax.fori_loop(0, nc, all_chains, 0)
    for hh in range(ML_HEADS_PER_STEP):
        if emit_state:
            for d in range(2):
                cf_ref[0, d, hh] = mem_s[2 * hh + d][...]
                nf_ref[0, d, hh] = nrm_s[2 * hh + d][...]
                mf_ref[0, d, hh] = m_s[2 * hh + d][...]
        kv = slice(hh * ML_DV, (hh + 1) * ML_DV)
        hs = hs_s[2 * hh][...] + hs_s[2 * hh + 1][...]
        hn = hs * lax.rsqrt(jnp.mean(hs * hs, axis=-1, keepdims=True) + NORM_EPS) * gh_ref[:, kv]
        y_ref[:, kv] = (hn * jax.nn.sigmoid(o_ref[:, kv])).astype(y_ref.dtype)


def mlstm_scan(p, gcol, grow, g_h, *, row0, n_batch, seq, state=None, emit_state=False):
    nc = seq // ML_CHUNK
    rb0 = row0 // seq
    hps = ML_HEADS_PER_STEP
    groups = ML_HEADS // hps
    has_init = state is not None
    in_specs = [
        pl.BlockSpec((seq, hps * ML_DK), lambda b, h: (rb0 + b, h)),
        pl.BlockSpec((seq, hps * ML_DK), lambda b, h: (rb0 + b, groups + h)),
        pl.BlockSpec((seq, hps * ML_DV), lambda b, h: (rb0 + b, groups + h)),
        pl.BlockSpec((seq, hps * ML_DV), lambda b, h: (rb0 + b, 2 * groups + h)),
        pl.BlockSpec((1, hps, nc, ML_CHUNK, 4), lambda b, h: (b, h, 0, 0, 0)),
        pl.BlockSpec((1, hps, nc, 4, ML_CHUNK), lambda b, h: (b, h, 0, 0, 0)),
        pl.BlockSpec((1, hps * ML_DV), lambda b, h: (0, h)),
    ]
    args = [p, p, p, p, gcol, grow, g_h.reshape(1, -1)]
    c_spec = pl.BlockSpec((1, 2, hps, ML_DK, ML_DV), lambda b, h: (b, 0, h, 0, 0))
    n_spec = pl.BlockSpec((1, 2, hps, 1, ML_DK), lambda b, h: (b, 0, h, 0, 0))
    m_spec = pl.BlockSpec((1, 2, hps, 1, LANES), lambda b, h: (b, 0, h, 0, 0))
    if has_init:
        in_specs += [c_spec, n_spec, m_spec]
        args += list(state)
    out_specs = [pl.BlockSpec((seq, hps * ML_DV), lambda b, h: (b, h))]
    out_shape = [jax.ShapeDtypeStruct((n_batch * seq, ML_HEADS * ML_DV), BF16)]
    if emit_state:
        out_specs += [c_spec, n_spec, m_spec]
        out_shape += [
            jax.ShapeDtypeStruct((n_batch, 2, ML_HEADS, ML_DK, ML_DV), F32),
            jax.ShapeDtypeStruct((n_batch, 2, ML_HEADS, 1, ML_DK), F32),
            jax.ShapeDtypeStruct((n_batch, 2, ML_HEADS, 1, LANES), F32),
        ]
    return pl.pallas_call(
        functools.partial(_mlstm_kernel, nc=nc, has_init=has_init, emit_state=emit_state),
        grid=(n_batch, groups),
        in_specs=in_specs,
        out_specs=out_specs,
        out_shape=out_shape,
        scratch_shapes=(
            [pltpu.VMEM((ML_DK, ML_DV), F32)] * (2 * hps) + [pltpu.VMEM((1, ML_DK), F32)] * (2 * hps)
            + [pltpu.VMEM((1, LANES), F32)] * (2 * hps) + [pltpu.VMEM((seq, ML_DV), F32)] * (2 * hps)),
        compiler_params=_params("arbitrary", "arbitrary"),
        name="mlstm_scan",
    )(*args)


def _gate_layouts(gc, gr, row0, n_batch, seq):
    nc = seq // ML_CHUNK
    n = n_batch * seq
    gcol = gc[row0:row0 + n, :4 * ML_HEADS].reshape(n_batch, nc, ML_CHUNK, 4, ML_HEADS).transpose(0, 4, 1, 2, 3)
    grow = gr[:, row0:row0 + n].reshape(4, ML_HEADS, n_batch, nc, ML_CHUNK).transpose(2, 1, 3, 0, 4)
    return gcol, grow


def mlstm_layer(h, state_c, state_n, state_m, w_in, b_gate, g_h):
    hk = ML_HEADS * ML_DK
    hv = ML_HEADS * ML_DV
    n_main = 2 * hk + 2 * hv
    p = matmul(h, w_in.astype(BF16), tm=1024, tn=1024 + 512, out_dtype=F32, n_cols=n_main)
    gc, gr = mlstm_gates(h, w_in[:, n_main:], b_gate)

    gcol, grow = _gate_layouts(gc, gr, 0, BATCH, SEQ)
    y_ctx, cf, nf, mf = mlstm_scan(p, gcol, grow, g_h, row0=0, n_batch=BATCH, seq=SEQ, emit_state=True)

    gcol, grow = _gate_layouts(gc, gr, T_CTX, DEC_BATCH, DEC_SEQ)
    c0 = state_c[:, 0]
    n0 = state_n[:, 0].reshape(DEC_BATCH, 2, ML_HEADS, 1, ML_DK)
    m0 = jnp.broadcast_to(state_m[:, 0].reshape(DEC_BATCH, 2, ML_HEADS, 1, 1), (DEC_BATCH, 2, ML_HEADS, 1, LANES))
    (y_lat,) = mlstm_scan(p, gcol, grow, g_h, row0=T_CTX, n_batch=DEC_BATCH, seq=DEC_SEQ, state=(c0, n0, m0))

    y = (y_ctx, y_lat)
    new_c = cf.reshape(BATCH, 1, 2, ML_HEADS, ML_DK, ML_DV)
    new_n = nf.reshape(BATCH, 1, 2, ML_HEADS, ML_DK)
    new_m = mf[..., 0, 0].reshape(BATCH, 1, 2, ML_HEADS)
    return y, new_c, new_n, new_m


def _topk_rows(x, payload, n_out):
    return _topk_rows_multi([x], [payload], n_out)[0]


def _topk_rows_multi(xs, payloads, n_out):
    rows = float(xs[0].shape[0])
    iota = lax.broadcasted_iota(I32, xs[0].shape, 0).astype(F32)
    xs = list(xs)
    vals = [[] for _ in xs]
    outs = [[] for _ in xs]
    for _ in range(n_out):
        for j, payload in enumerate(payloads):
            x = xs[j]
            m = jnp.max(x, axis=0, keepdims=True)
            pos = jnp.min(jnp.where(x == m, iota, rows), axis=0, keepdims=True)
            sel = iota == pos
            vals[j].append(m)
            outs[j].append(pos if payload is None else jnp.max(jnp.where(sel, payload, -1.0), axis=0, keepdims=True))
            xs[j] = jnp.where(sel, NEG_INF, x)
    return [(jnp.concatenate(v, axis=0), jnp.concatenate(o, axis=0)) for v, o in zip(vals, outs)]


def _pair_candidates(a, b, combine, fill):
    k = PEER_TOPK
    row = lax.broadcasted_iota(I32, (SUBLANES, a.shape[1]), 0)
    blocks = [combine(a[0:1, :], b), combine(a[1:2, :], b[0:SUBLANES, :])]
    for i in range(2, SUBLANES):
        blocks.append(jnp.where(row < k // (i + 1), combine(a[i:i + 1, :], b[0:SUBLANES, :]), fill))
    blocks.append(combine(a[SUBLANES:k, :], b[0:1, :]))
    return jnp.concatenate(blocks, axis=0)


PEER_TILE = 128


def _peer_head_topk(qh, keys_ref):
    half = PEER_QDIM // 2
    nt = (((1,), (1,)), ((), ()))
    sc = [lax.dot_general(keys_ref[p], qh[:, p * half:(p + 1) * half].astype(BF16), nt,
                          preferred_element_type=F32) for p in range(2)]
    (sv0, si0), (sv1, si1) = _topk_rows_multi(sc, [None, None], PEER_TOPK)
    sv, si = [sv0, sv1], [si0, si1]
    cand = _pair_candidates(sv[0], sv[1], lambda x, y: x + y, NEG_INF)
    cidx = _pair_candidates(si[0], si[1], lambda x, y: x * float(PEER_NKEYS) + y, -1.0)
    best, eidx = _topk_rows(cand, cidx, PEER_TOPK)
    ex = jnp.exp(best - best[0:1, :])
    return eidx.astype(I32), ex / jnp.sum(ex, axis=0, keepdims=True)


def _peer_topk_kernel(q_ref, keys_ref, e_ref, gw_ref):
    parts = [_peer_head_topk(q_ref[hd], keys_ref) for hd in range(PEER_HEADS)]
    e_ref[...] = jnp.concatenate([p[0] for p in parts], axis=0).T
    gw_ref[...] = jnp.concatenate([p[1] for p in parts], axis=0).T


def peer_topk(q3, keys, n_tokens):
    tt = PEER_TILE
    return pl.pallas_call(
        _peer_topk_kernel,
        grid=(n_tokens // tt,),
        in_specs=[pl.BlockSpec((PEER_HEADS, tt, PEER_QDIM), lambda i: (0, i, 0)),
                  pl.BlockSpec(keys.shape, lambda i: (0, 0, 0))],
        out_specs=[pl.BlockSpec((tt, PEER_SEL), lambda i: (i, 0)), pl.BlockSpec((tt, PEER_SEL), lambda i: (i, 0))],
        out_shape=[jax.ShapeDtypeStruct((n_tokens, PEER_SEL), I32), jax.ShapeDtypeStruct((n_tokens, PEER_SEL), F32)],
        compiler_params=_params("arbitrary"),
        name="peer_topk",
    )(q3, keys)


def _peer_query_kernel(x_ref, w_ref, o_ref):
    acc = jnp.dot(x_ref[...], w_ref[...], preferred_element_type=F32)
    for hd in range(PEER_HEADS):
        o_ref[hd] = acc[:, hd * PEER_QDIM:(hd + 1) * PEER_QDIM]


def peer_query(h, w_q):
    t, d = h.shape
    tm = min(1024, t)
    return pl.pallas_call(
        _peer_query_kernel,
        grid=(t // tm,),
        in_specs=[pl.BlockSpec((tm, d), lambda i: (i, 0)), pl.BlockSpec(w_q.shape, lambda i: (0, 0))],
        out_specs=pl.BlockSpec((PEER_HEADS, tm, PEER_QDIM), lambda i: (0, i, 0)),
        out_shape=jax.ShapeDtypeStruct((PEER_HEADS, t, PEER_QDIM), F32),
        compiler_params=_params("arbitrary"),
        name="peer_query",
    )(h, w_q)


PEER_STEP_TOK = 16
PEER_SLABS = D_MODEL // LANES
PEER_ROWS = PEER_SEL * PEER_SLABS
_ERF_GELU_C = 0.7071067811865476


def _peer_apply_kernel(e_ref, en_ref, h_ref, gw_ref, expand_ref, expand_t_ref, uv_ref, o_ref,
                       buf_a, buf_b, zs, wexp, sem, *, layer):
    i = pl.program_id(0)
    n_steps = pl.num_programs(0)
    ns = PEER_SLABS
    tt = PEER_STEP_TOK
    nt = (((1,), (1,)), ((), ()))

    def gather_copy(idx, buf, n, s):
        return pltpu.make_async_copy(uv_ref.at[layer, idx], buf.at[n], sem.at[s])

    def issue(idx_ref, buf, s):
        for tok in range(tt):
            for k in range(PEER_SEL):
                gather_copy(idx_ref[tok, k], buf, tok * PEER_SEL + k, s).start(priority=k % 2)

    def wait(buf, s):
        pltpu.make_async_copy(buf, buf, sem.at[s]).wait()

    def compute(buf):
        sub = lax.broadcasted_iota(I32, (ns, PEER_ROWS), 0)
        col = lax.broadcasted_iota(I32, (ns, PEER_ROWS), 1)
        diag = (col % ns) == sub
        for t in range(tt):
            u_t = buf[pl.ds(t * PEER_SEL, PEER_SEL), 0].reshape(PEER_ROWS, LANES)
            y = lax.dot_general(h_ref[t], u_t, nt, preferred_element_type=F32)
            zs[pl.ds(t, 1), :] = jnp.sum(jnp.where(diag, y, 0.0), axis=0, keepdims=True)
        z = zs[...]
        z_hi = z.astype(BF16)
        z_lo = (z - z_hi.astype(F32)).astype(BF16)
        act = (jnp.dot(z_hi, expand_t_ref[...], preferred_element_type=F32)
               + jnp.dot(z_lo, expand_t_ref[...], preferred_element_type=F32))
        gelu = 0.5 * act * (1.0 + lax.erf(act * _ERF_GELU_C))
        w = (gw_ref[...] * gelu).astype(BF16)
        wexp[...] = jnp.dot(w, expand_ref[...], preferred_element_type=F32)
        for t in range(tt):
            wrow = wexp[pl.ds(t, 1), :]
            wbig = jnp.where(diag, jnp.broadcast_to(wrow, (ns, PEER_ROWS)), 0.0).astype(BF16)
            v_t = buf[pl.ds(t * PEER_SEL, PEER_SEL), 1].reshape(PEER_ROWS, LANES)
            o = jnp.dot(wbig, v_t, preferred_element_type=F32)
            for s in range(ns):
                o_ref[t:t + 1, s * LANES:(s + 1) * LANES] = o[s:s + 1, :]

    def step(cur, s_cur, nxt, s_nxt):
        wait(cur, s_cur)
        issue(en_ref, nxt, s_nxt)
        compute(cur)

        @pl.when(i == n_steps - 1)
        def _():
            wait(nxt, s_nxt)

    @pl.when(i == 0)
    def _():
        issue(e_ref, buf_a, 0)

    @pl.when(i % 2 == 0)
    def _():
        step(buf_a, 0, buf_b, 1)

    @pl.when(i % 2 == 1)
    def _():
        step(buf_b, 1, buf_a, 0)


def peer_apply(e, h, gw, uv, layer):
    t, d = h.shape
    ns = PEER_SLABS
    tt = PEER_STEP_TOK
    n_steps = t // tt
    h3 = h.reshape(t, ns, LANES)
    group = np.arange(PEER_ROWS) // ns
    expand = jnp.asarray(group[None, :] == np.arange(PEER_SEL)[:, None], BF16)
    out = pl.pallas_call(
        functools.partial(_peer_apply_kernel, layer=layer),
        grid=(n_steps,),
        in_specs=[
            pl.BlockSpec((tt, PEER_SEL), lambda i: (i, 0), memory_space=pltpu.SMEM),
            pl.BlockSpec((tt, PEER_SEL), lambda i: (jnp.minimum(i + 1, n_steps - 1), 0), memory_space=pltpu.SMEM),
            pl.BlockSpec((tt, ns, LANES), lambda i: (i, 0, 0)),
            pl.BlockSpec((tt, PEER_SEL), lambda i: (i, 0)),
            pl.BlockSpec((PEER_SEL, PEER_ROWS), lambda i: (0, 0)),
            pl.BlockSpec((PEER_ROWS, PEER_SEL), lambda i: (0, 0)),
            pl.BlockSpec(memory_space=pl.ANY),
        ],
        out_specs=pl.BlockSpec((tt, d), lambda i: (i, 0)),
        out_shape=jax.ShapeDtypeStruct((t, d), F32),
        scratch_shapes=[
            pltpu.VMEM((tt * PEER_SEL, 2, ns, LANES), BF16),
            pltpu.VMEM((tt * PEER_SEL, 2, ns, LANES), BF16),
            pltpu.VMEM((tt, PEER_ROWS), F32),
            pltpu.VMEM((tt, PEER_ROWS), F32),
            pltpu.SemaphoreType.DMA((2,)),
        ],
        compiler_params=pltpu.CompilerParams(dimension_semantics=("arbitrary",), vmem_limit_bytes=VMEM_LIMIT_BYTES,
                                             disable_bounds_checks=True),
        name="peer_apply",
    )(e, e, h3, gw, expand, expand.T, uv)
    return out


def peer_tables(peer_u, peer_v):
    depth, n_exp, _ = peer_u.shape
    shape = (depth, n_exp, PEER_SLABS, LANES)
    return jnp.stack([peer_u.astype(BF16).reshape(shape), peer_v.astype(BF16).reshape(shape)], axis=2)


def peer_layer(h, w_q, keys, uv, layer):
    q3 = peer_query(h, w_q.astype(BF16))
    e, gw = peer_topk(q3, keys.astype(BF16), h.shape[0])
    return peer_apply(e, h, gw, uv, layer)


def kernel(x_prompt, x_sample, cache_ckv, cache_krope, state_C, state_n, state_m, c, c_ctx, mod_w, mod_b, norm_mix, norm_ffn, mla_w_in, mla_g_q, mla_g_kv, mla_w_uq, mla_g_qn, mla_w_uk, mla_w_uv, mla_g_kn, mla_w_o, ml_w_in, ml_b_gate, ml_g_h, ml_w_o, peer_w_q, peer_keys, peer_u, peer_v):
    d = D_MODEL
    x = (x_prompt.reshape(T_CTX, d), x_sample.reshape(T_LAT, d))
    cvec = jnp.concatenate([c_ctx.reshape(1, d), c, jnp.zeros((GROUP_PAD - N_GROUPS, d), c.dtype)], axis=0)
    mod = mod_vectors(cvec, mod_w, mod_b)
    modrows = mod.reshape(mod.shape[0] * GROUP_PAD * N_MOD, 1, d)
    uv = peer_tables(peer_u, peer_v)

    (h,) = resid_modulate(x, modrows, norm_g=norm_mix[0], shift=(0, 0), scale=(0, 1))
    y, new_ckv, new_krope = mla_layer(h, cache_ckv[:, 0], cache_krope[:, 0], mla_w_in[0], mla_g_q[0], mla_g_kv[0],
                                      mla_w_uq[0], mla_g_qn[0], mla_w_uk[0], mla_w_uv[0], mla_g_kn[0])
    x, h = resid_modulate(x, modrows, y=y, w=mla_w_o[0].astype(BF16), gate=(0, 2),
                          norm_g=norm_ffn[0], shift=(0, 3), scale=(0, 4))
    y = peer_layer(h, peer_w_q[0], peer_keys[0], uv, 0)

    x, h = resid_modulate(x, modrows, y=y, gate=(0, 5), norm_g=norm_mix[1], shift=(1, 0), scale=(1, 1))
    y, new_c, new_n, new_m = mlstm_layer(h, state_C, state_n, state_m, ml_w_in[0], ml_b_gate[0], ml_g_h[0])
    x, h = resid_modulate(x, modrows, y=y, w=ml_w_o[0].astype(BF16), gate=(1, 2),
                          norm_g=norm_ffn[1], shift=(1, 3), scale=(1, 4))
    y = peer_layer(h, peer_w_q[1], peer_keys[1], uv, 1)
    y_prompt, y_sample = resid_modulate(x, modrows, y=y, gate=(1, 5), split_out=True)

    return (y_prompt.reshape(BATCH, SEQ, d), y_sample.reshape(DEC_BATCH, DEC_SEQ, d),
            new_ckv, new_krope, new_c, new_n, new_m)
```

```python
import functools

import numpy as np
import jax
import jax.numpy as jnp
from jax import lax
from jax.experimental import pallas as pl
from jax.experimental.pallas import tpu as pltpu

F32 = jnp.float32
BF16 = jnp.bfloat16
I32 = jnp.int32

D_MODEL = 2048
BATCH, SEQ = 32, 256
DEC_BATCH, DEC_SEQ = 8, 1024
PAST_LEN = 512
GRID_W = 64
N_MOD = 6
NORM_EPS = 1e-6
MLA_HEADS = 16
MLA_Q_RANK = 512
MLA_KV_RANK = 512
MLA_NOPE = 128
MLA_ROPE = 64
MLA_QK_DIM = MLA_NOPE + MLA_ROPE
MLA_HEAD_PAD = 256
ROPE_BASE = 10000.0
ML_HEADS = 8
ML_DV = D_MODEL // ML_HEADS
ML_DK = ML_DV // 2
ML_CHUNK = 64
PEER_HEADS = 8
PEER_NKEYS = 128
PEER_QDIM = 128
PEER_TOPK = 16
PEER_SEL = PEER_HEADS * PEER_TOPK

T_CTX = BATCH * SEQ
T_LAT = DEC_BATCH * DEC_SEQ
T_ALL = T_CTX + T_LAT
N_GROUPS = 1 + DEC_BATCH
GROUP_PAD = 16
ROW_TILE = 256

VMEM_LIMIT_BYTES = 56 * 1024 * 1024
LANES = 128
SUBLANES = 8

NEG_INF = float("-inf")


def _params(*sem):
    return pltpu.CompilerParams(dimension_semantics=sem, vmem_limit_bytes=VMEM_LIMIT_BYTES)


def _mod_kernel(c_ref, w_ref, b_ref, o_ref):
    c = c_ref[...]
    a = (c * jax.nn.sigmoid(c)).astype(BF16)
    o_ref[0] = jnp.dot(a, w_ref[0].astype(BF16), preferred_element_type=F32) + b_ref[0]


def mod_vectors(cvec, mod_w, mod_b):
    depth, d, n = mod_w.shape
    tn = 1024
    return pl.pallas_call(
        _mod_kernel,
        grid=(depth, n // tn),
        in_specs=[
            pl.BlockSpec((GROUP_PAD, d), lambda l, j: (0, 0)),
            pl.BlockSpec((1, d, tn), lambda l, j: (l, 0, j)),
            pl.BlockSpec((1, 1, tn), lambda l, j: (l, 0, j)),
        ],
        out_specs=pl.BlockSpec((1, GROUP_PAD, tn), lambda l, j: (l, 0, j)),
        out_shape=jax.ShapeDtypeStruct((depth, GROUP_PAD, n), F32),
        compiler_params=_params("arbitrary", "arbitrary"),
        name="mod_vectors",
    )(cvec, mod_w, mod_b.reshape(depth, 1, n))


def _group_of_tile(i):
    ctx_tiles = T_CTX // ROW_TILE
    tiles_per_lat = DEC_SEQ // ROW_TILE
    return jnp.where(i < ctx_tiles, 0, 1 + (i - ctx_tiles) // tiles_per_lat)


CTX_TILES = T_CTX // ROW_TILE


def _row_tile(parts, i):
    if len(parts) == 1:
        return parts[0][...]
    return jnp.where(i < CTX_TILES, parts[0][...], parts[1][...])


def _resmod_kernel(*refs, nx, ny, has_w, has_mod, split_out):
    i = pl.program_id(0)
    refs = list(refs)
    x = _row_tile([refs.pop(0) for _ in range(nx)], i)
    if ny:
        y = _row_tile([refs.pop(0) for _ in range(ny)], i)
        if has_w:
            y = jnp.dot(y, refs.pop(0)[...], preferred_element_type=F32)
        gate_ref = refs.pop(0)
        x = x + gate_ref[0] * y
    if has_mod:
        g_ref, sh_ref, sc_ref = refs.pop(0), refs.pop(0), refs.pop(0)
    if ny:
        if split_out:
            xa_ref, xb_ref = refs.pop(0), refs.pop(0)

            @pl.when(i < CTX_TILES)
            def _():
                xa_ref[...] = x

            @pl.when(i >= CTX_TILES)
            def _():
                xb_ref[...] = x
        else:
            refs.pop(0)[...] = x
    if has_mod:
        h_ref = refs.pop(0)
        ms = jnp.mean(x * x, axis=-1, keepdims=True)
        yn = x * lax.rsqrt(ms + NORM_EPS) * g_ref[...]
        h_ref[...] = (yn * (1.0 + sc_ref[0]) + sh_ref[0]).astype(h_ref.dtype)


def resid_modulate(x, modrows, *, y=None, w=None, gate=None, norm_g=None, shift=None, scale=None, split_out=False):
    xs = list(x) if isinstance(x, (tuple, list)) else [x]
    ys = [] if y is None else (list(y) if isinstance(y, (tuple, list)) else [y])
    d = xs[0].shape[1]
    t = sum(a.shape[0] for a in xs)
    has_mod = norm_g is not None
    row_spec = pl.BlockSpec((ROW_TILE, d), lambda i: (i, 0))
    ctx_spec = pl.BlockSpec((ROW_TILE, d), lambda i: (jnp.minimum(i, CTX_TILES - 1), 0))
    lat_spec = pl.BlockSpec((ROW_TILE, d), lambda i: (jnp.maximum(i - CTX_TILES, 0), 0))

    def part_specs(parts):
        return [row_spec] if len(parts) == 1 else [ctx_spec, lat_spec]

    def mod_spec(layer_k):
        layer, k = layer_k
        return pl.BlockSpec((1, 1, d), lambda i: ((layer * GROUP_PAD + _group_of_tile(i)) * N_MOD + k, 0, 0))

    args, in_specs = list(xs), part_specs(xs)
    if ys:
        args += ys
        in_specs += part_specs(ys)
        if w is not None:
            args.append(w)
            in_specs.append(pl.BlockSpec(w.shape, lambda i: (0, 0)))
        args.append(modrows)
        in_specs.append(mod_spec(gate))
    if has_mod:
        args += [norm_g.reshape(1, d), modrows, modrows]
        in_specs += [pl.BlockSpec((1, d), lambda i: (0, 0)), mod_spec(shift), mod_spec(scale)]
    out_shape, out_specs = [], []
    if ys:
        if split_out:
            out_shape += [jax.ShapeDtypeStruct((T_CTX, d), F32), jax.ShapeDtypeStruct((t - T_CTX, d), F32)]
            out_specs += [ctx_spec, lat_spec]
        else:
            out_shape.append(jax.ShapeDtypeStruct((t, d), F32))
            out_specs.append(row_spec)
    if has_mod:
        out_shape.append(jax.ShapeDtypeStruct((t, d), BF16))
        out_specs.append(row_spec)
    outs = pl.pallas_call(
        functools.partial(_resmod_kernel, nx=len(xs), ny=len(ys), has_w=w is not None, has_mod=has_mod,
                          split_out=split_out),
        grid=(t // ROW_TILE,),
        in_specs=in_specs,
        out_specs=out_specs,
        out_shape=out_shape,
        compiler_params=_params("arbitrary"),
        name="resid_modulate",
    )(*args)
    return outs


def _mm_kernel(x_ref, w_ref, o_ref):
    o_ref[...] = jnp.dot(x_ref[...].astype(BF16), w_ref[...], preferred_element_type=F32).astype(o_ref.dtype)


def matmul(x, w, *, tm, tn, out_dtype, n_cols=None):
    m, k = x.shape
    n = w.shape[1] if n_cols is None else n_cols
    return pl.pallas_call(
        _mm_kernel,
        grid=(m // tm, n // tn),
        in_specs=[pl.BlockSpec((tm, k), lambda i, j: (i, 0)), pl.BlockSpec((k, tn), lambda i, j: (0, j))],
        out_specs=pl.BlockSpec((tm, tn), lambda i, j: (i, j)),
        out_shape=jax.ShapeDtypeStruct((m, n), out_dtype),
        compiler_params=_params("arbitrary", "arbitrary"),
        name="matmul",
    )(x, w)


def _rms(x, n):
    return lax.rsqrt(jnp.sum(x * x, axis=-1, keepdims=True) / n + NORM_EPS)


def _mla_in_kernel(h_ref, w_ref, gq_ref, gkv_ref, cq_ref, ckv_ref, kr_ref):
    a = jnp.dot(h_ref[...], w_ref[...], preferred_element_type=F32)
    cq = a[:, :MLA_Q_RANK]
    ckv = a[:, MLA_Q_RANK:MLA_Q_RANK + MLA_KV_RANK]
    cq_ref[...] = (cq * _rms(cq, MLA_Q_RANK) * gq_ref[...]).astype(cq_ref.dtype)
    ckv_ref[...] = ckv * _rms(ckv, MLA_KV_RANK) * gkv_ref[...]
    kr_ref[...] = a[:, MLA_Q_RANK + MLA_KV_RANK:]


def mla_in_proj(h, w_ext, g_q, g_kv):
    t, d = h.shape
    n = w_ext.shape[1]
    tm = 512
    return pl.pallas_call(
        _mla_in_kernel,
        grid=(t // tm,),
        in_specs=[
            pl.BlockSpec((tm, d), lambda i: (i, 0)),
            pl.BlockSpec((d, n), lambda i: (0, 0)),
            pl.BlockSpec((1, MLA_Q_RANK), lambda i: (0, 0)),
            pl.BlockSpec((1, MLA_KV_RANK), lambda i: (0, 0)),
        ],
        out_specs=[
            pl.BlockSpec((tm, MLA_Q_RANK), lambda i: (i, 0)),
            pl.BlockSpec((tm, MLA_KV_RANK), lambda i: (i, 0)),
            pl.BlockSpec((tm, LANES), lambda i: (i, 0)),
        ],
        out_shape=[
            jax.ShapeDtypeStruct((t, MLA_Q_RANK), BF16),
            jax.ShapeDtypeStruct((t, MLA_KV_RANK), F32),
            jax.ShapeDtypeStruct((t, LANES), F32),
        ],
        compiler_params=_params("arbitrary"),
        name="mla_in_proj",
    )(h, w_ext, g_q.reshape(1, -1), g_kv.reshape(1, -1))


def _rope_block(x, c, s1, s2):
    return x * c + pltpu.roll(x, 96, 1) * s1 + pltpu.roll(x, 32, 1) * s2


def _q_up_kernel(cq_ref, w_ref, g_ref, c_ref, s1_ref, s2_ref, q_ref, *, heads):
    a = jnp.dot(cq_ref[...], w_ref[...], preferred_element_type=F32)
    g = g_ref[...]
    for hh in range(heads):
        base = hh * MLA_HEAD_PAD
        nope = a[:, base:base + MLA_NOPE]
        rp = a[:, base + MLA_NOPE:base + MLA_HEAD_PAD]
        ss = jnp.sum(nope * nope, axis=-1, keepdims=True) + jnp.sum(rp * rp, axis=-1, keepdims=True)
        r = lax.rsqrt(ss / MLA_QK_DIM + NORM_EPS)
        xr = _rope_block(rp * r * g[:, MLA_NOPE:], c_ref[...], s1_ref[...], s2_ref[...])
        q_ref[:, base:base + MLA_NOPE] = (nope * r * g[:, :MLA_NOPE]).astype(q_ref.dtype)
        q_ref[:, base + MLA_NOPE:base + MLA_HEAD_PAD] = xr.astype(q_ref.dtype)


def mla_q_up(cq, w_uq_ext, g_qn_ext, rope_c, rope_s1, rope_s2):
    t, r = cq.shape
    n = w_uq_ext.shape[1]
    tm, heads = 512, 4
    tn = heads * MLA_HEAD_PAD
    tab = pl.BlockSpec((tm, LANES), lambda i, j: (i, 0))
    return pl.pallas_call(
        functools.partial(_q_up_kernel, heads=heads),
        grid=(t // tm, n // tn),
        in_specs=[
            pl.BlockSpec((tm, r), lambda i, j: (i, 0)),
            pl.BlockSpec((r, tn), lambda i, j: (0, j)),
            pl.BlockSpec((1, MLA_HEAD_PAD), lambda i, j: (0, 0)),
            tab, tab, tab,
        ],
        out_specs=pl.BlockSpec((tm, tn), lambda i, j: (i, j)),
        out_shape=jax.ShapeDtypeStruct((t, n), BF16),
        compiler_params=_params("arbitrary", "arbitrary"),
        name="mla_q_up",
    )(cq, w_uq_ext, g_qn_ext, rope_c, rope_s1, rope_s2)


def _kv_up_kernel(ckv_ref, kr_ref, wk_ref, wv_ref, g_ref, c_ref, s1_ref, s2_ref, k_ref, v_ref, *, heads):
    ckv = ckv_ref[...].astype(BF16)
    kn = jnp.dot(ckv, wk_ref[...], preferred_element_type=F32)
    v_ref[...] = jnp.dot(ckv, wv_ref[...], preferred_element_type=F32).astype(v_ref.dtype)
    g = g_ref[...]
    lane = lax.broadcasted_iota(I32, (1, LANES), 1)
    kr = jnp.where(lane < MLA_ROPE, kr_ref[...], 0.0)
    ss_r = jnp.sum(kr * kr, axis=-1, keepdims=True)
    krot = _rope_block(kr * g[:, MLA_NOPE:], c_ref[...], s1_ref[...], s2_ref[...])
    for hh in range(heads):
        nope = kn[:, hh * MLA_NOPE:(hh + 1) * MLA_NOPE]
        r = lax.rsqrt((jnp.sum(nope * nope, axis=-1, keepdims=True) + ss_r) / MLA_QK_DIM + NORM_EPS)
        base = hh * MLA_HEAD_PAD
        k_ref[:, base:base + MLA_NOPE] = (nope * r * g[:, :MLA_NOPE]).astype(k_ref.dtype)
        k_ref[:, base + MLA_NOPE:base + MLA_HEAD_PAD] = (krot * r).astype(k_ref.dtype)


def mla_kv_up(ckv, kr, w_uk, w_uv, g_kn_ext, rope_c, rope_s1, rope_s2):
    t, r = ckv.shape
    tm, heads = 512, 4
    tab = pl.BlockSpec((tm, LANES), lambda i, j: (i, 0))
    return pl.pallas_call(
        functools.partial(_kv_up_kernel, heads=heads),
        grid=(t // tm, MLA_HEADS // heads),
        in_specs=[
            pl.BlockSpec((tm, r), lambda i, j: (i, 0)),
            tab,
            pl.BlockSpec((r, heads * MLA_NOPE), lambda i, j: (0, j)),
            pl.BlockSpec((r, heads * MLA_NOPE), lambda i, j: (0, j)),
            pl.BlockSpec((1, MLA_HEAD_PAD), lambda i, j: (0, 0)),
            tab, tab, tab,
        ],
        out_specs=[
            pl.BlockSpec((tm, heads * MLA_HEAD_PAD), lambda i, j: (i, j)),
            pl.BlockSpec((tm, heads * MLA_NOPE), lambda i, j: (i, j)),
        ],
        out_shape=[
            jax.ShapeDtypeStruct((t, MLA_HEADS * MLA_HEAD_PAD), BF16),
            jax.ShapeDtypeStruct((t, MLA_HEADS * MLA_NOPE), BF16),
        ],
        compiler_params=_params("arbitrary", "arbitrary"),
        name="mla_kv_up",
    )(ckv, kr, w_uk, w_uv, g_kn_ext, rope_c, rope_s1, rope_s2)


ATTN_HEADS_PER_STEP = 4
_LOG2_E = 1.4426950408889634


def _attn_kernel(*refs, nseg):
    q_ref = refs[0]
    k_refs = refs[1:1 + nseg]
    v_refs = refs[1 + nseg:1 + 2 * nseg]
    o_ref = refs[-1]
    scale = MLA_QK_DIM ** -0.5
    nt = (((1,), (1,)), ((), ()))
    for hh in range(ATTN_HEADS_PER_STEP):
        qk_cols = slice(hh * MLA_HEAD_PAD, (hh + 1) * MLA_HEAD_PAD)
        v_cols = slice(hh * MLA_NOPE, (hh + 1) * MLA_NOPE)
        q = q_ref[:, qk_cols]
        s = [lax.dot_general(q, k[:, qk_cols], nt, preferred_element_type=F32) for k in k_refs]
        m = functools.reduce(jnp.maximum, [jnp.max(x, axis=-1, keepdims=True) for x in s])
        e = [jnp.exp2((x - m) * (scale * _LOG2_E)) for x in s]
        inv = 1.0 / functools.reduce(lambda a, b: a + b, [jnp.sum(x, axis=-1, keepdims=True) for x in e])
        o = functools.reduce(
            lambda a, b: a + b,
            [jnp.dot(x.astype(BF16), v[:, v_cols], preferred_element_type=F32) for x, v in zip(e, v_refs)])
        o_ref[:, v_cols] = (o * inv).astype(o_ref.dtype)


def mla_attention(q, k, v, *, q_row0, n_batch, s_q, segs):
    tq = 256
    nq = s_q // tq
    nseg = len(segs)
    q_blk0 = q_row0 // tq
    hp = ATTN_HEADS_PER_STEP
    in_specs = [pl.BlockSpec((tq, hp * MLA_HEAD_PAD), lambda b, h, i: (q_blk0 + b * nq + i, h))]
    for row0, length in segs:
        in_specs.append(pl.BlockSpec((length, hp * MLA_HEAD_PAD), lambda b, h, i, o=row0 // length: (o + b, h)))
    for row0, length in segs:
        in_specs.append(pl.BlockSpec((length, hp * MLA_NOPE), lambda b, h, i, o=row0 // length: (o + b, h)))
    return pl.pallas_call(
        functools.partial(_attn_kernel, nseg=nseg),
        grid=(n_batch, MLA_HEADS // hp, nq),
        in_specs=in_specs,
        out_specs=pl.BlockSpec((tq, hp * MLA_NOPE), lambda b, h, i: (b * nq + i, h)),
        out_shape=jax.ShapeDtypeStruct((n_batch * s_q, MLA_HEADS * MLA_NOPE), BF16),
        compiler_params=_params("arbitrary", "arbitrary", "arbitrary"),
        name="mla_attention",
    )(q, *([k] * nseg), *([v] * nseg))


def _rope_tables():
    nf = MLA_ROPE // 4
    inv_freq = jnp.power(ROPE_BASE, -jnp.arange(nf, dtype=F32) / nf)
    tok = jnp.arange(DEC_SEQ)
    row = (tok // GRID_W).astype(F32)[:, None] * inv_freq[None, :]
    col = (tok % GRID_W).astype(F32)[:, None] * inv_freq[None, :]
    ang = jnp.concatenate([row, col], axis=-1)
    cos, sin = jnp.cos(ang), jnp.sin(ang)
    z32 = jnp.zeros_like(cos)
    z64 = jnp.zeros((DEC_SEQ, 64), F32)
    c = jnp.concatenate([cos, cos, z64], axis=-1)
    s1 = jnp.concatenate([-sin, z32, z64], axis=-1)
    s2 = jnp.concatenate([z32, sin, z64], axis=-1)
    ident_c = jnp.concatenate([jnp.ones((1, 64), F32), jnp.zeros((1, 64), F32)], axis=-1)

    def full(lat, ident):
        n_cache = DEC_BATCH * PAST_LEN
        return jnp.concatenate([
            jnp.broadcast_to(ident, (T_CTX, LANES)),
            jnp.tile(lat, (DEC_BATCH, 1)),
            jnp.broadcast_to(ident, (n_cache, LANES)),
        ], axis=0)

    zero = jnp.zeros((1, LANES), F32)
    return full(c, ident_c), full(s1, zero), full(s2, zero)


_ROPE_PERM = np.concatenate([np.arange(0, 16), np.arange(32, 48), np.arange(16, 32), np.arange(48, 64)])


def mla_layer(h, cache_ckv, cache_krope, w_in, g_q, g_kv, w_uq, g_qn, w_uk, w_uv, g_kn):
    d = h.shape[1]
    perm = _ROPE_PERM
    n_lat = MLA_Q_RANK + MLA_KV_RANK
    w_in_ext = jnp.concatenate([w_in[:, :n_lat], w_in[:, n_lat:][:, perm], w_in[:, n_lat:]], axis=1).astype(BF16)
    cq, ckv, kr = mla_in_proj(h, w_in_ext, g_q, g_kv)

    w3 = w_uq.reshape(MLA_Q_RANK, MLA_HEADS, MLA_QK_DIM)
    w_uq_ext = jnp.concatenate([
        w3[:, :, :MLA_NOPE], w3[:, :, MLA_NOPE:][:, :, perm],
        jnp.zeros((MLA_Q_RANK, MLA_HEADS, MLA_HEAD_PAD - MLA_QK_DIM), w_uq.dtype)], axis=-1)
    w_uq_ext = w_uq_ext.reshape(MLA_Q_RANK, MLA_HEADS * MLA_HEAD_PAD).astype(BF16)

    def gain_ext(g):
        return jnp.concatenate([g[:MLA_NOPE], g[MLA_NOPE:][perm],
                                jnp.zeros((MLA_HEAD_PAD - MLA_QK_DIM,), g.dtype)]).reshape(1, MLA_HEAD_PAD)

    rope_c, rope_s1, rope_s2 = _rope_tables()
    q = mla_q_up(cq, w_uq_ext, gain_ext(g_qn), rope_c[:T_ALL], rope_s1[:T_ALL], rope_s2[:T_ALL])

    n_cache = DEC_BATCH * PAST_LEN
    ckv_all = jnp.concatenate([ckv, cache_ckv.reshape(n_cache, MLA_KV_RANK)], axis=0)
    kr_cache = cache_krope.reshape(n_cache, MLA_ROPE)
    kr_cache = jnp.concatenate([kr_cache[:, perm], kr_cache], axis=1)
    kr_all = jnp.concatenate([kr, kr_cache], axis=0)
    k, v = mla_kv_up(ckv_all, kr_all, w_uk.astype(BF16), w_uv.astype(BF16), gain_ext(g_kn),
                     rope_c, rope_s1, rope_s2)

    o_ctx = mla_attention(q, k, v, q_row0=0, n_batch=BATCH, s_q=SEQ, segs=[(0, SEQ)])
    o_lat = mla_attention(q, k, v, q_row0=T_CTX, n_batch=DEC_BATCH, s_q=DEC_SEQ,
                          segs=[(T_ALL, PAST_LEN), (T_CTX, DEC_SEQ)])
    y = (o_ctx, o_lat)
    new_ckv = ckv[:T_CTX].reshape(BATCH, 1, SEQ, MLA_KV_RANK)
    new_krope = kr[:T_CTX, MLA_ROPE:].reshape(BATCH, 1, SEQ, MLA_ROPE)
    return y, new_ckv, new_krope


def _log_sigmoid(x):
    return jnp.minimum(x, 0.0) - jnp.log(1.0 + jnp.exp(-jnp.abs(x)))


def _gates_kernel(h_ref, wg_ref, wgt_ref, b_ref, bt_ref, gc_ref, gr_ref):
    h = h_ref[...]
    gc = jnp.dot(h, wg_ref[...], preferred_element_type=F32) + b_ref[...]
    gr = lax.dot_general(wgt_ref[...], h, (((1,), (1,)), ((), ())), preferred_element_type=F32) + bt_ref[...]
    lane = lax.broadcasted_iota(I32, (1, LANES), 1)
    is_f = ((lane >= ML_HEADS) & (lane < 2 * ML_HEADS)) | ((lane >= 3 * ML_HEADS) & (lane < 4 * ML_HEADS))
    gc_ref[...] = jnp.where(is_f, _log_sigmoid(gc), gc)
    row = lax.broadcasted_iota(I32, (4 * ML_HEADS, 1), 0)
    is_fr = ((row >= ML_HEADS) & (row < 2 * ML_HEADS)) | ((row >= 3 * ML_HEADS) & (row < 4 * ML_HEADS))
    gr_ref[...] = jnp.where(is_fr, _log_sigmoid(gr), gr)


def mlstm_gates(h, w_g, b_gate):
    t, d = h.shape
    ng = 4 * ML_HEADS
    tm = 512
    wg = jnp.concatenate([w_g, jnp.zeros((d, LANES - ng), w_g.dtype)], axis=1).astype(BF16)
    wgt = w_g.T.astype(BF16)
    b = jnp.concatenate([b_gate, jnp.zeros((LANES - ng,), b_gate.dtype)]).reshape(1, LANES)
    bt = b_gate.reshape(ng, 1)
    return pl.pallas_call(
        _gates_kernel,
        grid=(t // tm,),
        in_specs=[
            pl.BlockSpec((tm, d), lambda i: (i, 0)),
            pl.BlockSpec((d, LANES), lambda i: (0, 0)),
            pl.BlockSpec((ng, d), lambda i: (0, 0)),
            pl.BlockSpec((1, LANES), lambda i: (0, 0)),
            pl.BlockSpec((ng, 1), lambda i: (0, 0)),
        ],
        out_specs=[pl.BlockSpec((tm, LANES), lambda i: (i, 0)), pl.BlockSpec((ng, tm), lambda i: (0, i))],
        out_shape=[jax.ShapeDtypeStruct((t, LANES), F32), jax.ShapeDtypeStruct((ng, t), F32)],
        compiler_params=_params("arbitrary"),
        name="mlstm_gates",
    )(h, wg, wgt, b, bt)


ML_HEADS_PER_STEP = 2


def _mlstm_kernel(*refs, nc, has_init, emit_state):
    refs = list(refs)
    q_ref, k_ref, v_ref, o_ref, gc_ref, gr_ref, gh_ref = [refs.pop(0) for _ in range(7)]
    if has_init:
        c0_ref, n0_ref, m0_ref = [refs.pop(0) for _ in range(3)]
    y_ref = refs.pop(0)
    if emit_state:
        cf_ref, nf_ref, mf_ref = [refs.pop(0) for _ in range(3)]
    n_chain = 2 * ML_HEADS_PER_STEP
    mem_s, nrm_s, m_s, hs_s, dmat_s, qk_s, cols_s = (refs[j * n_chain:(j + 1) * n_chain] for j in range(7))

    L = ML_CHUNK
    tt = lax.broadcasted_iota(I32, (L, L), 0)
    ss = lax.broadcasted_iota(I32, (L, L), 1)
    q_scale = ML_DK ** -0.5
    nt = (((1,), (1,)), ((), ()))
    tn = (((0,), (0,)), ((), ()))

    for hh in range(ML_HEADS_PER_STEP):
        for d in range(2):
            ch = 2 * hh + d
            if has_init:
                mem_s[ch][...] = c0_ref[0, d, hh]
                nrm_s[ch][...] = n0_ref[0, d, hh]
                m_s[ch][...] = m0_ref[0, d, hh]
            else:
                mem_s[ch][...] = jnp.zeros(mem_s[ch].shape, F32)
                nrm_s[ch][...] = jnp.zeros(nrm_s[ch].shape, F32)
                m_s[ch][...] = jnp.zeros(m_s[ch].shape, F32)

    def gate_part(hh, d, c):
        ch = 2 * hh + d
        kq = slice(hh * ML_DK, (hh + 1) * ML_DK)
        causal = (ss <= tt) if d == 0 else (ss >= tt)
        causal_t = (tt <= ss) if d == 0 else (tt >= ss)
        gcol = gc_ref[0, hh, c]
        grow = gr_ref[0, hh, c]
        f_col = gcol[:, 2 * d + 1:2 * d + 2]
        i_row = grow[2 * d:2 * d + 1, :]
        f_row = grow[2 * d + 1:2 * d + 2, :]
        cum_col = jnp.sum(jnp.where(causal, jnp.broadcast_to(f_row, (L, L)), 0.0), axis=1, keepdims=True)
        cum_row = jnp.sum(jnp.where(causal_t, jnp.broadcast_to(f_col, (L, L)), 0.0), axis=0, keepdims=True)
        dmat = jnp.where(causal, cum_col - cum_row + i_row, NEG_INF)
        rows = pl.ds(pl.multiple_of(c * L, L), L)
        qb = (q_ref[rows, kq] * q_scale).astype(BF16)
        dmat_s[ch][c] = dmat
        qk_s[ch][c] = lax.dot_general(qb, k_ref[rows, kq].astype(BF16), nt, preferred_element_type=F32)
        cols_s[ch][c, :, 0:1] = cum_col
        cols_s[ch][c, :, 1:2] = jnp.max(dmat, axis=1, keepdims=True)

    def chunk(hh, d, c):
        ch = 2 * hh + d
        kq = slice(hh * ML_DK, (hh + 1) * ML_DK)
        kv = slice(hh * ML_DV, (hh + 1) * ML_DV)
        last = L - 1 if d == 0 else 0
        i_col = gc_ref[0, hh, c][:, 2 * d:2 * d + 1]
        cols = cols_s[ch][c]
        cum_col = cols[:, 0:1]
        max_d = cols[:, 1:2]
        dmat = dmat_s[ch][c]
        total = cum_col[last:last + 1, :]
        m_prev = m_s[ch][:, 0:1]
        inter = cum_col + m_prev
        m_t = jnp.maximum(inter, max_d)
        w_inter = jnp.exp(inter - m_t)
        rows = pl.ds(pl.multiple_of(c * L, L), L)
        qf = q_ref[rows, kq] * q_scale
        kf = k_ref[rows, kq]
        vb = v_ref[rows, kv].astype(BF16)
        qb = qf.astype(BF16)
        a = jnp.exp(dmat - m_t) * qk_s[ch][c]
        mem = mem_s[ch][...]
        nrm = nrm_s[ch][...]
        num = (w_inter * jnp.dot(qb, mem.astype(BF16), preferred_element_type=F32)
               + jnp.dot(a.astype(BF16), vb, preferred_element_type=F32))
        den = w_inter * jnp.sum(qf * nrm, axis=1, keepdims=True) + jnp.sum(a, axis=1, keepdims=True)
        hs_s[ch][rows, :] = num / jnp.maximum(jnp.abs(den), jnp.exp(-m_t))
        m_new = m_t[last:last + 1, :]
        decay = jnp.exp(total + m_prev - m_new)
        w_s = jnp.exp(total - cum_col + i_col - m_new)
        wk = w_s * kf
        mem_s[ch][...] = decay * mem + lax.dot_general(wk.astype(BF16), vb, tn, preferred_element_type=F32)
        nrm_s[ch][...] = decay * nrm + jnp.sum(wk, axis=0, keepdims=True)
        m_s[ch][...] = jnp.broadcast_to(m_new, m_s[ch].shape)

    def all_gate_parts(c, carry):
        for hh in range(ML_HEADS_PER_STEP):
            gate_part(hh, 0, c)
            gate_part(hh, 1, c)
        return carry

    def all_chains(ci, carry):
        for hh in range(ML_HEADS_PER_STEP):
            chunk(hh, 0, ci)
            chunk(hh, 1, nc - 1 - ci)
        return carry

    lax.fori_loop(0, nc, all_gate_parts, 0)
    lax.fori_loop(0, nc, all_chains, 0)
    for hh in range(ML_HEADS_PER_STEP):
        if emit_state:
            for d in range(2):
                cf_ref[0, d, hh] = mem_s[2 * hh + d][...]
                nf_ref[0, d, hh] = nrm_s[2 * hh + d][...]
                mf_ref[0, d, hh] = m_s[2 * hh + d][...]
        kv = slice(hh * ML_DV, (hh + 1) * ML_DV)
        hs = hs_s[2 * hh][...] + hs_s[2 * hh + 1][...]
        hn = hs * lax.rsqrt(jnp.mean(hs * hs, axis=-1, keepdims=True) + NORM_EPS) * gh_ref[:, kv]
        y_ref[:, kv] = (hn * jax.nn.sigmoid(o_ref[:, kv])).astype(y_ref.dtype)


def mlstm_scan(p, gcol, grow, g_h, *, row0, n_batch, seq, state=None, emit_state=False):
    nc = seq // ML_CHUNK
    rb0 = row0 // seq
    hps = ML_HEADS_PER_STEP
    groups = ML_HEADS // hps
    has_init = state is not None
    in_specs = [
        pl.BlockSpec((seq, hps * ML_DK), lambda b, h: (rb0 + b, h)),
        pl.BlockSpec((seq, hps * ML_DK), lambda b, h: (rb0 + b, groups + h)),
        pl.BlockSpec((seq, hps * ML_DV), lambda b, h: (rb0 + b, groups + h)),
        pl.BlockSpec((seq, hps * ML_DV), lambda b, h: (rb0 + b, 2 * groups + h)),
        pl.BlockSpec((1, hps, nc, ML_CHUNK, 4), lambda b, h: (b, h, 0, 0, 0)),
        pl.BlockSpec((1, hps, nc, 4, ML_CHUNK), lambda b, h: (b, h, 0, 0, 0)),
        pl.BlockSpec((1, hps * ML_DV), lambda b, h: (0, h)),
    ]
    args = [p, p, p, p, gcol, grow, g_h.reshape(1, -1)]
    c_spec = pl.BlockSpec((1, 2, hps, ML_DK, ML_DV), lambda b, h: (b, 0, h, 0, 0))
    n_spec = pl.BlockSpec((1, 2, hps, 1, ML_DK), lambda b, h: (b, 0, h, 0, 0))
    m_spec = pl.BlockSpec((1, 2, hps, 1, LANES), lambda b, h: (b, 0, h, 0, 0))
    if has_init:
        in_specs += [c_spec, n_spec, m_spec]
        args += list(state)
    out_specs = [pl.BlockSpec((seq, hps * ML_DV), lambda b, h: (b, h))]
    out_shape = [jax.ShapeDtypeStruct((n_batch * seq, ML_HEADS * ML_DV), BF16)]
    if emit_state:
        out_specs += [c_spec, n_spec, m_spec]
        out_shape += [
            jax.ShapeDtypeStruct((n_batch, 2, ML_HEADS, ML_DK, ML_DV), F32),
            jax.ShapeDtypeStruct((n_batch, 2, ML_HEADS, 1, ML_DK), F32),
            jax.ShapeDtypeStruct((n_batch, 2, ML_HEADS, 1, LANES), F32),
        ]
    return pl.pallas_call(
        functools.partial(_mlstm_kernel, nc=nc, has_init=has_init, emit_state=emit_state),
        grid=(n_batch, groups),
        in_specs=in_specs,
        out_specs=out_specs,
        out_shape=out_shape,
        scratch_shapes=(
            [pltpu.VMEM((ML_DK, ML_DV), F32)] * (2 * hps) + [pltpu.VMEM((1, ML_DK), F32)] * (2 * hps)
            + [pltpu.VMEM((1, LANES), F32)] * (2 * hps) + [pltpu.VMEM((seq, ML_DV), F32)] * (2 * hps)
            + [pltpu.VMEM((nc, ML_CHUNK, ML_CHUNK), F32)] * (4 * hps) + [pltpu.VMEM((nc, ML_CHUNK, 2), F32)] * (2 * hps)),
        compiler_params=_params("arbitrary", "arbitrary"),
        name="mlstm_scan",
    )(*args)


def _gate_layouts(gc, gr, row0, n_batch, seq):
    nc = seq // ML_CHUNK
    n = n_batch * seq
    gcol = gc[row0:row0 + n, :4 * ML_HEADS].reshape(n_batch, nc, ML_CHUNK, 4, ML_HEADS).transpose(0, 4, 1, 2, 3)
    grow = gr[:, row0:row0 + n].reshape(4, ML_HEADS, n_batch, nc, ML_CHUNK).transpose(2, 1, 3, 0, 4)
    return gcol, grow


def mlstm_layer(h, state_c, state_n, state_m, w_in, b_gate, g_h):
    hk = ML_HEADS * ML_DK
    hv = ML_HEADS * ML_DV
    n_main = 2 * hk + 2 * hv
    p = matmul(h, w_in.astype(BF16), tm=1024, tn=1024 + 512, out_dtype=F32, n_cols=n_main)
    gc, gr = mlstm_gates(h, w_in[:, n_main:], b_gate)

    gcol, grow = _gate_layouts(gc, gr, 0, BATCH, SEQ)
    y_ctx, cf, nf, mf = mlstm_scan(p, gcol, grow, g_h, row0=0, n_batch=BATCH, seq=SEQ, emit_state=True)

    gcol, grow = _gate_layouts(gc, gr, T_CTX, DEC_BATCH, DEC_SEQ)
    c0 = state_c[:, 0]
    n0 = state_n[:, 0].reshape(DEC_BATCH, 2, ML_HEADS, 1, ML_DK)
    m0 = jnp.broadcast_to(state_m[:, 0].reshape(DEC_BATCH, 2, ML_HEADS, 1, 1), (DEC_BATCH, 2, ML_HEADS, 1, LANES))
    (y_lat,) = mlstm_scan(p, gcol, grow, g_h, row0=T_CTX, n_batch=DEC_BATCH, seq=DEC_SEQ, state=(c0, n0, m0))

    y = (y_ctx, y_lat)
    new_c = cf.reshape(BATCH, 1, 2, ML_HEADS, ML_DK, ML_DV)
    new_n = nf.reshape(BATCH, 1, 2, ML_HEADS, ML_DK)
    new_m = mf[..., 0, 0].reshape(BATCH, 1, 2, ML_HEADS)
    return y, new_c, new_n, new_m


def _topk_rows(x, payload, n_out):
    return _topk_rows_multi([x], [payload], n_out)[0]


def _topk_rows_multi(xs, payloads, n_out):
    rows = float(xs[0].shape[0])
    iota = lax.broadcasted_iota(I32, xs[0].shape, 0).astype(F32)
    xs = list(xs)
    vals = [[] for _ in xs]
    outs = [[] for _ in xs]
    for _ in range(n_out):
        for j, payload in enumerate(payloads):
            x = xs[j]
            m = jnp.max(x, axis=0, keepdims=True)
            pos = jnp.min(jnp.where(x == m, iota, rows), axis=0, keepdims=True)
            sel = iota == pos
            vals[j].append(m)
            outs[j].append(pos if payload is None else jnp.max(jnp.where(sel, payload, -1.0), axis=0, keepdims=True))
            xs[j] = jnp.where(sel, NEG_INF, x)
    return [(jnp.concatenate(v, axis=0), jnp.concatenate(o, axis=0)) for v, o in zip(vals, outs)]


def _pair_candidates(a, b, combine, fill):
    k = PEER_TOPK
    row = lax.broadcasted_iota(I32, (SUBLANES, a.shape[1]), 0)
    blocks = [combine(a[0:1, :], b), combine(a[1:2, :], b[0:SUBLANES, :])]
    for i in range(2, SUBLANES):
        blocks.append(jnp.where(row < k // (i + 1), combine(a[i:i + 1, :], b[0:SUBLANES, :]), fill))
    blocks.append(combine(a[SUBLANES:k, :], b[0:1, :]))
    return jnp.concatenate(blocks, axis=0)


PEER_TILE = 128


def _peer_head_topk(qh, keys_ref):
    half = PEER_QDIM // 2
    nt = (((1,), (1,)), ((), ()))
    sc = [lax.dot_general(keys_ref[p], qh[:, p * half:(p + 1) * half].astype(BF16), nt,
                          preferred_element_type=F32) for p in range(2)]
    (sv0, si0), (sv1, si1) = _topk_rows_multi(sc, [None, None], PEER_TOPK)
    sv, si = [sv0, sv1], [si0, si1]
    cand = _pair_candidates(sv[0], sv[1], lambda x, y: x + y, NEG_INF)
    cidx = _pair_candidates(si[0], si[1], lambda x, y: x * float(PEER_NKEYS) + y, -1.0)
    best, eidx = _topk_rows(cand, cidx, PEER_TOPK)
    ex = jnp.exp(best - best[0:1, :])
    return eidx.astype(I32), ex / jnp.sum(ex, axis=0, keepdims=True)


def _peer_topk_kernel(q_ref, keys_ref, e_ref, gw_ref):
    parts = [_peer_head_topk(q_ref[hd], keys_ref) for hd in range(PEER_HEADS)]
    e_ref[...] = jnp.concatenate([p[0] for p in parts], axis=0).T
    gw_ref[...] = jnp.concatenate([p[1] for p in parts], axis=0).T


def peer_topk(q3, keys, n_tokens):
    tt = PEER_TILE
    return pl.pallas_call(
        _peer_topk_kernel,
        grid=(n_tokens // tt,),
        in_specs=[pl.BlockSpec((PEER_HEADS, tt, PEER_QDIM), lambda i: (0, i, 0)),
                  pl.BlockSpec(keys.shape, lambda i: (0, 0, 0))],
        out_specs=[pl.BlockSpec((tt, PEER_SEL), lambda i: (i, 0)), pl.BlockSpec((tt, PEER_SEL), lambda i: (i, 0))],
        out_shape=[jax.ShapeDtypeStruct((n_tokens, PEER_SEL), I32), jax.ShapeDtypeStruct((n_tokens, PEER_SEL), F32)],
        compiler_params=_params("arbitrary"),
        name="peer_topk",
    )(q3, keys)


def _peer_query_kernel(x_ref, w_ref, o_ref):
    acc = jnp.dot(x_ref[...], w_ref[...], preferred_element_type=F32)
    for hd in range(PEER_HEADS):
        o_ref[hd] = acc[:, hd * PEER_QDIM:(hd + 1) * PEER_QDIM]


def peer_query(h, w_q):
    t, d = h.shape
    tm = min(1024, t)
    return pl.pallas_call(
        _peer_query_kernel,
        grid=(t // tm,),
        in_specs=[pl.BlockSpec((tm, d), lambda i: (i, 0)), pl.BlockSpec(w_q.shape, lambda i: (0, 0))],
        out_specs=pl.BlockSpec((PEER_HEADS, tm, PEER_QDIM), lambda i: (0, i, 0)),
        out_shape=jax.ShapeDtypeStruct((PEER_HEADS, t, PEER_QDIM), F32),
        compiler_params=_params("arbitrary"),
        name="peer_query",
    )(h, w_q)


PEER_STEP_TOK = 16
PEER_SLABS = D_MODEL // LANES
PEER_ROWS = PEER_SEL * PEER_SLABS
_ERF_GELU_C = 0.7071067811865476


def _peer_apply_kernel(e_ref, en_ref, h_ref, gw_ref, expand_ref, expand_t_ref, uv_ref, o_ref,
                       buf_a, buf_b, zs, wexp, sem, *, layer):
    i = pl.program_id(0)
    n_steps = pl.num_programs(0)
    ns = PEER_SLABS
    tt = PEER_STEP_TOK
    nt = (((1,), (1,)), ((), ()))

    def gather_copy(idx, buf, n, s):
        return pltpu.make_async_copy(uv_ref.at[layer, idx], buf.at[n], sem.at[s])

    def issue(idx_ref, buf, s):
        for tok in range(tt):
            for k in range(PEER_SEL):
                gather_copy(idx_ref[tok, k], buf, tok * PEER_SEL + k, s).start(priority=k % 2)

    def wait(buf, s):
        pltpu.make_async_copy(buf, buf, sem.at[s]).wait()

    def compute(buf):
        sub = lax.broadcasted_iota(I32, (ns, PEER_ROWS), 0)
        col = lax.broadcasted_iota(I32, (ns, PEER_ROWS), 1)
        diag = (col % ns) == sub
        for t in range(tt):
            u_t = buf[pl.ds(t * PEER_SEL, PEER_SEL), 0].reshape(PEER_ROWS, LANES)
            y = lax.dot_general(h_ref[t], u_t, nt, preferred_element_type=F32)
            zs[pl.ds(t, 1), :] = jnp.sum(jnp.where(diag, y, 0.0), axis=0, keepdims=True)
        z = zs[...]
        z_hi = z.astype(BF16)
        z_lo = (z - z_hi.astype(F32)).astype(BF16)
        act = (jnp.dot(z_hi, expand_t_ref[...], preferred_element_type=F32)
               + jnp.dot(z_lo, expand_t_ref[...], preferred_element_type=F32))
        gelu = 0.5 * act * (1.0 + lax.erf(act * _ERF_GELU_C))
        w = (gw_ref[...] * gelu).astype(BF16)
        wexp[...] = jnp.dot(w, expand_ref[...], preferred_element_type=F32)
        for t in range(tt):
            wrow = wexp[pl.ds(t, 1), :]
            wbig = jnp.where(diag, jnp.broadcast_to(wrow, (ns, PEER_ROWS)), 0.0).astype(BF16)
            v_t = buf[pl.ds(t * PEER_SEL, PEER_SEL), 1].reshape(PEER_ROWS, LANES)
            o = jnp.dot(wbig, v_t, preferred_element_type=F32)
            for s in range(ns):
                o_ref[t:t + 1, s * LANES:(s + 1) * LANES] = o[s:s + 1, :]

    def step(cur, s_cur, nxt, s_nxt):
        wait(cur, s_cur)
        issue(en_ref, nxt, s_nxt)
        compute(cur)

        @pl.when(i == n_steps - 1)
        def _():
            wait(nxt, s_nxt)

    @pl.when(i == 0)
    def _():
        issue(e_ref, buf_a, 0)

    @pl.when(i % 2 == 0)
    def _():
        step(buf_a, 0, buf_b, 1)

    @pl.when(i % 2 == 1)
    def _():
        step(buf_b, 1, buf_a, 0)


def peer_apply(e, h, gw, uv, layer):
    t, d = h.shape
    ns = PEER_SLABS
    tt = PEER_STEP_TOK
    n_steps = t // tt
    h3 = h.reshape(t, ns, LANES)
    group = np.arange(PEER_ROWS) // ns
    expand = jnp.asarray(group[None, :] == np.arange(PEER_SEL)[:, None], BF16)
    out = pl.pallas_call(
        functools.partial(_peer_apply_kernel, layer=layer),
        grid=(n_steps,),
        in_specs=[
            pl.BlockSpec((tt, PEER_SEL), lambda i: (i, 0), memory_space=pltpu.SMEM),
            pl.BlockSpec((tt, PEER_SEL), lambda i: (jnp.minimum(i + 1, n_steps - 1), 0), memory_space=pltpu.SMEM),
            pl.BlockSpec((tt, ns, LANES), lambda i: (i, 0, 0)),
            pl.BlockSpec((tt, PEER_SEL), lambda i: (i, 0)),
            pl.BlockSpec((PEER_SEL, PEER_ROWS), lambda i: (0, 0)),
            pl.BlockSpec((PEER_ROWS, PEER_SEL), lambda i: (0, 0)),
            pl.BlockSpec(memory_space=pl.ANY),
        ],
        out_specs=pl.BlockSpec((tt, d), lambda i: (i, 0)),
        out_shape=jax.ShapeDtypeStruct((t, d), F32),
        scratch_shapes=[
            pltpu.VMEM((tt * PEER_SEL, 2, ns, LANES), BF16),
            pltpu.VMEM((tt * PEER_SEL, 2, ns, LANES), BF16),
            pltpu.VMEM((tt, PEER_ROWS), F32),
            pltpu.VMEM((tt, PEER_ROWS), F32),
            pltpu.SemaphoreType.DMA((2,)),
        ],
        compiler_params=pltpu.CompilerParams(dimension_semantics=("arbitrary",), vmem_limit_bytes=VMEM_LIMIT_BYTES,
                                             disable_bounds_checks=True),
        name="peer_apply",
    )(e, e, h3, gw, expand, expand.T, uv)
    return out


def peer_tables(peer_u, peer_v):
    depth, n_exp, _ = peer_u.shape
    shape = (depth, n_exp, PEER_SLABS, LANES)
    return jnp.stack([peer_u.astype(BF16).reshape(shape), peer_v.astype(BF16).reshape(shape)], axis=2)


def peer_layer(h, w_q, keys, uv, layer):
    q3 = peer_query(h, w_q.astype(BF16))
    e, gw = peer_topk(q3, keys.astype(BF16), h.shape[0])
    return peer_apply(e, h, gw, uv, layer)


def kernel(x_prompt, x_sample, cache_ckv, cache_krope, state_C, state_n, state_m, c, c_ctx, mod_w, mod_b, norm_mix, norm_ffn, mla_w_in, mla_g_q, mla_g_kv, mla_w_uq, mla_g_qn, mla_w_uk, mla_w_uv, mla_g_kn, mla_w_o, ml_w_in, ml_b_gate, ml_g_h, ml_w_o, peer_w_q, peer_keys, peer_u, peer_v):
    d = D_MODEL
    x = (x_prompt.reshape(T_CTX, d), x_sample.reshape(T_LAT, d))
    cvec = jnp.concatenate([c_ctx.reshape(1, d), c, jnp.zeros((GROUP_PAD - N_GROUPS, d), c.dtype)], axis=0)
    mod = mod_vectors(cvec, mod_w, mod_b)
    modrows = mod.reshape(mod.shape[0] * GROUP_PAD * N_MOD, 1, d)
    uv = peer_tables(peer_u, peer_v)

    (h,) = resid_modulate(x, modrows, norm_g=norm_mix[0], shift=(0, 0), scale=(0, 1))
    y, new_ckv, new_krope = mla_layer(h, cache_ckv[:, 0], cache_krope[:, 0], mla_w_in[0], mla_g_q[0], mla_g_kv[0],
                                      mla_w_uq[0], mla_g_qn[0], mla_w_uk[0], mla_w_uv[0], mla_g_kn[0])
    x, h = resid_modulate(x, modrows, y=y, w=mla_w_o[0].astype(BF16), gate=(0, 2),
                          norm_g=norm_ffn[0], shift=(0, 3), scale=(0, 4))
    y = peer_layer(h, peer_w_q[0], peer_keys[0], uv, 0)

    x, h = resid_modulate(x, modrows, y=y, gate=(0, 5), norm_g=norm_mix[1], shift=(1, 0), scale=(1, 1))
    y, new_c, new_n, new_m = mlstm_layer(h, state_C, state_n, state_m, ml_w_in[0], ml_b_gate[0], ml_g_h[0])
    x, h = resid_modulate(x, modrows, y=y, w=ml_w_o[0].astype(BF16), gate=(1, 2),
                          norm_g=norm_ffn[1], shift=(1, 3), scale=(1, 4))
    y = peer_layer(h, peer_w_q[1], peer_keys[1], uv, 1)
    y_prompt, y_sample = resid_modulate(x, modrows, y=y, gate=(1, 5), split_out=True)

    return (y_prompt.reshape(BATCH, SEQ, d), y_sample.reshape(DEC_BATCH, DEC_SEQ, d),
            new_ckv, new_krope, new_c, new_n, new_m)
```

```python
import functools

import numpy as np
import jax
import jax.numpy as jnp
from jax import lax
from jax.experimental import pallas as pl
from jax.experimental.pallas import tpu as pltpu

F32 = jnp.float32
BF16 = jnp.bfloat16
I32 = jnp.int32

D_MODEL = 2048
BATCH, SEQ = 32, 256
DEC_BATCH, DEC_SEQ = 8, 1024
PAST_LEN = 512
GRID_W = 64
N_MOD = 6
NORM_EPS = 1e-6
MLA_HEADS = 16
MLA_Q_RANK = 512
MLA_KV_RANK = 512
MLA_NOPE = 128
MLA_ROPE = 64
MLA_QK_DIM = MLA_NOPE + MLA_ROPE
MLA_HEAD_PAD = 256
ROPE_BASE = 10000.0
ML_HEADS = 8
ML_DV = D_MODEL // ML_HEADS
ML_DK = ML_DV // 2
ML_CHUNK = 64
PEER_HEADS = 8
PEER_NKEYS = 128
PEER_QDIM = 128
PEER_TOPK = 16
PEER_SEL = PEER_HEADS * PEER_TOPK

T_CTX = BATCH * SEQ
T_LAT = DEC_BATCH * DEC_SEQ
T_ALL = T_CTX + T_LAT
N_GROUPS = 1 + DEC_BATCH
GROUP_PAD = 16
ROW_TILE = 256

VMEM_LIMIT_BYTES = 56 * 1024 * 1024
LANES = 128
SUBLANES = 8

NEG_INF = float("-inf")


def _params(*sem):
    return pltpu.CompilerParams(dimension_semantics=sem, vmem_limit_bytes=VMEM_LIMIT_BYTES)


def _mod_kernel(c_ref, w_ref, b_ref, o_ref):
    c = c_ref[...]
    a = (c * jax.nn.sigmoid(c)).astype(BF16)
    o_ref[0] = jnp.dot(a, w_ref[0].astype(BF16), preferred_element_type=F32) + b_ref[0]


def mod_vectors(cvec, mod_w, mod_b):
    depth, d, n = mod_w.shape
    tn = 1024
    return pl.pallas_call(
        _mod_kernel,
        grid=(depth, n // tn),
        in_specs=[
            pl.BlockSpec((GROUP_PAD, d), lambda l, j: (0, 0)),
            pl.BlockSpec((1, d, tn), lambda l, j: (l, 0, j)),
            pl.BlockSpec((1, 1, tn), lambda l, j: (l, 0, j)),
        ],
        out_specs=pl.BlockSpec((1, GROUP_PAD, tn), lambda l, j: (l, 0, j)),
        out_shape=jax.ShapeDtypeStruct((depth, GROUP_PAD, n), F32),
        compiler_params=_params("arbitrary", "arbitrary"),
        name="mod_vectors",
    )(cvec, mod_w, mod_b.reshape(depth, 1, n))


def _group_of_tile(i):
    ctx_tiles = T_CTX // ROW_TILE
    tiles_per_lat = DEC_SEQ // ROW_TILE
    return jnp.where(i < ctx_tiles, 0, 1 + (i - ctx_tiles) // tiles_per_lat)


CTX_TILES = T_CTX // ROW_TILE


def _row_tile(parts, i):
    if len(parts) == 1:
        return parts[0][...]
    return jnp.where(i < CTX_TILES, parts[0][...], parts[1][...])


def _resmod_kernel(*refs, nx, ny, has_w, has_mod, split_out):
    i = pl.program_id(0)
    refs = list(refs)
    x = _row_tile([refs.pop(0) for _ in range(nx)], i)
    if ny:
        y = _row_tile([refs.pop(0) for _ in range(ny)], i)
        if has_w:
            y = jnp.dot(y, refs.pop(0)[...], preferred_element_type=F32)
        gate_ref = refs.pop(0)
        x = x + gate_ref[0] * y
    if has_mod:
        g_ref, sh_ref, sc_ref = refs.pop(0), refs.pop(0), refs.pop(0)
    if ny:
        if split_out:
            xa_ref, xb_ref = refs.pop(0), refs.pop(0)

            @pl.when(i < CTX_TILES)
            def _():
                xa_ref[...] = x

            @pl.when(i >= CTX_TILES)
            def _():
                xb_ref[...] = x
        else:
            refs.pop(0)[...] = x
    if has_mod:
        h_ref = refs.pop(0)
        ms = jnp.mean(x * x, axis=-1, keepdims=True)
        yn = x * lax.rsqrt(ms + NORM_EPS) * g_ref[...]
        h_ref[...] = (yn * (1.0 + sc_ref[0]) + sh_ref[0]).astype(h_ref.dtype)


def resid_modulate(x, modrows, *, y=None, w=None, gate=None, norm_g=None, shift=None, scale=None, split_out=False):
    xs = list(x) if isinstance(x, (tuple, list)) else [x]
    ys = [] if y is None else (list(y) if isinstance(y, (tuple, list)) else [y])
    d = xs[0].shape[1]
    t = sum(a.shape[0] for a in xs)
    has_mod = norm_g is not None
    row_spec = pl.BlockSpec((ROW_TILE, d), lambda i: (i, 0))
    ctx_spec = pl.BlockSpec((ROW_TILE, d), lambda i: (jnp.minimum(i, CTX_TILES - 1), 0))
    lat_spec = pl.BlockSpec((ROW_TILE, d), lambda i: (jnp.maximum(i - CTX_TILES, 0), 0))

    def part_specs(parts):
        return [row_spec] if len(parts) == 1 else [ctx_spec, lat_spec]

    def mod_spec(layer_k):
        layer, k = layer_k
        return pl.BlockSpec((1, 1, d), lambda i: ((layer * GROUP_PAD + _group_of_tile(i)) * N_MOD + k, 0, 0))

    args, in_specs = list(xs), part_specs(xs)
    if ys:
        args += ys
        in_specs += part_specs(ys)
        if w is not None:
            args.append(w)
            in_specs.append(pl.BlockSpec(w.shape, lambda i: (0, 0)))
        args.append(modrows)
        in_specs.append(mod_spec(gate))
    if has_mod:
        args += [norm_g.reshape(1, d), modrows, modrows]
        in_specs += [pl.BlockSpec((1, d), lambda i: (0, 0)), mod_spec(shift), mod_spec(scale)]
    out_shape, out_specs = [], []
    if ys:
        if split_out:
            out_shape += [jax.ShapeDtypeStruct((T_CTX, d), F32), jax.ShapeDtypeStruct((t - T_CTX, d), F32)]
            out_specs += [ctx_spec, lat_spec]
        else:
            out_shape.append(jax.ShapeDtypeStruct((t, d), F32))
            out_specs.append(row_spec)
    if has_mod:
        out_shape.append(jax.ShapeDtypeStruct((t, d), BF16))
        out_specs.append(row_spec)
    outs = pl.pallas_call(
        functools.partial(_resmod_kernel, nx=len(xs), ny=len(ys), has_w=w is not None, has_mod=has_mod,
                          split_out=split_out),
        grid=(t // ROW_TILE,),
        in_specs=in_specs,
        out_specs=out_specs,
        out_shape=out_shape,
        compiler_params=_params("arbitrary"),
        name="resid_modulate",
    )(*args)
    return outs


def _mm_kernel(x_ref, w_ref, o_ref):
    o_ref[...] = jnp.dot(x_ref[...].astype(BF16), w_ref[...], preferred_element_type=F32).astype(o_ref.dtype)


def matmul(x, w, *, tm, tn, out_dtype, n_cols=None):
    m, k = x.shape
    n = w.shape[1] if n_cols is None else n_cols
    return pl.pallas_call(
        _mm_kernel,
        grid=(m // tm, n // tn),
        in_specs=[pl.BlockSpec((tm, k), lambda i, j: (i, 0)), pl.BlockSpec((k, tn), lambda i, j: (0, j))],
        out_specs=pl.BlockSpec((tm, tn), lambda i, j: (i, j)),
        out_shape=jax.ShapeDtypeStruct((m, n), out_dtype),
        compiler_params=_params("arbitrary", "arbitrary"),
        name="matmul",
    )(x, w)


def _rms(x, n):
    return lax.rsqrt(jnp.sum(x * x, axis=-1, keepdims=True) / n + NORM_EPS)


def _mla_in_kernel(h_ref, w_ref, gq_ref, gkv_ref, cq_ref, ckv_ref, kr_ref):
    a = jnp.dot(h_ref[...], w_ref[...], preferred_element_type=F32)
    cq = a[:, :MLA_Q_RANK]
    ckv = a[:, MLA_Q_RANK:MLA_Q_RANK + MLA_KV_RANK]
    cq_ref[...] = (cq * _rms(cq, MLA_Q_RANK) * gq_ref[...]).astype(cq_ref.dtype)
    ckv_ref[...] = ckv * _rms(ckv, MLA_KV_RANK) * gkv_ref[...]
    kr_ref[...] = a[:, MLA_Q_RANK + MLA_KV_RANK:]


def mla_in_proj(h, w_ext, g_q, g_kv):
    t, d = h.shape
    n = w_ext.shape[1]
    tm = 512
    return pl.pallas_call(
        _mla_in_kernel,
        grid=(t // tm,),
        in_specs=[
            pl.BlockSpec((tm, d), lambda i: (i, 0)),
            pl.BlockSpec((d, n), lambda i: (0, 0)),
            pl.BlockSpec((1, MLA_Q_RANK), lambda i: (0, 0)),
            pl.BlockSpec((1, MLA_KV_RANK), lambda i: (0, 0)),
        ],
        out_specs=[
            pl.BlockSpec((tm, MLA_Q_RANK), lambda i: (i, 0)),
            pl.BlockSpec((tm, MLA_KV_RANK), lambda i: (i, 0)),
            pl.BlockSpec((tm, LANES), lambda i: (i, 0)),
        ],
        out_shape=[
            jax.ShapeDtypeStruct((t, MLA_Q_RANK), BF16),
            jax.ShapeDtypeStruct((t, MLA_KV_RANK), F32),
            jax.ShapeDtypeStruct((t, LANES), F32),
        ],
        compiler_params=_params("arbitrary"),
        name="mla_in_proj",
    )(h, w_ext, g_q.reshape(1, -1), g_kv.reshape(1, -1))


def _rope_block(x, c, s1, s2):
    return x * c + pltpu.roll(x, 96, 1) * s1 + pltpu.roll(x, 32, 1) * s2


def _q_up_kernel(cq_ref, w_ref, g_ref, c_ref, s1_ref, s2_ref, q_ref, *, heads):
    a = jnp.dot(cq_ref[...], w_ref[...], preferred_element_type=F32)
    g = g_ref[...]
    for hh in range(heads):
        base = hh * MLA_HEAD_PAD
        nope = a[:, base:base + MLA_NOPE]
        rp = a[:, base + MLA_NOPE:base + MLA_HEAD_PAD]
        ss = jnp.sum(nope * nope, axis=-1, keepdims=True) + jnp.sum(rp * rp, axis=-1, keepdims=True)
        r = lax.rsqrt(ss / MLA_QK_DIM + NORM_EPS)
        xr = _rope_block(rp * r * g[:, MLA_NOPE:], c_ref[...], s1_ref[...], s2_ref[...])
        q_ref[:, base:base + MLA_NOPE] = (nope * r * g[:, :MLA_NOPE]).astype(q_ref.dtype)
        q_ref[:, base + MLA_NOPE:base + MLA_HEAD_PAD] = xr.astype(q_ref.dtype)


def mla_q_up(cq, w_uq_ext, g_qn_ext, rope_c, rope_s1, rope_s2):
    t, r = cq.shape
    n = w_uq_ext.shape[1]
    tm, heads = 512, 4
    tn = heads * MLA_HEAD_PAD
    tab = pl.BlockSpec((tm, LANES), lambda i, j: (i, 0))
    return pl.pallas_call(
        functools.partial(_q_up_kernel, heads=heads),
        grid=(t // tm, n // tn),
        in_specs=[
            pl.BlockSpec((tm, r), lambda i, j: (i, 0)),
            pl.BlockSpec((r, tn), lambda i, j: (0, j)),
            pl.BlockSpec((1, MLA_HEAD_PAD), lambda i, j: (0, 0)),
            tab, tab, tab,
        ],
        out_specs=pl.BlockSpec((tm, tn), lambda i, j: (i, j)),
        out_shape=jax.ShapeDtypeStruct((t, n), BF16),
        compiler_params=_params("arbitrary", "arbitrary"),
        name="mla_q_up",
    )(cq, w_uq_ext, g_qn_ext, rope_c, rope_s1, rope_s2)


def _kv_up_kernel(ckv_ref, kr_ref, wk_ref, wv_ref, g_ref, c_ref, s1_ref, s2_ref, k_ref, v_ref, *, heads):
    ckv = ckv_ref[...].astype(BF16)
    kn = jnp.dot(ckv, wk_ref[...], preferred_element_type=F32)
    v_ref[...] = jnp.dot(ckv, wv_ref[...], preferred_element_type=F32).astype(v_ref.dtype)
    g = g_ref[...]
    lane = lax.broadcasted_iota(I32, (1, LANES), 1)
    kr = jnp.where(lane < MLA_ROPE, kr_ref[...], 0.0)
    ss_r = jnp.sum(kr * kr, axis=-1, keepdims=True)
    krot = _rope_block(kr * g[:, MLA_NOPE:], c_ref[...], s1_ref[...], s2_ref[...])
    for hh in range(heads):
        nope = kn[:, hh * MLA_NOPE:(hh + 1) * MLA_NOPE]
        r = lax.rsqrt((jnp.sum(nope * nope, axis=-1, keepdims=True) + ss_r) / MLA_QK_DIM + NORM_EPS)
        base = hh * MLA_HEAD_PAD
        k_ref[:, base:base + MLA_NOPE] = (nope * r * g[:, :MLA_NOPE]).astype(k_ref.dtype)
        k_ref[:, base + MLA_NOPE:base + MLA_HEAD_PAD] = (krot * r).astype(k_ref.dtype)


def mla_kv_up(ckv, kr, w_uk, w_uv, g_kn_ext, rope_c, rope_s1, rope_s2):
    t, r = ckv.shape
    tm, heads = 512, 4
    tab = pl.BlockSpec((tm, LANES), lambda i, j: (i, 0))
    return pl.pallas_call(
        functools.partial(_kv_up_kernel, heads=heads),
        grid=(t // tm, MLA_HEADS // heads),
        in_specs=[
            pl.BlockSpec((tm, r), lambda i, j: (i, 0)),
            tab,
            pl.BlockSpec((r, heads * MLA_NOPE), lambda i, j: (0, j)),
            pl.BlockSpec((r, heads * MLA_NOPE), lambda i, j: (0, j)),
            pl.BlockSpec((1, MLA_HEAD_PAD), lambda i, j: (0, 0)),
            tab, tab, tab,
        ],
        out_specs=[
            pl.BlockSpec((tm, heads * MLA_HEAD_PAD), lambda i, j: (i, j)),
            pl.BlockSpec((tm, heads * MLA_NOPE), lambda i, j: (i, j)),
        ],
        out_shape=[
            jax.ShapeDtypeStruct((t, MLA_HEADS * MLA_HEAD_PAD), BF16),
            jax.ShapeDtypeStruct((t, MLA_HEADS * MLA_NOPE), BF16),
        ],
        compiler_params=_params("arbitrary", "arbitrary"),
        name="mla_kv_up",
    )(ckv, kr, w_uk, w_uv, g_kn_ext, rope_c, rope_s1, rope_s2)


ATTN_HEADS_PER_STEP = 4
_LOG2_E = 1.4426950408889634


def _attn_kernel(*refs, nseg):
    q_ref = refs[0]
    k_refs = refs[1:1 + nseg]
    v_refs = refs[1 + nseg:1 + 2 * nseg]
    o_ref = refs[-1]
    scale = MLA_QK_DIM ** -0.5
    nt = (((1,), (1,)), ((), ()))
    for hh in range(ATTN_HEADS_PER_STEP):
        qk_cols = slice(hh * MLA_HEAD_PAD, (hh + 1) * MLA_HEAD_PAD)
        v_cols = slice(hh * MLA_NOPE, (hh + 1) * MLA_NOPE)
        q = q_ref[:, qk_cols]
        s = [lax.dot_general(q, k[:, qk_cols], nt, preferred_element_type=F32) for k in k_refs]
        m = functools.reduce(jnp.maximum, [jnp.max(x, axis=-1, keepdims=True) for x in s])
        e = [jnp.exp2((x - m) * (scale * _LOG2_E)) for x in s]
        inv = 1.0 / functools.reduce(lambda a, b: a + b, [jnp.sum(x, axis=-1, keepdims=True) for x in e])
        o = functools.reduce(
            lambda a, b: a + b,
            [jnp.dot(x.astype(BF16), v[:, v_cols], preferred_element_type=F32) for x, v in zip(e, v_refs)])
        o_ref[:, v_cols] = (o * inv).astype(o_ref.dtype)


def mla_attention(q, k, v, *, q_row0, n_batch, s_q, segs):
    tq = 256
    nq = s_q // tq
    nseg = len(segs)
    q_blk0 = q_row0 // tq
    hp = ATTN_HEADS_PER_STEP
    in_specs = [pl.BlockSpec((tq, hp * MLA_HEAD_PAD), lambda b, h, i: (q_blk0 + b * nq + i, h))]
    for row0, length in segs:
        in_specs.append(pl.BlockSpec((length, hp * MLA_HEAD_PAD), lambda b, h, i, o=row0 // length: (o + b, h)))
    for row0, length in segs:
        in_specs.append(pl.BlockSpec((length, hp * MLA_NOPE), lambda b, h, i, o=row0 // length: (o + b, h)))
    return pl.pallas_call(
        functools.partial(_attn_kernel, nseg=nseg),
        grid=(n_batch, MLA_HEADS // hp, nq),
        in_specs=in_specs,
        out_specs=pl.BlockSpec((tq, hp * MLA_NOPE), lambda b, h, i: (b * nq + i, h)),
        out_shape=jax.ShapeDtypeStruct((n_batch * s_q, MLA_HEADS * MLA_NOPE), BF16),
        compiler_params=_params("arbitrary", "arbitrary", "arbitrary"),
        name="mla_attention",
    )(q, *([k] * nseg), *([v] * nseg))


def _rope_tables():
    nf = MLA_ROPE // 4
    inv_freq = jnp.power(ROPE_BASE, -jnp.arange(nf, dtype=F32) / nf)
    tok = jnp.arange(DEC_SEQ)
    row = (tok // GRID_W).astype(F32)[:, None] * inv_freq[None, :]
    col = (tok % GRID_W).astype(F32)[:, None] * inv_freq[None, :]
    ang = jnp.concatenate([row, col], axis=-1)
    cos, sin = jnp.cos(ang), jnp.sin(ang)
    z32 = jnp.zeros_like(cos)
    z64 = jnp.zeros((DEC_SEQ, 64), F32)
    c = jnp.concatenate([cos, cos, z64], axis=-1)
    s1 = jnp.concatenate([-sin, z32, z64], axis=-1)
    s2 = jnp.concatenate([z32, sin, z64], axis=-1)
    ident_c = jnp.concatenate([jnp.ones((1, 64), F32), jnp.zeros((1, 64), F32)], axis=-1)

    def full(lat, ident):
        n_cache = DEC_BATCH * PAST_LEN
        return jnp.concatenate([
            jnp.broadcast_to(ident, (T_CTX, LANES)),
            jnp.tile(lat, (DEC_BATCH, 1)),
            jnp.broadcast_to(ident, (n_cache, LANES)),
        ], axis=0)

    zero = jnp.zeros((1, LANES), F32)
    return full(c, ident_c), full(s1, zero), full(s2, zero)


_ROPE_PERM = np.concatenate([np.arange(0, 16), np.arange(32, 48), np.arange(16, 32), np.arange(48, 64)])


def mla_layer(h, cache_ckv, cache_krope, w_in, g_q, g_kv, w_uq, g_qn, w_uk, w_uv, g_kn):
    d = h.shape[1]
    perm = _ROPE_PERM
    n_lat = MLA_Q_RANK + MLA_KV_RANK
    w_in_ext = jnp.concatenate([w_in[:, :n_lat], w_in[:, n_lat:][:, perm], w_in[:, n_lat:]], axis=1).astype(BF16)
    cq, ckv, kr = mla_in_proj(h, w_in_ext, g_q, g_kv)

    w3 = w_uq.reshape(MLA_Q_RANK, MLA_HEADS, MLA_QK_DIM)
    w_uq_ext = jnp.concatenate([
        w3[:, :, :MLA_NOPE], w3[:, :, MLA_NOPE:][:, :, perm],
        jnp.zeros((MLA_Q_RANK, MLA_HEADS, MLA_HEAD_PAD - MLA_QK_DIM), w_uq.dtype)], axis=-1)
    w_uq_ext = w_uq_ext.reshape(MLA_Q_RANK, MLA_HEADS * MLA_HEAD_PAD).astype(BF16)

    def gain_ext(g):
        return jnp.concatenate([g[:MLA_NOPE], g[MLA_NOPE:][perm],
                                jnp.zeros((MLA_HEAD_PAD - MLA_QK_DIM,), g.dtype)]).reshape(1, MLA_HEAD_PAD)

    rope_c, rope_s1, rope_s2 = _rope_tables()
    q = mla_q_up(cq, w_uq_ext, gain_ext(g_qn), rope_c[:T_ALL], rope_s1[:T_ALL], rope_s2[:T_ALL])

    n_cache = DEC_BATCH * PAST_LEN
    ckv_all = jnp.concatenate([ckv, cache_ckv.reshape(n_cache, MLA_KV_RANK)], axis=0)
    kr_cache = cache_krope.reshape(n_cache, MLA_ROPE)
    kr_cache = jnp.concatenate([kr_cache[:, perm], kr_cache], axis=1)
    kr_all = jnp.concatenate([kr, kr_cache], axis=0)
    k, v = mla_kv_up(ckv_all, kr_all, w_uk.astype(BF16), w_uv.astype(BF16), gain_ext(g_kn),
                     rope_c, rope_s1, rope_s2)

    o_ctx = mla_attention(q, k, v, q_row0=0, n_batch=BATCH, s_q=SEQ, segs=[(0, SEQ)])
    o_lat = mla_attention(q, k, v, q_row0=T_CTX, n_batch=DEC_BATCH, s_q=DEC_SEQ,
                          segs=[(T_ALL, PAST_LEN), (T_CTX, DEC_SEQ)])
    y = (o_ctx, o_lat)
    new_ckv = ckv[:T_CTX].reshape(BATCH, 1, SEQ, MLA_KV_RANK)
    new_krope = kr[:T_CTX, MLA_ROPE:].reshape(BATCH, 1, SEQ, MLA_ROPE)
    return y, new_ckv, new_krope


def _log_sigmoid(x):
    return jnp.minimum(x, 0.0) - jnp.log(1.0 + jnp.exp(-jnp.abs(x)))


def _gates_kernel(h_ref, wg_ref, wgt_ref, b_ref, bt_ref, gc_ref, gr_ref):
    h = h_ref[...]
    gc = jnp.dot(h, wg_ref[...], preferred_element_type=F32) + b_ref[...]
    gr = lax.dot_general(wgt_ref[...], h, (((1,), (1,)), ((), ())), preferred_element_type=F32) + bt_ref[...]
    lane = lax.broadcasted_iota(I32, (1, LANES), 1)
    is_f = ((lane >= ML_HEADS) & (lane < 2 * ML_HEADS)) | ((lane >= 3 * ML_HEADS) & (lane < 4 * ML_HEADS))
    gc_ref[...] = jnp.where(is_f, _log_sigmoid(gc), gc)
    row = lax.broadcasted_iota(I32, (4 * ML_HEADS, 1), 0)
    is_fr = ((row >= ML_HEADS) & (row < 2 * ML_HEADS)) | ((row >= 3 * ML_HEADS) & (row < 4 * ML_HEADS))
    gr_ref[...] = jnp.where(is_fr, _log_sigmoid(gr), gr)


def mlstm_gates(h, w_g, b_gate):
    t, d = h.shape
    ng = 4 * ML_HEADS
    tm = 512
    wg = jnp.concatenate([w_g, jnp.zeros((d, LANES - ng), w_g.dtype)], axis=1).astype(BF16)
    wgt = w_g.T.astype(BF16)
    b = jnp.concatenate([b_gate, jnp.zeros((LANES - ng,), b_gate.dtype)]).reshape(1, LANES)
    bt = b_gate.reshape(ng, 1)
    return pl.pallas_call(
        _gates_kernel,
        grid=(t // tm,),
        in_specs=[
            pl.BlockSpec((tm, d), lambda i: (i, 0)),
            pl.BlockSpec((d, LANES), lambda i: (0, 0)),
            pl.BlockSpec((ng, d), lambda i: (0, 0)),
            pl.BlockSpec((1, LANES), lambda i: (0, 0)),
            pl.BlockSpec((ng, 1), lambda i: (0, 0)),
        ],
        out_specs=[pl.BlockSpec((tm, LANES), lambda i: (i, 0)), pl.BlockSpec((ng, tm), lambda i: (0, i))],
        out_shape=[jax.ShapeDtypeStruct((t, LANES), F32), jax.ShapeDtypeStruct((ng, t), F32)],
        compiler_params=_params("arbitrary"),
        name="mlstm_gates",
    )(h, wg, wgt, b, bt)


ML_HEADS_PER_STEP = 2


def _mlstm_kernel(*refs, nc, has_init, emit_state):
    refs = list(refs)
    q_ref, k_ref, v_ref, o_ref, gc_ref, gr_ref, gh_ref = [refs.pop(0) for _ in range(7)]
    if has_init:
        c0_ref, n0_ref, m0_ref = [refs.pop(0) for _ in range(3)]
    y_ref = refs.pop(0)
    if emit_state:
        cf_ref, nf_ref, mf_ref = [refs.pop(0) for _ in range(3)]
    n_chain = 2 * ML_HEADS_PER_STEP
    mem_s, nrm_s, m_s, hs_s, dmat_s, qk_s, cols_s = (refs[j * n_chain:(j + 1) * n_chain] for j in range(7))

    L = ML_CHUNK
    tt = lax.broadcasted_iota(I32, (L, L), 0)
    ss = lax.broadcasted_iota(I32, (L, L), 1)
    q_scale = ML_DK ** -0.5
    nt = (((1,), (1,)), ((), ()))
    tn = (((0,), (0,)), ((), ()))

    for hh in range(ML_HEADS_PER_STEP):
        for d in range(2):
            ch = 2 * hh + d
            if has_init:
                mem_s[ch][...] = c0_ref[0, d, hh]
                nrm_s[ch][...] = n0_ref[0, d, hh]
                m_s[ch][...] = m0_ref[0, d, hh]
            else:
                mem_s[ch][...] = jnp.zeros(mem_s[ch].shape, F32)
                nrm_s[ch][...] = jnp.zeros(nrm_s[ch].shape, F32)
                m_s[ch][...] = jnp.zeros(m_s[ch].shape, F32)

    def gate_part(hh, d, c):
        ch = 2 * hh + d
        kq = slice(hh * ML_DK, (hh + 1) * ML_DK)
        causal = (ss <= tt) if d == 0 else (ss >= tt)
        causal_t = (tt <= ss) if d == 0 else (tt >= ss)
        gcol = gc_ref[0, hh, c]
        grow = gr_ref[0, hh, c]
        f_col = gcol[:, 2 * d + 1:2 * d + 2]
        i_row = grow[2 * d:2 * d + 1, :]
        f_row = grow[2 * d + 1:2 * d + 2, :]
        cum_col = jnp.sum(jnp.where(causal, jnp.broadcast_to(f_row, (L, L)), 0.0), axis=1, keepdims=True)
        cum_row = jnp.sum(jnp.where(causal_t, jnp.broadcast_to(f_col, (L, L)), 0.0), axis=0, keepdims=True)
        dmat = jnp.where(causal, cum_col - cum_row + i_row, NEG_INF)
        rows = pl.ds(pl.multiple_of(c * L, L), L)
        qb = (q_ref[rows, kq] * q_scale).astype(BF16)
        dmat_s[ch][c] = dmat
        qk_s[ch][c] = lax.dot_general(qb, k_ref[rows, kq].astype(BF16), nt, preferred_element_type=F32)
        cols_s[ch][c, :, 0:1] = cum_col
        cols_s[ch][c, :, 1:2] = jnp.max(dmat, axis=1, keepdims=True)

    def chunk(hh, d, c):
        ch = 2 * hh + d
        kq = slice(hh * ML_DK, (hh + 1) * ML_DK)
        kv = slice(hh * ML_DV, (hh + 1) * ML_DV)
        last = L - 1 if d == 0 else 0
        i_col = gc_ref[0, hh, c][:, 2 * d:2 * d + 1]
        cols = cols_s[ch][c]
        cum_col = cols[:, 0:1]
        max_d = cols[:, 1:2]
        dmat = dmat_s[ch][c]
        total = cum_col[last:last + 1, :]
        m_prev = m_s[ch][:, 0:1]
        inter = cum_col + m_prev
        m_t = jnp.maximum(inter, max_d)
        w_inter = jnp.exp(inter - m_t)
        rows = pl.ds(pl.multiple_of(c * L, L), L)
        qf = q_ref[rows, kq] * q_scale
        kf = k_ref[rows, kq]
        vb = v_ref[rows, kv].astype(BF16)
        qb = qf.astype(BF16)
        a = jnp.exp(dmat - m_t) * qk_s[ch][c]
        mem = mem_s[ch][...]
        nrm = nrm_s[ch][...]
        num = (w_inter * jnp.dot(qb, mem.astype(BF16), preferred_element_type=F32)
               + jnp.dot(a.astype(BF16), vb, preferred_element_type=F32))
        den = w_inter * jnp.sum(qf * nrm, axis=1, keepdims=True) + jnp.sum(a, axis=1, keepdims=True)
        hs_s[ch][rows, :] = num / jnp.maximum(jnp.abs(den), jnp.exp(-m_t))
        m_new = m_t[last:last + 1, :]
        decay = jnp.exp(total + m_prev - m_new)
        w_s = jnp.exp(total - cum_col + i_col - m_new)
        wk = w_s * kf
        mem_s[ch][...] = decay * mem + lax.dot_general(wk.astype(BF16), vb, tn, preferred_element_type=F32)
        nrm_s[ch][...] = decay * nrm + jnp.sum(wk, axis=0, keepdims=True)
        m_s[ch][...] = jnp.broadcast_to(m_new, m_s[ch].shape)

    def all_gate_parts(c, carry):
        for hh in range(ML_HEADS_PER_STEP):
            gate_part(hh, 0, c)
            gate_part(hh, 1, c)
        return carry

    def all_chains(ci, carry):
        for hh in range(ML_HEADS_PER_STEP):
            chunk(hh, 0, ci)
            chunk(hh, 1, nc - 1 - ci)
        return carry

    lax.fori_loop(0, nc, all_gate_parts, 0)
    lax.fori_loop(0, nc, all_chains, 0)
    for hh in range(ML_HEADS_PER_STEP):
        if emit_state:
            for d in range(2):
                cf_ref[0, d, hh] = mem_s[2 * hh + d][...]
                nf_ref[0, d, hh] = nrm_s[2 * hh + d][...]
                mf_ref[0, d, hh] = m_s[2 * hh + d][...]
        kv = slice(hh * ML_DV, (hh + 1) * ML_DV)
        hs = hs_s[2 * hh][...] + hs_s[2 * hh + 1][...]
        hn = hs * lax.rsqrt(jnp.mean(hs * hs, axis=-1, keepdims=True) + NORM_EPS) * gh_ref[:, kv]
        y_ref[:, kv] = (hn * jax.nn.sigmoid(o_ref[:, kv])).astype(y_ref.dtype)


def mlstm_scan(p, gcol, grow, g_h, *, row0, n_batch, seq, state=None, emit_state=False):
    nc = seq // ML_CHUNK
    rb0 = row0 // seq
    hps = ML_HEADS_PER_STEP
    groups = ML_HEADS // hps
    has_init = state is not None
    in_specs = [
        pl.BlockSpec((seq, hps * ML_DK), lambda b, h: (rb0 + b, h)),
        pl.BlockSpec((seq, hps * ML_DK), lambda b, h: (rb0 + b, groups + h)),
        pl.BlockSpec((seq, hps * ML_DV), lambda b, h: (rb0 + b, groups + h)),
        pl.BlockSpec((seq, hps * ML_DV), lambda b, h: (rb0 + b, 2 * groups + h)),
        pl.BlockSpec((1, hps, nc, ML_CHUNK, 4), lambda b, h: (b, h, 0, 0, 0)),
        pl.BlockSpec((1, hps, nc, 4, ML_CHUNK), lambda b, h: (b, h, 0, 0, 0)),
        pl.BlockSpec((1, hps * ML_DV), lambda b, h: (0, h)),
    ]
    args = [p, p, p, p, gcol, grow, g_h.reshape(1, -1)]
    c_spec = pl.BlockSpec((1, 2, hps, ML_DK, ML_DV), lambda b, h: (b, 0, h, 0, 0))
    n_spec = pl.BlockSpec((1, 2, hps, 1, ML_DK), lambda b, h: (b, 0, h, 0, 0))
    m_spec = pl.BlockSpec((1, 2, hps, 1, LANES), lambda b, h: (b, 0, h, 0, 0))
    if has_init:
        in_specs += [c_spec, n_spec, m_spec]
        args += list(state)
    out_specs = [pl.BlockSpec((seq, hps * ML_DV), lambda b, h: (b, h))]
    out_shape = [jax.ShapeDtypeStruct((n_batch * seq, ML_HEADS * ML_DV), BF16)]
    if emit_state:
        out_specs += [c_spec, n_spec, m_spec]
        out_shape += [
            jax.ShapeDtypeStruct((n_batch, 2, ML_HEADS, ML_DK, ML_DV), F32),
            jax.ShapeDtypeStruct((n_batch, 2, ML_HEADS, 1, ML_DK), F32),
            jax.ShapeDtypeStruct((n_batch, 2, ML_HEADS, 1, LANES), F32),
        ]
    return pl.pallas_call(
        functools.partial(_mlstm_kernel, nc=nc, has_init=has_init, emit_state=emit_state),
        grid=(n_batch, groups),
        in_specs=in_specs,
        out_specs=out_specs,
        out_shape=out_shape,
        scratch_shapes=(
            [pltpu.VMEM((ML_DK, ML_DV), F32)] * (2 * hps) + [pltpu.VMEM((1, ML_DK), F32)] * (2 * hps)
            + [pltpu.VMEM((1, LANES), F32)] * (2 * hps) + [pltpu.VMEM((seq, ML_DV), F32)] * (2 * hps)
            + [pltpu.VMEM((nc, ML_CHUNK, ML_CHUNK), F32)] * (4 * hps) + [pltpu.VMEM((nc, ML_CHUNK, 2), F32)] * (2 * hps)),
        compiler_params=_params("arbitrary", "arbitrary"),
        name="mlstm_scan",
    )(*args)


def _gate_layouts(gc, gr, row0, n_batch, seq):
    nc = seq // ML_CHUNK
    n = n_batch * seq
    gcol = gc[row0:row0 + n, :4 * ML_HEADS].reshape(n_batch, nc, ML_CHUNK, 4, ML_HEADS).transpose(0, 4, 1, 2, 3)
    grow = gr[:, row0:row0 + n].reshape(4, ML_HEADS, n_batch, nc, ML_CHUNK).transpose(2, 1, 3, 0, 4)
    return gcol, grow


def mlstm_layer(h, state_c, state_n, state_m, w_in, b_gate, g_h):
    hk = ML_HEADS * ML_DK
    hv = ML_HEADS * ML_DV
    n_main = 2 * hk + 2 * hv
    p = matmul(h, w_in.astype(BF16), tm=1024, tn=1024 + 512, out_dtype=F32, n_cols=n_main)
    gc, gr = mlstm_gates(h, w_in[:, n_main:], b_gate)

    gcol, grow = _gate_layouts(gc, gr, 0, BATCH, SEQ)
    y_ctx, cf, nf, mf = mlstm_scan(p, gcol, grow, g_h, row0=0, n_batch=BATCH, seq=SEQ, emit_state=True)

    gcol, grow = _gate_layouts(gc, gr, T_CTX, DEC_BATCH, DEC_SEQ)
    c0 = state_c[:, 0]
    n0 = state_n[:, 0].reshape(DEC_BATCH, 2, ML_HEADS, 1, ML_DK)
    m0 = jnp.broadcast_to(state_m[:, 0].reshape(DEC_BATCH, 2, ML_HEADS, 1, 1), (DEC_BATCH, 2, ML_HEADS, 1, LANES))
    (y_lat,) = mlstm_scan(p, gcol, grow, g_h, row0=T_CTX, n_batch=DEC_BATCH, seq=DEC_SEQ, state=(c0, n0, m0))

    y = (y_ctx, y_lat)
    new_c = cf.reshape(BATCH, 1, 2, ML_HEADS, ML_DK, ML_DV)
    new_n = nf.reshape(BATCH, 1, 2, ML_HEADS, ML_DK)
    new_m = mf[..., 0, 0].reshape(BATCH, 1, 2, ML_HEADS)
    return y, new_c, new_n, new_m


def _topk_rows(x, payload, n_out):
    return _topk_rows_multi([x], [payload], n_out)[0]


def _topk_rows_multi(xs, payloads, n_out):
    rows = float(xs[0].shape[0])
    iota = lax.broadcasted_iota(I32, xs[0].shape, 0).astype(F32)
    xs = list(xs)
    vals = [[] for _ in xs]
    outs = [[] for _ in xs]
    for _ in range(n_out):
        for j, payload in enumerate(payloads):
            x = xs[j]
            m = jnp.max(x, axis=0, keepdims=True)
            pos = jnp.min(jnp.where(x == m, iota, rows), axis=0, keepdims=True)
            sel = iota == pos
            vals[j].append(m)
            outs[j].append(pos if payload is None else jnp.max(jnp.where(sel, payload, -1.0), axis=0, keepdims=True))
            xs[j] = jnp.where(sel, NEG_INF, x)
    return [(jnp.concatenate(v, axis=0), jnp.concatenate(o, axis=0)) for v, o in zip(vals, outs)]


def _pair_candidates(a, b, combine, fill):
    k = PEER_TOPK
    row = lax.broadcasted_iota(I32, (SUBLANES, a.shape[1]), 0)
    blocks = [combine(a[0:1, :], b), combine(a[1:2, :], b[0:SUBLANES, :])]
    for i in range(2, SUBLANES):
        blocks.append(jnp.where(row < k // (i + 1), combine(a[i:i + 1, :], b[0:SUBLANES, :]), fill))
    blocks.append(combine(a[SUBLANES:k, :], b[0:1, :]))
    return jnp.concatenate(blocks, axis=0)


PEER_TILE = 128


def _peer_head_topk(qh, keys_ref):
    half = PEER_QDIM // 2
    nt = (((1,), (1,)), ((), ()))
    sc = [lax.dot_general(keys_ref[p], qh[:, p * half:(p + 1) * half].astype(BF16), nt,
                          preferred_element_type=F32) for p in range(2)]
    (sv0, si0), (sv1, si1) = _topk_rows_multi(sc, [None, None], PEER_TOPK)
    sv, si = [sv0, sv1], [si0, si1]
    cand = _pair_candidates(sv[0], sv[1], lambda x, y: x + y, NEG_INF)
    cidx = _pair_candidates(si[0], si[1], lambda x, y: x * float(PEER_NKEYS) + y, -1.0)
    best, eidx = _topk_rows(cand, cidx, PEER_TOPK)
    ex = jnp.exp(best - best[0:1, :])
    return eidx.astype(I32), ex / jnp.sum(ex, axis=0, keepdims=True)


def _peer_topk_kernel(q_ref, keys_ref, e_ref, gw_ref):
    parts = [_peer_head_topk(q_ref[hd], keys_ref) for hd in range(PEER_HEADS)]
    e_ref[...] = jnp.concatenate([p[0] for p in parts], axis=0).T
    gw_ref[...] = jnp.concatenate([p[1] for p in parts], axis=0).T


def peer_topk(q3, keys, n_tokens):
    tt = PEER_TILE
    return pl.pallas_call(
        _peer_topk_kernel,
        grid=(n_tokens // tt,),
        in_specs=[pl.BlockSpec((PEER_HEADS, tt, PEER_QDIM), lambda i: (0, i, 0)),
                  pl.BlockSpec(keys.shape, lambda i: (0, 0, 0))],
        out_specs=[pl.BlockSpec((tt, PEER_SEL), lambda i: (i, 0)), pl.BlockSpec((tt, PEER_SEL), lambda i: (i, 0))],
        out_shape=[jax.ShapeDtypeStruct((n_tokens, PEER_SEL), I32), jax.ShapeDtypeStruct((n_tokens, PEER_SEL), F32)],
        compiler_params=_params("arbitrary"),
        name="peer_topk",
    )(q3, keys)


def _peer_query_kernel(x_ref, w_ref, o_ref):
    acc = jnp.dot(x_ref[...], w_ref[...], preferred_element_type=F32)
    for hd in range(PEER_HEADS):
        o_ref[hd] = acc[:, hd * PEER_QDIM:(hd + 1) * PEER_QDIM]


def peer_query(h, w_q):
    t, d = h.shape
    tm = min(1024, t)
    return pl.pallas_call(
        _peer_query_kernel,
        grid=(t // tm,),
        in_specs=[pl.BlockSpec((tm, d), lambda i: (i, 0)), pl.BlockSpec(w_q.shape, lambda i: (0, 0))],
        out_specs=pl.BlockSpec((PEER_HEADS, tm, PEER_QDIM), lambda i: (0, i, 0)),
        out_shape=jax.ShapeDtypeStruct((PEER_HEADS, t, PEER_QDIM), F32),
        compiler_params=_params("arbitrary"),
        name="peer_query",
    )(h, w_q)


PEER_STEP_TOK = 16
PEER_SLABS = D_MODEL // LANES
PEER_ROWS = PEER_SEL * PEER_SLABS
_ERF_GELU_C = 0.7071067811865476


def _peer_apply_kernel(e_ref, en_ref, h_ref, gw_ref, expand_ref, expand_t_ref, uv_ref, o_ref,
                       buf_a, buf_b, zs, wexp, sem, *, layer):
    i = pl.program_id(0)
    n_steps = pl.num_programs(0)
    ns = PEER_SLABS
    tt = PEER_STEP_TOK
    nt = (((1,), (1,)), ((), ()))

    def gather_copy(idx, buf, n, s):
        return pltpu.make_async_copy(uv_ref.at[layer, idx], buf.at[n], sem.at[s])

    def issue(idx_ref, buf, s):
        for tok in range(tt):
            for k in range(PEER_SEL):
                gather_copy(idx_ref[tok, k], buf, tok * PEER_SEL + k, s).start(priority=k % 2)

    def wait(buf, s):
        pltpu.make_async_copy(buf, buf, sem.at[s]).wait()

    def compute(buf):
        sub = lax.broadcasted_iota(I32, (ns, PEER_ROWS), 0)
        col = lax.broadcasted_iota(I32, (ns, PEER_ROWS), 1)
        diag = (col % ns) == sub
        for t in range(tt):
            u_t = buf[pl.ds(t * PEER_SEL, PEER_SEL), 0].reshape(PEER_ROWS, LANES)
            y = lax.dot_general(h_ref[t], u_t, nt, preferred_element_type=F32)
            zs[pl.ds(t, 1), :] = jnp.sum(jnp.where(diag, y, 0.0), axis=0, keepdims=True)
        z = zs[...]
        z_hi = z.astype(BF16)
        z_lo = (z - z_hi.astype(F32)).astype(BF16)
        act = (jnp.dot(z_hi, expand_t_ref[...], preferred_element_type=F32)
               + jnp.dot(z_lo, expand_t_ref[...], preferred_element_type=F32))
        gelu = 0.5 * act * (1.0 + lax.erf(act * _ERF_GELU_C))
        w = (gw_ref[...] * gelu).astype(BF16)
        wexp[...] = jnp.dot(w, expand_ref[...], preferred_element_type=F32)
        for t in range(tt):
            wrow = wexp[pl.ds(t, 1), :]
            wbig = jnp.where(diag, jnp.broadcast_to(wrow, (ns, PEER_ROWS)), 0.0).astype(BF16)
            v_t = buf[pl.ds(t * PEER_SEL, PEER_SEL), 1].reshape(PEER_ROWS, LANES)
            o = jnp.dot(wbig, v_t, preferred_element_type=F32)
            for s in range(ns):
                o_ref[t:t + 1, s * LANES:(s + 1) * LANES] = o[s:s + 1, :]

    def step(cur, s_cur, nxt, s_nxt):
        wait(cur, s_cur)
        issue(en_ref, nxt, s_nxt)
        compute(cur)

        @pl.when(i == n_steps - 1)
        def _():
            wait(nxt, s_nxt)

    @pl.when(i == 0)
    def _():
        issue(e_ref, buf_a, 0)

    @pl.when(i % 2 == 0)
    def _():
        step(buf_a, 0, buf_b, 1)

    @pl.when(i % 2 == 1)
    def _():
        step(buf_b, 1, buf_a, 0)


def peer_apply(e, h, gw, uv, layer):
    t, d = h.shape
    ns = PEER_SLABS
    tt = PEER_STEP_TOK
    n_steps = t // tt
    h3 = h.reshape(t, ns, LANES)
    group = np.arange(PEER_ROWS) // ns
    expand = jnp.asarray(group[None, :] == np.arange(PEER_SEL)[:, None], BF16)
    out = pl.pallas_call(
        functools.partial(_peer_apply_kernel, layer=layer),
        grid=(n_steps,),
        in_specs=[
            pl.BlockSpec((tt, PEER_SEL), lambda i: (i, 0), memory_space=pltpu.SMEM),
            pl.BlockSpec((tt, PEER_SEL), lambda i: (jnp.minimum(i + 1, n_steps - 1), 0), memory_space=pltpu.SMEM),
            pl.BlockSpec((tt, ns, LANES), lambda i: (i, 0, 0)),
            pl.BlockSpec((tt, PEER_SEL), lambda i: (i, 0)),
            pl.BlockSpec((PEER_SEL, PEER_ROWS), lambda i: (0, 0)),
            pl.BlockSpec((PEER_ROWS, PEER_SEL), lambda i: (0, 0)),
            pl.BlockSpec(memory_space=pl.ANY),
        ],
        out_specs=pl.BlockSpec((tt, d), lambda i: (i, 0)),
        out_shape=jax.ShapeDtypeStruct((t, d), F32),
        scratch_shapes=[
            pltpu.VMEM((tt * PEER_SEL, 2, ns, LANES), BF16),
            pltpu.VMEM((tt * PEER_SEL, 2, ns, LANES), BF16),
            pltpu.VMEM((tt, PEER_ROWS), F32),
            pltpu.VMEM((tt, PEER_ROWS), F32),
            pltpu.SemaphoreType.DMA((2,)),
        ],
        compiler_params=pltpu.CompilerParams(dimension_semantics=("arbitrary",), vmem_limit_bytes=VMEM_LIMIT_BYTES,
                                             disable_bounds_checks=True),
        name="peer_apply",
    )(e, e, h3, gw, expand, expand.T, uv)
    return out


PEER_PACK_ROWS = 128


def _peer_pack_kernel(u_ref, v_ref, o_ref):
    ub = u_ref[0].astype(BF16)
    vb = v_ref[0].astype(BF16)
    for s in range(PEER_SLABS):
        o_ref[0, :, 0, s, :] = ub[:, s * LANES:(s + 1) * LANES]
        o_ref[0, :, 1, s, :] = vb[:, s * LANES:(s + 1) * LANES]


def peer_tables(peer_u, peer_v):
    depth, n_exp, d = peer_u.shape
    rows = PEER_PACK_ROWS
    tab_spec = pl.BlockSpec((1, rows, d), lambda l, i: (l, i, 0))
    return pl.pallas_call(
        _peer_pack_kernel,
        grid=(depth, n_exp // rows),
        in_specs=[tab_spec, tab_spec],
        out_specs=pl.BlockSpec((1, rows, 2, PEER_SLABS, LANES), lambda l, i: (l, i, 0, 0, 0)),
        out_shape=jax.ShapeDtypeStruct((depth, n_exp, 2, PEER_SLABS, LANES), BF16),
        compiler_params=_params("arbitrary", "arbitrary"),
        name="peer_tables",
    )(peer_u, peer_v)


def peer_layer(h, w_q, keys, uv, layer):
    q3 = peer_query(h, w_q.astype(BF16))
    e, gw = peer_topk(q3, keys.astype(BF16), h.shape[0])
    return peer_apply(e, h, gw, uv, layer)


def kernel(x_prompt, x_sample, cache_ckv, cache_krope, state_C, state_n, state_m, c, c_ctx, mod_w, mod_b, norm_mix, norm_ffn, mla_w_in, mla_g_q, mla_g_kv, mla_w_uq, mla_g_qn, mla_w_uk, mla_w_uv, mla_g_kn, mla_w_o, ml_w_in, ml_b_gate, ml_g_h, ml_w_o, peer_w_q, peer_keys, peer_u, peer_v):
    d = D_MODEL
    x = (x_prompt.reshape(T_CTX, d), x_sample.reshape(T_LAT, d))
    cvec = jnp.concatenate([c_ctx.reshape(1, d), c, jnp.zeros((GROUP_PAD - N_GROUPS, d), c.dtype)], axis=0)
    mod = mod_vectors(cvec, mod_w, mod_b)
    modrows = mod.reshape(mod.shape[0] * GROUP_PAD * N_MOD, 1, d)
    uv = peer_tables(peer_u, peer_v)

    (h,) = resid_modulate(x, modrows, norm_g=norm_mix[0], shift=(0, 0), scale=(0, 1))
    y, new_ckv, new_krope = mla_layer(h, cache_ckv[:, 0], cache_krope[:, 0], mla_w_in[0], mla_g_q[0], mla_g_kv[0],
                                      mla_w_uq[0], mla_g_qn[0], mla_w_uk[0], mla_w_uv[0], mla_g_kn[0])
    x, h = resid_modulate(x, modrows, y=y, w=mla_w_o[0].astype(BF16), gate=(0, 2),
                          norm_g=norm_ffn[0], shift=(0, 3), scale=(0, 4))
    y = peer_layer(h, peer_w_q[0], peer_keys[0], uv, 0)

    x, h = resid_modulate(x, modrows, y=y, gate=(0, 5), norm_g=norm_mix[1], shift=(1, 0), scale=(1, 1))
    y, new_c, new_n, new_m = mlstm_layer(h, state_C, state_n, state_m, ml_w_in[0], ml_b_gate[0], ml_g_h[0])
    x, h = resid_modulate(x, modrows, y=y, w=ml_w_o[0].astype(BF16), gate=(1, 2),
                          norm_g=norm_ffn[1], shift=(1, 3), scale=(1, 4))
    y = peer_layer(h, peer_w_q[1], peer_keys[1], uv, 1)
    y_prompt, y_sample = resid_modulate(x, modrows, y=y, gate=(1, 5), split_out=True)

    return (y_prompt.reshape(BATCH, SEQ, d), y_sample.reshape(DEC_BATCH, DEC_SEQ, d),
            new_ckv, new_krope, new_c, new_n, new_m)
```
